```python
import math
import jax
import jax.numpy as jnp
from jax import lax
import numpy as np

D_MODEL = 1024
BATCH = 1
SEQ = 16384
DEPTH = 2

GRID_W = 64
CTX_LEN = 256
EPS = 1e-6
POOL_GROUPS = 4
POOL_WINDOWS = (2, 4, 8, 16)
POOL_WIDTH = D_MODEL // 4
POOL_GROUP_DIM = POOL_WIDTH // POOL_GROUPS
HEAD_DIM = 128
N_Q_HEADS = (D_MODEL - POOL_WIDTH) // HEAD_DIM
N_KV_HEADS = 2
ATT_WIDTH = N_Q_HEADS * HEAD_DIM
KV_WIDTH = N_KV_HEADS * HEAD_DIM
Q_BLOCK = 128
ROPE_THETA = 10000.0
ROPE_FREQS = HEAD_DIM // 4
EVEN_SPLITS = (POOL_WIDTH, 2 * POOL_WIDTH, 2 * POOL_WIDTH + ATT_WIDTH,
               2 * POOL_WIDTH + ATT_WIDTH + KV_WIDTH, 2 * POOL_WIDTH + ATT_WIDTH + 2 * KV_WIDTH)
EVEN_IN = 2 * POOL_WIDTH + 2 * ATT_WIDTH + 2 * KV_WIDTH
EVEN_MIX = POOL_WIDTH + ATT_WIDTH
HY_ORDER = 2
HY_WIDTH = 3 * D_MODEL // 4
HY_EMB = 33
HY_BANDS = (HY_EMB - 1) // 2
HY_HIDDEN = 64
HY_DECAY_TARGET = 1e-2
HY_FAST_DECAY = 0.3
HY_SLOW_DECAY = 1.5
FN_WIDTH = D_MODEL - HY_WIDTH
ODD_SPLITS = ((HY_ORDER + 1) * HY_WIDTH, (HY_ORDER + 2) * HY_WIDTH,
              (HY_ORDER + 2) * HY_WIDTH + FN_WIDTH)
ODD_IN = (HY_ORDER + 2) * HY_WIDTH + 2 * FN_WIDTH
ODD_MIX = HY_WIDTH + FN_WIDTH
N_EVEN = (DEPTH + 1) // 2
N_ODD = DEPTH // 2

kernel_name = "hybrid_pool_attn_hyena_fourier_dit"


def rmsnorm(x, g):
    xf = x.astype(jnp.float32)
    y = xf * lax.rsqrt(jnp.mean(xf * xf, axis=-1, keepdims=True) + EPS)
    return y.astype(x.dtype) * g


def modulation(cond, w_mod, b_mod):
    m = jax.nn.silu(cond) @ w_mod + b_mod
    return jnp.split(m, 3, axis=-1)


def to_heads(t, n_heads):
    return t.reshape(t.shape[:2] + (n_heads, HEAD_DIM))


def axial_rope_tables(row_idx, col_idx):
    inv_freq = ROPE_THETA ** (-jnp.arange(ROPE_FREQS, dtype=jnp.float32) / ROPE_FREQS)
    ang = jnp.stack([row_idx[:, None] * inv_freq, col_idx[:, None] * inv_freq], axis=1)
    ang = ang[:, None, :, None, :]
    return jnp.cos(ang), jnp.sin(ang)


def apply_rope(x, cos, sin):
    xr = x.reshape(x.shape[:-1] + (2, 2, ROPE_FREQS)).astype(jnp.float32)
    rot = jnp.concatenate([-xr[..., 1:, :], xr[..., :1, :]], axis=-2)
    return (xr * cos + rot * sin).astype(x.dtype).reshape(x.shape)


def blocked_attention(q, k, v):
    b, lq, hq, dh = q.shape
    hkv = k.shape[2]
    grp = hq // hkv
    nb = lq // Q_BLOCK
    qb = q.reshape(b, nb, Q_BLOCK, hkv, grp, dh).transpose(1, 0, 2, 3, 4, 5)
    scale = dh ** -0.5

    def one_block(qblk):
        s = jnp.einsum('bqhgd,bkhd->bhgqk', qblk, k, preferred_element_type=jnp.float32) * scale
        p = jax.nn.softmax(s, axis=-1).astype(v.dtype)
        return jnp.einsum('bhgqk,bkhd->bqhgd', p, v)

    o = lax.map(one_block, qb)
    return o.transpose(1, 0, 2, 3, 4, 5).reshape(b, lq, hq * dh)


def pool_mixer(u, w_grp, scale):
    b, L, _ = u.shape
    ug = u.reshape(b, L, POOL_GROUPS, POOL_GROUP_DIM).astype(jnp.float32)
    cs = jnp.concatenate([jnp.zeros((b, 1, POOL_GROUPS, POOL_GROUP_DIM), jnp.float32),
                          jnp.cumsum(ug, axis=1)], axis=1)
    t = jnp.arange(L)
    outs = []
    for gi, w in enumerate(POOL_WINDOWS):
        lo = jnp.clip(t - w // 2, 0, L)
        hi = jnp.clip(t + w // 2, 0, L)
        csg = cs[:, :, gi]
        win_sum = jnp.take(csg, hi, axis=1) - jnp.take(csg, lo, axis=1)
        cnt = (hi - lo).astype(jnp.float32)[None, :, None]
        outs.append(win_sum / cnt - ug[:, :, gi])
    d = jnp.stack(outs, axis=2).astype(u.dtype)
    y = jnp.einsum('blgc,gcd->blgd', d, w_grp)
    return y.reshape(b, L, POOL_WIDTH) * scale


def even_mix(a_val, a_gate, q, b_gate, k_all, v_all, pool_w, pool_scale, w_out):
    ya = pool_mixer(a_val, pool_w, pool_scale)
    yb = blocked_attention(q, k_all, v_all)
    y = jnp.concatenate([ya * jax.nn.silu(a_gate), yb * jax.nn.silu(b_gate)], axis=-1)
    return y @ w_out


def short_conv3(u, w, b):
    up = jnp.pad(u, ((0, 0), (1, 1), (0, 0)))
    return up[:, :-2] * w[0] + up[:, 1:-1] * w[1] + up[:, 2:] * w[2] + b


def hyena_filters(L, w1, b1, w2, b2, w3, freq):
    f32 = jnp.float32
    t = jnp.linspace(0.0, 1.0, L, dtype=f32)[:, None]
    w = 2.0 * math.pi * jnp.arange(L, dtype=f32)[:, None] / L
    f = jnp.linspace(1e-4, HY_BANDS - 1, HY_BANDS, dtype=f32)[None, :]
    emb = jnp.concatenate([t, jnp.cos(f * w), -jnp.sin(f * w)], axis=-1)
    fr = freq.astype(f32)
    hdn = jnp.sin(fr * (emb @ w1.astype(f32) + b1.astype(f32)))
    hdn = jnp.sin(fr * (hdn @ w2.astype(f32) + b2.astype(f32)))
    h = (hdn @ w3.astype(f32)).reshape(L, HY_ORDER, 2, HY_WIDTH)
    max_decay = math.log(HY_DECAY_TARGET) / HY_FAST_DECAY
    min_decay = math.log(HY_DECAY_TARGET) / HY_SLOW_DECAY
    deltas = jnp.linspace(min_decay, max_decay, HY_WIDTH, dtype=f32)
    decay = jnp.exp(-t * jnp.abs(deltas))
    h = h * decay[:, None, None, :]
    h = h / (jnp.sum(jnp.abs(h), axis=(0, 2), keepdims=True) + EPS)
    zero = jnp.zeros((1, HY_ORDER, HY_WIDTH), f32)
    return jnp.concatenate([h[:, :, 0], zero, h[:0:-1, :, 1]], axis=0)


def fft_long_conv(z, kfull):
    L = z.shape[1]
    zf = jnp.fft.rfft(z, n=2 * L, axis=1)
    kf = jnp.fft.rfft(kfull, n=2 * L, axis=0)
    return jnp.fft.irfft(zf * kf[None], n=2 * L, axis=1)[:, :L]


def hyena_mixer(u, conv_w, conv_b, w1, b1, w2, b2, w3, freq, skip):
    L = u.shape[1]
    uc = short_conv3(u, conv_w, conv_b)
    v, x1, x2 = jnp.split(uc, HY_ORDER + 1, axis=-1)
    k = hyena_filters(L, w1, b1, w2, b2, w3, freq)
    z = v.astype(jnp.float32)
    for o, gate_o in enumerate((x1, x2)):
        z = gate_o.astype(jnp.float32) * (fft_long_conv(z, k[:, o]) + skip[o].astype(jnp.float32) * z)
    return z.astype(u.dtype)


def fourier_mixer(u, w):
    y = jnp.fft.fft2(u.astype(jnp.float32), axes=(1, 2), norm='ortho').real
    return y.astype(u.dtype) @ w


def odd_mix(h, w_in, w_out, fn_w, conv_w, conv_b, w1, b1, w2, b2, w3, freq, skip):
    hy_in, hy_gate, fn_in, fn_gate = jnp.split(h @ w_in, ODD_SPLITS, axis=-1)
    yc = hyena_mixer(hy_in, conv_w, conv_b, w1, b1, w2, b2, w3, freq, skip)
    yd = fourier_mixer(fn_in, fn_w)
    y = jnp.concatenate([yc * jax.nn.silu(hy_gate), yd * jax.nn.silu(fn_gate)], axis=-1)
    return y @ w_out


def setup_inputs(seed: int = 0) -> dict:
    key = jax.random.key(seed)
    ks = jax.random.split(key, 26)

    def nrm(k, shape, s):
        return jax.random.normal(k, shape, jnp.float32) * s

    return {
        'x': nrm(ks[0], (BATCH, SEQ, D_MODEL), 1.0),
        'c': nrm(ks[1], (BATCH, D_MODEL), 1.0),
        'ctx': nrm(ks[2], (BATCH, CTX_LEN, D_MODEL), 1.0),
        'c_ctx': nrm(ks[3], (D_MODEL,), 1.0),
        'w_mod': nrm(ks[4], (DEPTH, D_MODEL, 3 * D_MODEL), 0.5 * D_MODEL ** -0.5),
        'b_mod': nrm(ks[5], (DEPTH, 3 * D_MODEL), 0.01),
        'norm_g': 1.0 + nrm(ks[6], (DEPTH, D_MODEL), 0.05),
        'ev_w_in': nrm(ks[7], (N_EVEN, D_MODEL, EVEN_IN), D_MODEL ** -0.5),
        'ev_w_out': nrm(ks[8], (N_EVEN, EVEN_MIX, D_MODEL), EVEN_MIX ** -0.5),
        'pool_w': nrm(ks[9], (N_EVEN, POOL_GROUPS, POOL_GROUP_DIM, POOL_GROUP_DIM), POOL_GROUP_DIM ** -0.5),
        'pool_scale': 1.0 + nrm(ks[10], (N_EVEN, POOL_WIDTH), 0.1),
        'q_norm_g': 1.0 + nrm(ks[11], (N_EVEN, HEAD_DIM), 0.05),
        'k_norm_g': 1.0 + nrm(ks[12], (N_EVEN, HEAD_DIM), 0.05),
        'od_w_in': nrm(ks[13], (N_ODD, D_MODEL, ODD_IN), D_MODEL ** -0.5),
        'od_w_out': nrm(ks[14], (N_ODD, ODD_MIX, D_MODEL), ODD_MIX ** -0.5),
        'hy_conv_w': nrm(ks[15], (N_ODD, 3, (HY_ORDER + 1) * HY_WIDTH), 3 ** -0.5),
        'hy_conv_b': nrm(ks[16], (N_ODD, (HY_ORDER + 1) * HY_WIDTH), 0.01),
        'hy_w1': nrm(ks[17], (N_ODD, HY_EMB, HY_HIDDEN), HY_EMB ** -0.5),
        'hy_b1': nrm(ks[18], (N_ODD, HY_HIDDEN), 0.1),
        'hy_w2': nrm(ks[19], (N_ODD, HY_HIDDEN, HY_HIDDEN), HY_HIDDEN ** -0.5),
        'hy_b2': nrm(ks[20], (N_ODD, HY_HIDDEN), 0.1),
        'hy_w3': nrm(ks[21], (N_ODD, HY_HIDDEN, HY_ORDER * 2 * HY_WIDTH), HY_HIDDEN ** -0.5),
        'hy_freq': 1.0 + nrm(ks[22], (N_ODD, HY_HIDDEN), 0.1),
        'hy_skip': nrm(ks[23], (N_ODD, HY_ORDER, HY_WIDTH), 1.0),
        'fn_w': nrm(ks[24], (N_ODD, FN_WIDTH, FN_WIDTH), FN_WIDTH ** -0.5),
        'final_g': 1.0 + nrm(ks[25], (D_MODEL,), 0.05),
    }


def reference(x, c, ctx, c_ctx, w_mod, b_mod, norm_g, ev_w_in, ev_w_out, pool_w, pool_scale,
              q_norm_g, k_norm_g, od_w_in, od_w_out, hy_conv_w, hy_conv_b, hy_w1, hy_b1, hy_w2, hy_b2,
              hy_w3, hy_freq, hy_skip, fn_w, final_g):
    n_tok = x.shape[1]
    rows = n_tok // GRID_W
    row_idx = jnp.broadcast_to(jnp.arange(rows, dtype=jnp.float32)[:, None], (rows, GRID_W)).reshape(-1)
    col_idx = jnp.broadcast_to(jnp.arange(GRID_W, dtype=jnp.float32)[None, :], (rows, GRID_W)).reshape(-1)
    cos, sin = axial_rope_tables(row_idx, col_idx)

    for i in range(DEPTH):
        li = i // 2
        is_even = i % 2 == 0
        ctx_needed = any(j % 2 == 0 for j in range(i + 1, DEPTH))
        shift, scale, gate = modulation(c[:, None, :], w_mod[i], b_mod[i])
        h = rmsnorm(x, norm_g[i]) * (1 + scale) + shift
        if is_even or ctx_needed:
            cshift, cscale, cgate = modulation(c_ctx[None, None, :], w_mod[i], b_mod[i])
            hc = rmsnorm(ctx, norm_g[i]) * (1 + cscale) + cshift
        if is_even:
            w_in = ev_w_in[li]
            a_val, a_gate, q, k, v, b_gate = jnp.split(h @ w_in, EVEN_SPLITS, axis=-1)
            q = apply_rope(rmsnorm(to_heads(q, N_Q_HEADS), q_norm_g[li]), cos, sin)
            k = apply_rope(rmsnorm(to_heads(k, N_KV_HEADS), k_norm_g[li]), cos, sin)
            if ctx_needed:
                ca_val, ca_gate, cq, ck, cv, cb_gate = jnp.split(hc @ w_in, EVEN_SPLITS, axis=-1)
            else:
                ck, cv = jnp.split(hc @ w_in[:, EVEN_SPLITS[2]:EVEN_SPLITS[4]], 2, axis=-1)
            ck = rmsnorm(to_heads(ck, N_KV_HEADS), k_norm_g[li])
            cv = to_heads(cv, N_KV_HEADS)
            k_all = jnp.concatenate([ck, k], axis=1)
            v_all = jnp.concatenate([cv, to_heads(v, N_KV_HEADS)], axis=1)
            y = even_mix(a_val, a_gate, q, b_gate, k_all, v_all, pool_w[li], pool_scale[li], ev_w_out[li])
            if ctx_needed:
                cq = rmsnorm(to_heads(cq, N_Q_HEADS), q_norm_g[li])
                yc = even_mix(ca_val, ca_gate, cq, cb_gate, ck, cv, pool_w[li], pool_scale[li], ev_w_out[li])
                ctx = ctx + cgate * yc
            x = x + gate * y
        else:
            y = odd_mix(h, od_w_in[li], od_w_out[li], fn_w[li], hy_conv_w[li], hy_conv_b[li],
                        hy_w1[li], hy_b1[li], hy_w2[li], hy_b2[li], hy_w3[li], hy_freq[li], hy_skip[li])
            if ctx_needed:
                yc = odd_mix(hc, od_w_in[li], od_w_out[li], fn_w[li], hy_conv_w[li], hy_conv_b[li],
                             hy_w1[li], hy_b1[li], hy_w2[li], hy_b2[li], hy_w3[li], hy_freq[li], hy_skip[li])
                ctx = ctx + cgate * yc
            x = x + gate * y
    return rmsnorm(x, final_g)
```

```python
import functools
import math

import numpy as np
import jax
import jax.numpy as jnp
from jax import lax
from jax.experimental import pallas as pl
from jax.experimental.pallas import tpu as pltpu

F32 = jnp.float32
BF16 = jnp.bfloat16
HIGHEST = lax.Precision.HIGHEST

EPS = 1e-6
GRID_W = 64
HEAD_DIM = 128
ROPE_FREQS = 32
ROPE_THETA = 10000.0
N_Q_HEADS = 6
N_KV_HEADS = 2
Q_PER_KV = N_Q_HEADS // N_KV_HEADS
POOL_WIDTH = 256
POOL_GROUP_DIM = 64
POOL_WINDOWS = (2, 4, 8, 16)
POOL_HALO = 8
ATT_WIDTH = N_Q_HEADS * HEAD_DIM
KV_WIDTH = N_KV_HEADS * HEAD_DIM
HY_WIDTH = 768
HY_ORDER = 2
HY_EMB = 33
HY_BANDS = 16
HY_HIDDEN = 64
FN_WIDTH = 256
HY_DECAY_TARGET = 1e-2
HY_FAST_DECAY = 0.3
HY_SLOW_DECAY = 1.5

LANES = 128
DFT_N1 = 64
VMEM_LIMIT = 56 * 1024 * 1024


def _row_tile(n, pref):
    t = min(pref, n)
    assert n % t == 0
    return t


def _silu(x):
    return x * jax.nn.sigmoid(x)


def _params(sem, vmem=None):
    return pltpu.CompilerParams(dimension_semantics=sem, vmem_limit_bytes=vmem)


def _const_spec(shape):
    nd = len(shape)
    return pl.BlockSpec(shape, lambda *_: (0,) * nd, pipeline_mode=pl.Buffered(1))


def _mod_kernel(cond_ref, w_ref, b_ref, o_ref):
    s = _silu(cond_ref[...])
    o_ref[0] = jnp.dot(s, w_ref[0], precision=HIGHEST, preferred_element_type=F32) + b_ref[0]


def _modulation(cond, w_mod, b_mod):
    depth, d, d3 = w_mod.shape
    tn = 1024
    return pl.pallas_call(
        _mod_kernel,
        out_shape=jax.ShapeDtypeStruct((depth, 8, d3), F32),
        grid=(depth, d3 // tn),
        in_specs=[pl.BlockSpec((8, d), lambda i, j: (0, 0)),
                  pl.BlockSpec((1, d, tn), lambda i, j: (i, 0, j)),
                  pl.BlockSpec((1, 1, tn), lambda i, j: (i, 0, j))],
        out_specs=pl.BlockSpec((1, 8, tn), lambda i, j: (i, 0, j)),
        compiler_params=_params(("arbitrary", "arbitrary")),
        name="modulation",
    )(cond, w_mod, b_mod.reshape(depth, 1, d3))


def _norm_mod(x, g, scale, shift):
    y = x * lax.rsqrt(jnp.mean(x * x, axis=-1, keepdims=True) + EPS)
    return (y * g) * (1.0 + scale) + shift


def _even_in_kernel(x_ref, g_ref, sc_ref, sh_ref, w_ref, qg_ref, kg_ref, cos_ref, sin_ref,
                    aval_ref, agate_ref, q_ref, k_ref, v_ref, bgate_ref):
    h = _norm_mod(x_ref[...], g_ref[...], sc_ref[...], sh_ref[...])
    p = jnp.dot(h.astype(BF16), w_ref[...], preferred_element_type=F32)
    aval_ref[...] = p[:, 0:POOL_WIDTH]
    agate_ref[...] = _silu(p[:, POOL_WIDTH:2 * POOL_WIDTH])
    cos = cos_ref[...]
    sin = sin_ref[...]
    lane = lax.broadcasted_iota(jnp.int32, cos.shape, 1)
    low_half = (lane % (2 * ROPE_FREQS)) < ROPE_FREQS

    def head(xh, g):
        y = xh * lax.rsqrt(jnp.mean(xh * xh, axis=-1, keepdims=True) + EPS) * g
        rot = jnp.where(low_half, pltpu.roll(y, HEAD_DIM - ROPE_FREQS, 1), pltpu.roll(y, ROPE_FREQS, 1))
        return y * cos + rot * sin

    q0 = 2 * POOL_WIDTH
    for hq in range(N_Q_HEADS):
        sl = slice(hq * HEAD_DIM, (hq + 1) * HEAD_DIM)
        q_ref[:, sl] = head(p[:, q0 + hq * HEAD_DIM:q0 + (hq + 1) * HEAD_DIM], qg_ref[...]).astype(BF16)
    k0 = q0 + ATT_WIDTH
    for hk in range(N_KV_HEADS):
        sl = slice(hk * HEAD_DIM, (hk + 1) * HEAD_DIM)
        k_ref[:, sl] = head(p[:, k0 + hk * HEAD_DIM:k0 + (hk + 1) * HEAD_DIM], kg_ref[...]).astype(BF16)
    v0 = k0 + KV_WIDTH
    v_ref[...] = p[:, v0:v0 + KV_WIDTH].astype(BF16)
    b0 = v0 + KV_WIDTH
    bgate_ref[...] = _silu(p[:, b0:b0 + ATT_WIDTH])


def _even_in(x, g, scale, shift, w_bf, qg, kg, cos, sin):
    n, d = x.shape
    n_in = w_bf.shape[1]
    tm = _row_tile(n, 512)
    row = lambda w: pl.BlockSpec((tm, w), lambda i: (i, 0))
    vec = lambda w: pl.BlockSpec((1, w), lambda i: (0, 0))
    return pl.pallas_call(
        _even_in_kernel,
        out_shape=(jax.ShapeDtypeStruct((n, POOL_WIDTH), F32), jax.ShapeDtypeStruct((n, POOL_WIDTH), F32),
                   jax.ShapeDtypeStruct((n, ATT_WIDTH), BF16), jax.ShapeDtypeStruct((n, KV_WIDTH), BF16),
                   jax.ShapeDtypeStruct((n, KV_WIDTH), BF16), jax.ShapeDtypeStruct((n, ATT_WIDTH), F32)),
        grid=(n // tm,),
        in_specs=[row(d), vec(d), vec(d), vec(d), _const_spec((d, n_in)), vec(HEAD_DIM), vec(HEAD_DIM),
                  row(HEAD_DIM), row(HEAD_DIM)],
        out_specs=(row(POOL_WIDTH), row(POOL_WIDTH), row(ATT_WIDTH), row(KV_WIDTH), row(KV_WIDTH),
                   row(ATT_WIDTH)),
        compiler_params=_params(("parallel",), VMEM_LIMIT),
        name="even_in",
    )(x, g, scale, shift, w_bf, qg, kg, cos, sin)


def _attn_kernel(q_ref, ck_ref, cv_ref, k_ref, v_ref, bg_ref, o_ref, *, tq, tk, n_kv_tiles):
    scale = HEAD_DIM ** -0.5
    rows = Q_PER_KV * tq
    for h in range(N_KV_HEADS):
        hs = slice(h * HEAD_DIM, (h + 1) * HEAD_DIM)
        qs = jnp.concatenate(
            [q_ref[:, (Q_PER_KV * h + g) * HEAD_DIM:(Q_PER_KV * h + g + 1) * HEAD_DIM] for g in range(Q_PER_KV)],
            axis=0)

        def step(kt, vt, m, l, acc):
            s = lax.dot_general(qs, kt, (((1,), (1,)), ((), ())), preferred_element_type=F32) * scale
            m_new = jnp.maximum(m, jnp.max(s, axis=-1, keepdims=True))
            alpha = jnp.exp(m - m_new)
            p = jnp.exp(s - m_new)
            l = alpha * l + jnp.sum(p, axis=-1, keepdims=True)
            acc = alpha * acc + jnp.dot(p.astype(BF16), vt, preferred_element_type=F32)
            return m_new, l, acc

        m0 = jnp.full((rows, 1), -1e30, F32)
        l0 = jnp.zeros((rows, 1), F32)
        a0 = jnp.zeros((rows, HEAD_DIM), F32)
        carry = step(ck_ref[:, hs], cv_ref[:, hs], m0, l0, a0)

        def body(i, c):
            r0 = pl.multiple_of(i * tk, tk)
            return step(k_ref[pl.ds(r0, tk), hs], v_ref[pl.ds(r0, tk), hs], *c)

        m, l, acc = lax.fori_loop(0, n_kv_tiles, body, carry)
        out = acc / l
        for g in range(Q_PER_KV):
            cs = slice((Q_PER_KV * h + g) * HEAD_DIM, (Q_PER_KV * h + g + 1) * HEAD_DIM)
            o_ref[:, cs] = (out[g * tq:(g + 1) * tq] * bg_ref[:, cs]).astype(o_ref.dtype)


def _attention(q, ck, cv, k, v, bgate):
    n = q.shape[0]
    nc = ck.shape[0]
    tq = _row_tile(n, 256)
    tk = _row_tile(n, 512)
    kern = functools.partial(_attn_kernel, tq=tq, tk=tk, n_kv_tiles=n // tk)
    return pl.pallas_call(
        kern,
        out_shape=jax.ShapeDtypeStruct((n, ATT_WIDTH), F32),
        grid=(n // tq,),
        in_specs=[pl.BlockSpec((tq, ATT_WIDTH), lambda i: (i, 0)),
                  _const_spec((nc, KV_WIDTH)), _const_spec((nc, KV_WIDTH)),
                  _const_spec((n, KV_WIDTH)), _const_spec((n, KV_WIDTH)),
                  pl.BlockSpec((tq, ATT_WIDTH), lambda i: (i, 0))],
        out_specs=pl.BlockSpec((tq, ATT_WIDTH), lambda i: (i, 0)),
        compiler_params=_params(("parallel",), VMEM_LIMIT),
        name="attention",
    )(q, ck, cv, k, v, bgate)


def _even_out_kernel(a_ref, ap_ref, an_ref, ag_ref, yb_ref, x_ref, gate_ref, pw_ref, ps_ref, wo_ref, o_ref,
                     *, tm, n_rows):
    i = pl.program_id(0)
    n_tiles = pl.num_programs(0)
    u = a_ref[...]
    prev = jnp.where(i > 0, ap_ref[...], 0.0)
    nxt = jnp.where(i < n_tiles - 1, an_ref[...], 0.0)
    e = jnp.concatenate([prev, u, nxt], axis=0)
    n = tm + 2 * POOL_HALO
    s2 = e[0:n - 1] + e[1:n]
    s4 = s2[0:n - 3] + s2[2:n - 1]
    s8 = s4[0:n - 7] + s4[4:n - 3]
    s16 = s8[0:n - 15] + s8[8:n - 7]
    lane = lax.broadcasted_iota(jnp.int32, (tm, POOL_WIDTH), 1)
    grp = lane // POOL_GROUP_DIM
    win = jnp.where(grp == 0, s2[7:7 + tm],
                    jnp.where(grp == 1, s4[6:6 + tm], jnp.where(grp == 2, s8[4:4 + tm], s16[0:tm])))
    half = jnp.where(grp == 0, 1, jnp.where(grp == 1, 2, jnp.where(grp == 2, 4, 8)))
    t = i * tm + lax.broadcasted_iota(jnp.int32, (tm, POOL_WIDTH), 0)
    cnt = (jnp.minimum(t + half, n_rows) - jnp.maximum(t - half, 0)).astype(F32)
    d = win / cnt - u
    ya = jnp.dot(d.astype(BF16), pw_ref[...], preferred_element_type=F32) * ps_ref[...]
    ya = ya * ag_ref[...]
    y = jnp.dot(ya.astype(BF16), wo_ref[0:POOL_WIDTH, :], preferred_element_type=F32)
    y = y + jnp.dot(yb_ref[...].astype(BF16), wo_ref[POOL_WIDTH:, :], preferred_element_type=F32)
    o_ref[...] = x_ref[...] + gate_ref[...] * y


def _even_out(a_val, a_gate, yb, x, gate, pool_bd, pool_scale, w_out_bf):
    n, d = x.shape
    tm = _row_tile(n, 512)
    hb = tm // POOL_HALO
    last = n // POOL_HALO - 1
    row = lambda w: pl.BlockSpec((tm, w), lambda i: (i, 0))
    vec = lambda w: pl.BlockSpec((1, w), lambda i: (0, 0))
    kern = functools.partial(_even_out_kernel, tm=tm, n_rows=n)
    return pl.pallas_call(
        kern,
        out_shape=jax.ShapeDtypeStruct((n, d), F32),
        grid=(n // tm,),
        in_specs=[row(POOL_WIDTH),
                  pl.BlockSpec((POOL_HALO, POOL_WIDTH), lambda i: (jnp.maximum(i * hb - 1, 0), 0)),
                  pl.BlockSpec((POOL_HALO, POOL_WIDTH), lambda i: (jnp.minimum((i + 1) * hb, last), 0)),
                  row(POOL_WIDTH), row(ATT_WIDTH), row(d), vec(d),
                  _const_spec((POOL_WIDTH, POOL_WIDTH)), vec(POOL_WIDTH), _const_spec((d, d))],
        out_specs=row(d),
        compiler_params=_params(("parallel",), VMEM_LIMIT),
        name="even_out",
    )(a_val, a_val, a_val, a_gate, yb, x, gate, pool_bd, pool_scale, w_out_bf)


def _odd_in_kernel(x_ref, g_ref, sc_ref, sh_ref, w_ref, hy_ref, hyg_ref, fn_ref, fng_ref):
    h = _norm_mod(x_ref[...], g_ref[...], sc_ref[...], sh_ref[...])
    p = jnp.dot(h.astype(BF16), w_ref[...], preferred_element_type=F32)
    c0 = (HY_ORDER + 1) * HY_WIDTH
    hy_ref[...] = p[:, 0:c0]
    hyg_ref[...] = _silu(p[:, c0:c0 + HY_WIDTH])
    fn_ref[...] = p[:, c0 + HY_WIDTH:c0 + HY_WIDTH + FN_WIDTH]
    fng_ref[...] = _silu(p[:, c0 + HY_WIDTH + FN_WIDTH:])


def _odd_in(x, g, scale, shift, w_bf):
    n, d = x.shape
    n_in = w_bf.shape[1]
    tm = _row_tile(n, 512)
    row = lambda w: pl.BlockSpec((tm, w), lambda i: (i, 0))
    vec = lambda w: pl.BlockSpec((1, w), lambda i: (0, 0))
    c0 = (HY_ORDER + 1) * HY_WIDTH
    return pl.pallas_call(
        _odd_in_kernel,
        out_shape=(jax.ShapeDtypeStruct((n, c0), F32), jax.ShapeDtypeStruct((n, HY_WIDTH), F32),
                   jax.ShapeDtypeStruct((n, FN_WIDTH), F32), jax.ShapeDtypeStruct((n, FN_WIDTH), F32)),
        grid=(n // tm,),
        in_specs=[row(d), vec(d), vec(d), vec(d), _const_spec((d, n_in))],
        out_specs=(row(c0), row(HY_WIDTH), row(FN_WIDTH), row(FN_WIDTH)),
        compiler_params=_params(("parallel",), VMEM_LIMIT),
        name="odd_in",
    )(x, g, scale, shift, w_bf)


def _conv3_kernel(u_ref, up_ref, un_ref, w_ref, b_ref, o_ref, *, tm):
    i = pl.program_id(0)
    n_tiles = pl.num_programs(0)
    u = u_ref[...]
    prev = jnp.where(i > 0, up_ref[7:8, :], 0.0)
    nxt = jnp.where(i < n_tiles - 1, un_ref[0:1, :], 0.0)
    e = jnp.concatenate([prev, u, nxt], axis=0)
    o_ref[...] = e[0:tm] * w_ref[0:1, :] + u * w_ref[1:2, :] + e[2:tm + 2] * w_ref[2:3, :] + b_ref[...]


def _conv3(u, w, b):
    n, c = u.shape
    tm = _row_tile(n, 512)
    hb = tm // 8
    last = n // 8 - 1
    kern = functools.partial(_conv3_kernel, tm=tm)
    return pl.pallas_call(
        kern,
        out_shape=jax.ShapeDtypeStruct((n, c), F32),
        grid=(n // tm,),
        in_specs=[pl.BlockSpec((tm, c), lambda i: (i, 0)),
                  pl.BlockSpec((8, c), lambda i: (jnp.maximum(i * hb - 1, 0), 0)),
                  pl.BlockSpec((8, c), lambda i: (jnp.minimum((i + 1) * hb, last), 0)),
                  pl.BlockSpec((3, c), lambda i: (0, 0)), pl.BlockSpec((1, c), lambda i: (0, 0))],
        out_specs=pl.BlockSpec((tm, c), lambda i: (i, 0)),
        compiler_params=_params(("parallel",), VMEM_LIMIT),
        name="conv3",
    )(u, u, u, w, b)


def _filter_kernel(emb_ref, w1_ref, b1_ref, w2_ref, b2_ref, w3_ref, fr_ref, dl_ref, h_ref, mass_ref,
                   *, tm, n_rows):
    i = pl.program_id(0)
    fr = fr_ref[...]
    a = jnp.sin(fr * (jnp.dot(emb_ref[...], w1_ref[...], precision=HIGHEST, preferred_element_type=F32)
                      + b1_ref[...]))
    a = jnp.sin(fr * (jnp.dot(a, w2_ref[...], precision=HIGHEST, preferred_element_type=F32) + b2_ref[...]))
    h = jnp.dot(a, w3_ref[...], precision=HIGHEST, preferred_element_type=F32)
    t = (i * tm + lax.broadcasted_iota(jnp.int32, (tm, 1), 0)).astype(F32) * (1.0 / (n_rows - 1))
    decay = jnp.exp(-t * jnp.abs(dl_ref[...]))

    @pl.when(i == 0)
    def _():
        mass_ref[...] = jnp.zeros_like(mass_ref)

    for o in range(HY_ORDER):
        tot = jnp.zeros((1, HY_WIDTH), F32)
        for d in range(2):
            c = (2 * o + d) * HY_WIDTH
            hd = h[:, c:c + HY_WIDTH] * decay
            h_ref[:, c:c + HY_WIDTH] = hd
            tot = tot + jnp.sum(jnp.abs(hd), axis=0, keepdims=True)
        mass_ref[:, o * HY_WIDTH:(o + 1) * HY_WIDTH] += tot


def _hyena_filters(emb, w1p, b1, w2, b2, w3, freq, deltas):
    n = emb.shape[0]
    tm = _row_tile(n, 512)
    nh = 2 * HY_ORDER * HY_WIDTH
    kern = functools.partial(_filter_kernel, tm=tm, n_rows=n)
    full = lambda a: pl.BlockSpec(a.shape, lambda i: (0,) * a.ndim)
    return pl.pallas_call(
        kern,
        out_shape=(jax.ShapeDtypeStruct((n, nh), F32), jax.ShapeDtypeStruct((1, HY_ORDER * HY_WIDTH), F32)),
        grid=(n // tm,),
        in_specs=[pl.BlockSpec((tm, emb.shape[1]), lambda i: (i, 0)), full(w1p), full(b1), full(w2), full(b2),
                  full(w3), full(freq), full(deltas)],
        out_specs=(pl.BlockSpec((tm, nh), lambda i: (i, 0)),
                   pl.BlockSpec((1, HY_ORDER * HY_WIDTH), lambda i: (0, 0))),
        compiler_params=_params(("arbitrary",), VMEM_LIMIT),
        name="hyena_filters",
    )(emb, w1p, b1, w2, b2, w3, freq, deltas)


def _dft_tables(n_total, n2, half_shift):
    n1_full = n_total // n2
    sh = 0.5 if half_shift else 0.0
    k1 = np.arange(DFT_N1, dtype=np.float64)[None, :, None] + sh
    n1 = np.arange(DFT_N1, dtype=np.float64)[None, None, :]
    nn2 = np.arange(n2, dtype=np.float64)[:, None, None]
    ph = -2.0 * np.pi * (n1 * k1 / n1_full + nn2 * k1 / n_total)
    fwd = np.stack([np.cos(ph), np.sin(ph)], axis=2).reshape(n2, 2 * DFT_N1, DFT_N1)
    fwd_pair = fwd.reshape(n2 // 2, 2, 2 * DFT_N1, DFT_N1).transpose(0, 2, 1, 3).reshape(
        n2 // 2, 2 * DFT_N1, 2 * DFT_N1)
    inv = np.stack([np.cos(ph), np.sin(ph)], axis=2).reshape(n2, 2 * DFT_N1, DFT_N1)
    inv = (2.0 / n_total) * inv.transpose(0, 2, 1)
    k2 = np.arange(n2, dtype=np.float64)
    ph2 = -2.0 * np.pi * np.outer(k2, k2) / n2
    fr, fi = np.cos(ph2), np.sin(ph2)
    big_fwd = np.block([[fr, -fi], [fi, fr]])
    big_inv = np.block([[fr, fi], [-fi, fr]])
    as32 = lambda a: jnp.asarray(a.astype(np.float32))
    return as32(fwd_pair), as32(inv), as32(big_fwd), as32(big_inv)


def _dft_fwd_kernel(x_ref, m_ref, big_ref, o_ref, a_sc, *, n2, group):
    j = pl.program_id(1)

    @pl.when(j == 0)
    def _():
        def body(p, carry):
            m = m_ref[p]
            for half in range(2):
                nn = 2 * p + half
                xs = x_ref[pl.ds(nn, DFT_N1, stride=n2), :]
                a = jnp.dot(m[:, half * DFT_N1:(half + 1) * DFT_N1], xs, precision=HIGHEST,
                            preferred_element_type=F32)
                a_sc[pl.ds(nn, 2 * DFT_N1, stride=n2), :] = a
            return carry

        lax.fori_loop(0, n2 // 2, body, 0)

    for g in range(group):
        r0 = pl.multiple_of((j * group + g) * 2 * n2, 2 * n2)
        blk = a_sc[pl.ds(r0, 2 * n2), :]
        o_ref[g * 2 * n2:(g + 1) * 2 * n2, :] = jnp.dot(big_ref[...], blk, precision=HIGHEST,
                                                         preferred_element_type=F32)


def _dft_fwd(x, col0, width, m_pair, big):
    n = x.shape[0]
    n2 = n // DFT_N1
    group = 8
    wt = LANES
    c0 = col0 // wt
    rows = 2 * DFT_N1 * n2
    kern = functools.partial(_dft_fwd_kernel, n2=n2, group=group)
    return pl.pallas_call(
        kern,
        out_shape=jax.ShapeDtypeStruct((rows, width), F32),
        grid=(width // wt, DFT_N1 // group),
        in_specs=[pl.BlockSpec((n, wt), lambda i, j: (0, c0 + i)),
                  _const_spec(m_pair.shape), _const_spec(big.shape)],
        out_specs=pl.BlockSpec((group * 2 * n2, wt), lambda i, j: (j, i)),
        scratch_shapes=[pltpu.VMEM((rows, wt), F32)],
        compiler_params=_params(("parallel", "arbitrary"), VMEM_LIMIT),
        name="dft_fwd",
    )(x, m_pair, big)


def _conv_inv_kernel(z_ref, hf_ref, hb_ref, hb0_ref, mass_ref, minv_ref, big_ref, y_ref, b_sc, *, n2, group):
    j = pl.program_id(1)
    inv_mass = 1.0 / (mass_ref[...] + EPS)
    hb0 = hb0_ref[...]
    for g in range(group):
        re = slice(g * 2 * n2, g * 2 * n2 + n2)
        im = slice(g * 2 * n2 + n2, (g + 1) * 2 * n2)
        kr = (hf_ref[re, :] + hb_ref[re, :] - hb0) * inv_mass
        ki = (hf_ref[im, :] - hb_ref[im, :]) * inv_mass
        zr = z_ref[re, :]
        zi = z_ref[im, :]
        prod = jnp.concatenate([zr * kr - zi * ki, zr * ki + zi * kr], axis=0)
        r0 = pl.multiple_of((j * group + g) * 2 * n2, 2 * n2)
        b_sc[pl.ds(r0, 2 * n2), :] = jnp.dot(big_ref[...], prod, precision=HIGHEST, preferred_element_type=F32)

    @pl.when(j == pl.num_programs(1) - 1)
    def _():
        def body(nn, carry):
            bs = b_sc[pl.ds(nn, 2 * DFT_N1, stride=n2), :]
            y = jnp.dot(minv_ref[nn], bs, precision=HIGHEST, preferred_element_type=F32)
            y_ref[pl.ds(nn, DFT_N1, stride=n2), :] = y
            return carry

        lax.fori_loop(0, n2, body, 0)


def _conv_inv(zf, hspec, order, hb0, mass, m_inv, big_inv):
    rows, width = zf.shape
    n2 = rows // (2 * DFT_N1)
    n = DFT_N1 * n2
    group = 8
    wt = LANES
    cf = (2 * order) * HY_WIDTH // wt
    cb = (2 * order + 1) * HY_WIDTH // wt
    cm = order * HY_WIDTH // wt
    kern = functools.partial(_conv_inv_kernel, n2=n2, group=group)
    blk = lambda c: pl.BlockSpec((group * 2 * n2, wt), lambda i, j: (j, c + i))
    return pl.pallas_call(
        kern,
        out_shape=jax.ShapeDtypeStruct((n, width), F32),
        grid=(width // wt, DFT_N1 // group),
        in_specs=[blk(0), blk(cf), blk(cb),
                  pl.BlockSpec((1, wt), lambda i, j: (0, cb + i)),
                  pl.BlockSpec((1, wt), lambda i, j: (0, cm + i)),
                  _const_spec(m_inv.shape), _const_spec(big_inv.shape)],
        out_specs=pl.BlockSpec((n, wt), lambda i, j: (0, i)),
        scratch_shapes=[pltpu.VMEM((rows, wt), F32)],
        compiler_params=_params(("parallel", "arbitrary"), VMEM_LIMIT),
        name="conv_inv",
    )(zf, hspec, hspec, hb0, mass, m_inv, big_inv)


def _hy_gate_kernel(c_ref, g_ref, z_ref, s_ref, o_ref):
    z = z_ref[...]
    o_ref[...] = g_ref[...] * (c_ref[...] + s_ref[...] * z)


def _hy_gate(conv, uc, gate_col, z_col, skip):
    n, w = conv.shape
    tm = _row_tile(n, 512)
    gc = gate_col // w
    zc = z_col // w
    return pl.pallas_call(
        _hy_gate_kernel,
        out_shape=jax.ShapeDtypeStruct((n, w), F32),
        grid=(n // tm,),
        in_specs=[pl.BlockSpec((tm, w), lambda i: (i, 0)), pl.BlockSpec((tm, w), lambda i: (i, gc)),
                  pl.BlockSpec((tm, w), lambda i: (i, zc)), pl.BlockSpec((1, w), lambda i: (0, 0))],
        out_specs=pl.BlockSpec((tm, w), lambda i: (i, 0)),
        compiler_params=_params(("parallel",), VMEM_LIMIT),
        name="hy_gate",
    )(conv, uc, uc, skip)


def _chan_dft_kernel(u_ref, cs_ref, o_ref):
    o_ref[...] = jnp.dot(u_ref[...], cs_ref[...], precision=HIGHEST, preferred_element_type=F32)


def _chan_dft(u, cs):
    n, c = u.shape
    tm = _row_tile(n, 512)
    return pl.pallas_call(
        _chan_dft_kernel,
        out_shape=jax.ShapeDtypeStruct((n, 2 * c), F32),
        grid=(n // tm,),
        in_specs=[pl.BlockSpec((tm, c), lambda i: (i, 0)), _const_spec(cs.shape)],
        out_specs=pl.BlockSpec((tm, 2 * c), lambda i: (i, 0)),
        compiler_params=_params(("parallel",), VMEM_LIMIT),
        name="chan_dft",
    )(u, cs)


def _fn_out_kernel(p_ref, q_ref, w_ref, o_ref, *, kb, norm):
    w = w_ref[...]
    for jj in range(kb):
        r = (p_ref[:, 0, jj, :] + q_ref[:, 0, jj, :]) * norm
        o_ref[jj] = jnp.dot(r.astype(BF16), w, preferred_element_type=F32)


def _fn_out(spec, fn_w_bf, n):
    n2 = n // DFT_N1
    c = FN_WIDTH
    kb = 8
    s4 = spec.reshape(DFT_N1, 2, n2, 2 * c)
    kern = functools.partial(_fn_out_kernel, kb=kb, norm=1.0 / math.sqrt(n * c))
    out = pl.pallas_call(
        kern,
        out_shape=jax.ShapeDtypeStruct((n2, DFT_N1, c), F32),
        grid=(n2 // kb,),
        in_specs=[pl.BlockSpec((DFT_N1, 1, kb, c), lambda i: (0, 0, i, 0)),
                  pl.BlockSpec((DFT_N1, 1, kb, c), lambda i: (0, 1, i, 1)),
                  _const_spec((c, c))],
        out_specs=pl.BlockSpec((kb, DFT_N1, c), lambda i: (i, 0, 0)),
        compiler_params=_params(("parallel",), VMEM_LIMIT),
        name="fn_out",
    )(s4, s4, fn_w_bf)
    return out.reshape(n, c)


def _odd_out_kernel(c2_ref, x2_ref, z1_ref, s_ref, hyg_ref, yd_ref, fng_ref, x_ref, gate_ref, wo_ref, fg_ref,
                    o_ref):
    z1 = z1_ref[...]
    yc = x2_ref[...] * (c2_ref[...] + s_ref[...] * z1)
    y = jnp.dot((yc * hyg_ref[...]).astype(BF16), wo_ref[0:HY_WIDTH, :], preferred_element_type=F32)
    y = y + jnp.dot((yd_ref[...] * fng_ref[...]).astype(BF16), wo_ref[HY_WIDTH:, :],
                    preferred_element_type=F32)
    xo = x_ref[...] + gate_ref[...] * y
    o_ref[...] = xo * lax.rsqrt(jnp.mean(xo * xo, axis=-1, keepdims=True) + EPS) * fg_ref[...]


def _odd_out(c2, uc, z1, skip1, hy_gate, yd, fn_gate, x, gate, w_out_bf, final_g):
    n, d = x.shape
    tm = _row_tile(n, 512)
    row = lambda w: pl.BlockSpec((tm, w), lambda i: (i, 0))
    vec = lambda w: pl.BlockSpec((1, w), lambda i: (0, 0))
    return pl.pallas_call(
        _odd_out_kernel,
        out_shape=jax.ShapeDtypeStruct((n, d), F32),
        grid=(n // tm,),
        in_specs=[row(HY_WIDTH), pl.BlockSpec((tm, HY_WIDTH), lambda i: (i, 2)), row(HY_WIDTH), vec(HY_WIDTH),
                  row(HY_WIDTH), row(FN_WIDTH), row(FN_WIDTH), row(d), vec(d), _const_spec((d, d)), vec(d)],
        out_specs=row(d),
        compiler_params=_params(("parallel",), VMEM_LIMIT),
        name="odd_out",
    )(c2, uc, z1, skip1, hy_gate, yd, fn_gate, x, gate, w_out_bf, final_g)


def _rope_tables(n):
    t = jnp.arange(n, dtype=jnp.int32)
    row = (t // GRID_W).astype(F32)
    col = (t % GRID_W).astype(F32)
    inv_freq = ROPE_THETA ** (-jnp.arange(ROPE_FREQS, dtype=F32) / ROPE_FREQS)
    ang_r = row[:, None] * inv_freq
    ang_c = col[:, None] * inv_freq
    ang = jnp.concatenate([ang_r, ang_r, ang_c, ang_c], axis=-1)
    sign = jnp.tile(jnp.concatenate([-jnp.ones((ROPE_FREQS,), F32), jnp.ones((ROPE_FREQS,), F32)]), 2)
    return jnp.cos(ang), jnp.sin(ang) * sign


def _hyena_embedding(n):
    t = jnp.linspace(0.0, 1.0, n, dtype=F32)[:, None]
    w = 2.0 * math.pi * jnp.arange(n, dtype=F32)[:, None] / n
    f = jnp.linspace(1e-4, HY_BANDS - 1, HY_BANDS, dtype=F32)[None, :]
    emb = jnp.concatenate([t, jnp.cos(f * w), -jnp.sin(f * w)], axis=-1)
    return jnp.pad(emb, ((0, 0), (0, LANES - HY_EMB)))


def kernel(x, c, ctx, c_ctx, w_mod, b_mod, norm_g, ev_w_in, ev_w_out, pool_w, pool_scale, q_norm_g, k_norm_g,
           od_w_in, od_w_out, hy_conv_w, hy_conv_b, hy_w1, hy_b1, hy_w2, hy_b2, hy_w3, hy_freq, hy_skip, fn_w,
           final_g):
    n, d = x.shape[1], x.shape[2]
    nc = ctx.shape[1]
    x0 = x[0]
    ctx0 = ctx[0]
    vec = lambda a: a.reshape(1, -1)

    cond = jnp.zeros((8, d), F32).at[0].set(c[0]).at[1].set(c_ctx)
    mod = _modulation(cond, w_mod, b_mod)
    shift0, scale0, gate0 = (mod[0, 0:1, k * d:(k + 1) * d] for k in range(3))
    cshift0, cscale0 = mod[0, 1:2, 0:d], mod[0, 1:2, d:2 * d]
    shift1, scale1, gate1 = (mod[1, 0:1, k * d:(k + 1) * d] for k in range(3))

    w_in0 = ev_w_in[0].astype(BF16)
    cos, sin = _rope_tables(n)
    g0 = vec(norm_g[0])
    qg, kg = vec(q_norm_g[0]), vec(k_norm_g[0])
    a_val, a_gate, q, k, v, b_gate = _even_in(x0, g0, scale0, shift0, w_in0, qg, kg, cos, sin)
    ones = jnp.ones((nc, HEAD_DIM), F32)
    _, _, _, ck, cv, _ = _even_in(ctx0, g0, cscale0, cshift0, w_in0, qg, kg, ones, jnp.zeros_like(ones))
    yb = _attention(q, ck, cv, k, v, b_gate)
    pool_bd = jax.scipy.linalg.block_diag(*[pool_w[0, gi] for gi in range(pool_w.shape[1])]).astype(BF16)
    x1 = _even_out(a_val, a_gate, yb, x0, gate0, pool_bd, vec(pool_scale[0]), ev_w_out[0].astype(BF16))

    hy_in, hy_gate, fn_in, fn_gate = _odd_in(x1, vec(norm_g[1]), scale1, shift1, od_w_in[0].astype(BF16))
    uc = _conv3(hy_in, hy_conv_w[0], vec(hy_conv_b[0]))

    max_decay = math.log(HY_DECAY_TARGET) / HY_FAST_DECAY
    min_decay = math.log(HY_DECAY_TARGET) / HY_SLOW_DECAY
    deltas = jnp.linspace(min_decay, max_decay, HY_WIDTH, dtype=F32)[None, :]
    w1p = jnp.pad(hy_w1[0], ((0, LANES - HY_EMB), (0, 0)))
    h_raw, mass = _hyena_filters(_hyena_embedding(n), w1p, vec(hy_b1[0]), hy_w2[0], vec(hy_b2[0]), hy_w3[0],
                                 vec(hy_freq[0]), deltas)

    n2 = n // DFT_N1
    m_pair, m_inv, big_fwd, big_inv = _dft_tables(2 * n, n2, True)
    h_spec = _dft_fwd(h_raw, 0, h_raw.shape[1], m_pair, big_fwd)
    hb0 = h_raw[0:1, :]
    v_spec = _dft_fwd(uc, 0, HY_WIDTH, m_pair, big_fwd)
    conv1 = _conv_inv(v_spec, h_spec, 0, hb0, mass, m_inv, big_inv)
    z1 = _hy_gate(conv1, uc, HY_WIDTH, 0, hy_skip[0, 0:1])
    z1_spec = _dft_fwd(z1, 0, HY_WIDTH, m_pair, big_fwd)
    conv2 = _conv_inv(z1_spec, h_spec, 1, hb0, mass, m_inv, big_inv)

    ch = np.arange(FN_WIDTH, dtype=np.float64)
    ph = 2.0 * np.pi * np.outer(ch, ch) / FN_WIDTH
    cs = jnp.asarray(np.concatenate([np.cos(ph), np.sin(ph)], axis=1).astype(np.float32))
    pq = _chan_dft(fn_in, cs)
    f_pair, _, f_big, _ = _dft_tables(n, n2, False)
    pq_spec = _dft_fwd(pq, 0, 2 * FN_WIDTH, f_pair, f_big)
    yd = _fn_out(pq_spec, fn_w[0].astype(BF16), n)

    out = _odd_out(conv2, uc, z1, hy_skip[0, 1:2], hy_gate, yd, fn_gate, x1, gate1, od_w_out[0].astype(BF16),
                   vec(final_g))
    return out[None]
```

```python
import functools
import math

import numpy as np
import jax
import jax.numpy as jnp
from jax import lax
from jax.experimental import pallas as pl
from jax.experimental.pallas import tpu as pltpu

F32 = jnp.float32
BF16 = jnp.bfloat16
HIGHEST = lax.Precision.HIGHEST

EPS = 1e-6
GRID_W = 64
HEAD_DIM = 128
ROPE_FREQS = 32
ROPE_THETA = 10000.0
N_Q_HEADS = 6
N_KV_HEADS = 2
Q_PER_KV = N_Q_HEADS // N_KV_HEADS
POOL_WIDTH = 256
POOL_GROUP_DIM = 64
POOL_WINDOWS = (2, 4, 8, 16)
POOL_HALO = 8
ATT_WIDTH = N_Q_HEADS * HEAD_DIM
KV_WIDTH = N_KV_HEADS * HEAD_DIM
HY_WIDTH = 768
HY_ORDER = 2
HY_EMB = 33
HY_BANDS = 16
HY_HIDDEN = 64
FN_WIDTH = 256
HY_DECAY_TARGET = 1e-2
HY_FAST_DECAY = 0.3
HY_SLOW_DECAY = 1.5

LANES = 128
DFT_N1 = 64
VMEM_LIMIT = 56 * 1024 * 1024


def _row_tile(n, pref):
    t = min(pref, n)
    assert n % t == 0
    return t


def _silu(x):
    return x * jax.nn.sigmoid(x)


def _params(sem, vmem=None):
    return pltpu.CompilerParams(dimension_semantics=sem, vmem_limit_bytes=vmem)


def _const_spec(shape):
    nd = len(shape)
    return pl.BlockSpec(shape, lambda *_: (0,) * nd, pipeline_mode=pl.Buffered(1))


def _mod_kernel(cond_ref, w_ref, b_ref, o_ref):
    s = _silu(cond_ref[...])
    o_ref[0] = jnp.dot(s, w_ref[0], precision=HIGHEST, preferred_element_type=F32) + b_ref[0]


def _modulation(cond, w_mod, b_mod):
    depth, d, d3 = w_mod.shape
    tn = 1024
    return pl.pallas_call(
        _mod_kernel,
        out_shape=jax.ShapeDtypeStruct((depth, 8, d3), F32),
        grid=(depth, d3 // tn),
        in_specs=[pl.BlockSpec((8, d), lambda i, j: (0, 0)),
                  pl.BlockSpec((1, d, tn), lambda i, j: (i, 0, j)),
                  pl.BlockSpec((1, 1, tn), lambda i, j: (i, 0, j))],
        out_specs=pl.BlockSpec((1, 8, tn), lambda i, j: (i, 0, j)),
        compiler_params=_params(("arbitrary", "arbitrary")),
        name="modulation",
    )(cond, w_mod, b_mod.reshape(depth, 1, d3))


def _norm_mod(x, g, scale, shift):
    y = x * lax.rsqrt(jnp.mean(x * x, axis=-1, keepdims=True) + EPS)
    return (y * g) * (1.0 + scale) + shift


def _even_in_kernel(x_ref, g_ref, sc_ref, sh_ref, w_ref, qg_ref, kg_ref, cos_ref, sin_ref,
                    aval_ref, agate_ref, q_ref, k_ref, v_ref, bgate_ref):
    h = _norm_mod(x_ref[...], g_ref[...], sc_ref[...], sh_ref[...])
    p = jnp.dot(h.astype(BF16), w_ref[...], preferred_element_type=F32)
    aval_ref[...] = p[:, 0:POOL_WIDTH]
    agate_ref[...] = _silu(p[:, POOL_WIDTH:2 * POOL_WIDTH])
    cos = cos_ref[...]
    sin = sin_ref[...]
    lane = lax.broadcasted_iota(jnp.int32, cos.shape, 1)
    low_half = (lane % (2 * ROPE_FREQS)) < ROPE_FREQS

    def head(xh, g):
        y = xh * lax.rsqrt(jnp.mean(xh * xh, axis=-1, keepdims=True) + EPS) * g
        rot = jnp.where(low_half, pltpu.roll(y, HEAD_DIM - ROPE_FREQS, 1), pltpu.roll(y, ROPE_FREQS, 1))
        return y * cos + rot * sin

    q0 = 2 * POOL_WIDTH
    for hq in range(N_Q_HEADS):
        sl = slice(hq * HEAD_DIM, (hq + 1) * HEAD_DIM)
        q_ref[:, sl] = head(p[:, q0 + hq * HEAD_DIM:q0 + (hq + 1) * HEAD_DIM], qg_ref[...]).astype(BF16)
    k0 = q0 + ATT_WIDTH
    for hk in range(N_KV_HEADS):
        sl = slice(hk * HEAD_DIM, (hk + 1) * HEAD_DIM)
        k_ref[:, sl] = head(p[:, k0 + hk * HEAD_DIM:k0 + (hk + 1) * HEAD_DIM], kg_ref[...]).astype(BF16)
    v0 = k0 + KV_WIDTH
    v_ref[...] = p[:, v0:v0 + KV_WIDTH].astype(BF16)
    b0 = v0 + KV_WIDTH
    bgate_ref[...] = _silu(p[:, b0:b0 + ATT_WIDTH])


def _even_in(x, g, scale, shift, w_bf, qg, kg, cos, sin):
    n, d = x.shape
    n_in = w_bf.shape[1]
    tm = _row_tile(n, 512)
    row = lambda w: pl.BlockSpec((tm, w), lambda i: (i, 0))
    vec = lambda w: pl.BlockSpec((1, w), lambda i: (0, 0))
    return pl.pallas_call(
        _even_in_kernel,
        out_shape=(jax.ShapeDtypeStruct((n, POOL_WIDTH), F32), jax.ShapeDtypeStruct((n, POOL_WIDTH), F32),
                   jax.ShapeDtypeStruct((n, ATT_WIDTH), BF16), jax.ShapeDtypeStruct((n, KV_WIDTH), BF16),
                   jax.ShapeDtypeStruct((n, KV_WIDTH), BF16), jax.ShapeDtypeStruct((n, ATT_WIDTH), F32)),
        grid=(n // tm,),
        in_specs=[row(d), vec(d), vec(d), vec(d), _const_spec((d, n_in)), vec(HEAD_DIM), vec(HEAD_DIM),
                  row(HEAD_DIM), row(HEAD_DIM)],
        out_specs=(row(POOL_WIDTH), row(POOL_WIDTH), row(ATT_WIDTH), row(KV_WIDTH), row(KV_WIDTH),
                   row(ATT_WIDTH)),
        compiler_params=_params(("parallel",), VMEM_LIMIT),
        name="even_in",
    )(x, g, scale, shift, w_bf, qg, kg, cos, sin)


ATT_ROW_CHUNK = 32


def _attn_kernel(q_ref, ck_ref, cv_ref, k_ref, v_ref, bg_ref, o_ref, s_sc, p_sc, m_sc, al_sc, acc_sc,
                 *, tq, tk, n_kv_tiles):
    c = (HEAD_DIM ** -0.5) * math.log2(math.e)
    rows = Q_PER_KV * tq
    nc = ck_ref.shape[0]
    for h in range(N_KV_HEADS):
        hs = slice(h * HEAD_DIM, (h + 1) * HEAD_DIM)
        qs = jnp.concatenate(
            [q_ref[:, (Q_PER_KV * h + g) * HEAD_DIM:(Q_PER_KV * h + g + 1) * HEAD_DIM] for g in range(Q_PER_KV)],
            axis=0)

        def scores(buf, kt, width):
            s_sc[buf, :, 0:width] = lax.dot_general(qs, kt, (((1,), (1,)), ((), ())),
                                                    preferred_element_type=F32)

        def softmax_pv(buf, vt, width):
            for r in range(rows // ATT_ROW_CHUNK):
                rs = slice(r * ATT_ROW_CHUNK, (r + 1) * ATT_ROW_CHUNK)
                cols = [s_sc[buf, rs, j * LANES:(j + 1) * LANES] for j in range(width // LANES)]
                mx = functools.reduce(jnp.maximum, cols)
                m_old = m_sc[rs, :]
                m_new = jnp.maximum(m_old, jnp.max(mx, axis=-1, keepdims=True) * c)
                for j, sj in enumerate(cols):
                    p_sc[buf, rs, j * LANES:(j + 1) * LANES] = jnp.exp2(sj * c - m_new).astype(BF16)
                al_sc[rs, :] = jnp.exp2(m_old - m_new)
                m_sc[rs, :] = m_new
            v_ext = jnp.concatenate([vt, jnp.ones((width, HEAD_DIM), BF16)], axis=1)
            pv = jnp.dot(p_sc[buf, :, 0:width], v_ext, preferred_element_type=F32)
            al = al_sc[...]
            for j in range(2):
                js = slice(j * HEAD_DIM, (j + 1) * HEAD_DIM)
                acc_sc[:, js] = al * acc_sc[:, js] + pv[:, js]

        m_sc[...] = jnp.full(m_sc.shape, -1e30, F32)
        acc_sc[...] = jnp.zeros(acc_sc.shape, F32)
        scores(0, ck_ref[:, hs], nc)
        softmax_pv(0, cv_ref[:, hs], nc)
        scores(0, k_ref[0:tk, hs], tk)

        def body(i, carry):
            r0 = pl.multiple_of(2 * i * tk, tk)
            r1 = pl.multiple_of(r0 + tk, tk)
            r2 = pl.multiple_of(jnp.minimum(r0 + 2 * tk, (n_kv_tiles - 1) * tk), tk)
            scores(1, k_ref[pl.ds(r1, tk), hs], tk)
            softmax_pv(0, v_ref[pl.ds(r0, tk), hs], tk)
            scores(0, k_ref[pl.ds(r2, tk), hs], tk)
            softmax_pv(1, v_ref[pl.ds(r1, tk), hs], tk)
            return carry

        lax.fori_loop(0, n_kv_tiles // 2, body, 0)
        out = acc_sc[:, 0:HEAD_DIM] / acc_sc[:, HEAD_DIM:2 * HEAD_DIM]
        for g in range(Q_PER_KV):
            cs = slice((Q_PER_KV * h + g) * HEAD_DIM, (Q_PER_KV * h + g + 1) * HEAD_DIM)
            o_ref[:, cs] = (out[g * tq:(g + 1) * tq] * bg_ref[:, cs]).astype(o_ref.dtype)


def _attention(q, ck, cv, k, v, bgate):
    n = q.shape[0]
    nc = ck.shape[0]
    tq = _row_tile(n, 256)
    tk = _row_tile(n, 1024) if n >= 2048 else n // 2
    assert (n // tk) % 2 == 0 and nc <= tk and tk % LANES == 0 and nc % LANES == 0
    rows = Q_PER_KV * tq
    kern = functools.partial(_attn_kernel, tq=tq, tk=tk, n_kv_tiles=n // tk)
    return pl.pallas_call(
        kern,
        out_shape=jax.ShapeDtypeStruct((n, ATT_WIDTH), F32),
        grid=(n // tq,),
        in_specs=[pl.BlockSpec((tq, ATT_WIDTH), lambda i: (i, 0)),
                  _const_spec((nc, KV_WIDTH)), _const_spec((nc, KV_WIDTH)),
                  _const_spec((n, KV_WIDTH)), _const_spec((n, KV_WIDTH)),
                  pl.BlockSpec((tq, ATT_WIDTH), lambda i: (i, 0))],
        out_specs=pl.BlockSpec((tq, ATT_WIDTH), lambda i: (i, 0)),
        scratch_shapes=[pltpu.VMEM((2, rows, tk), F32), pltpu.VMEM((2, rows, tk), BF16),
                        pltpu.VMEM((rows, LANES), F32), pltpu.VMEM((rows, LANES), F32),
                        pltpu.VMEM((rows, 2 * HEAD_DIM), F32)],
        compiler_params=_params(("parallel",), VMEM_LIMIT),
        name="attention",
    )(q, ck, cv, k, v, bgate)


def _even_out_kernel(a_ref, ap_ref, an_ref, ag_ref, yb_ref, x_ref, gate_ref, pw_ref, ps_ref, wo_ref, o_ref,
                     *, tm, n_rows):
    i = pl.program_id(0)
    n_tiles = pl.num_programs(0)
    u = a_ref[...]
    prev = jnp.where(i > 0, ap_ref[...], 0.0)
    nxt = jnp.where(i < n_tiles - 1, an_ref[...], 0.0)
    e = jnp.concatenate([prev, u, nxt], axis=0)
    n = tm + 2 * POOL_HALO
    s2 = e[0:n - 1] + e[1:n]
    s4 = s2[0:n - 3] + s2[2:n - 1]
    s8 = s4[0:n - 7] + s4[4:n - 3]
    s16 = s8[0:n - 15] + s8[8:n - 7]
    lane = lax.broadcasted_iota(jnp.int32, (tm, POOL_WIDTH), 1)
    grp = lane // POOL_GROUP_DIM
    win = jnp.where(grp == 0, s2[7:7 + tm],
                    jnp.where(grp == 1, s4[6:6 + tm], jnp.where(grp == 2, s8[4:4 + tm], s16[0:tm])))
    half = jnp.where(grp == 0, 1, jnp.where(grp == 1, 2, jnp.where(grp == 2, 4, 8)))
    t = i * tm + lax.broadcasted_iota(jnp.int32, (tm, POOL_WIDTH), 0)
    cnt = (jnp.minimum(t + half, n_rows) - jnp.maximum(t - half, 0)).astype(F32)
    d = win / cnt - u
    ya = jnp.dot(d.astype(BF16), pw_ref[...], preferred_element_type=F32) * ps_ref[...]
    ya = ya * ag_ref[...]
    y = jnp.dot(ya.astype(BF16), wo_ref[0:POOL_WIDTH, :], preferred_element_type=F32)
    y = y + jnp.dot(yb_ref[...].astype(BF16), wo_ref[POOL_WIDTH:, :], preferred_element_type=F32)
    o_ref[...] = x_ref[...] + gate_ref[...] * y


def _even_out(a_val, a_gate, yb, x, gate, pool_bd, pool_scale, w_out_bf):
    n, d = x.shape
    tm = _row_tile(n, 512)
    hb = tm // POOL_HALO
    last = n // POOL_HALO - 1
    row = lambda w: pl.BlockSpec((tm, w), lambda i: (i, 0))
    vec = lambda w: pl.BlockSpec((1, w), lambda i: (0, 0))
    kern = functools.partial(_even_out_kernel, tm=tm, n_rows=n)
    return pl.pallas_call(
        kern,
        out_shape=jax.ShapeDtypeStruct((n, d), F32),
        grid=(n // tm,),
        in_specs=[row(POOL_WIDTH),
                  pl.BlockSpec((POOL_HALO, POOL_WIDTH), lambda i: (jnp.maximum(i * hb - 1, 0), 0)),
                  pl.BlockSpec((POOL_HALO, POOL_WIDTH), lambda i: (jnp.minimum((i + 1) * hb, last), 0)),
                  row(POOL_WIDTH), row(ATT_WIDTH), row(d), vec(d),
                  _const_spec((POOL_WIDTH, POOL_WIDTH)), vec(POOL_WIDTH), _const_spec((d, d))],
        out_specs=row(d),
        compiler_params=_params(("parallel",), VMEM_LIMIT),
        name="even_out",
    )(a_val, a_val, a_val, a_gate, yb, x, gate, pool_bd, pool_scale, w_out_bf)


def _odd_in_kernel(x_ref, g_ref, sc_ref, sh_ref, w_ref, hy_ref, hyg_ref, fn_ref, fng_ref):
    h = _norm_mod(x_ref[...], g_ref[...], sc_ref[...], sh_ref[...])
    p = jnp.dot(h.astype(BF16), w_ref[...], preferred_element_type=F32)
    c0 = (HY_ORDER + 1) * HY_WIDTH
    hy_ref[...] = p[:, 0:c0]
    hyg_ref[...] = _silu(p[:, c0:c0 + HY_WIDTH])
    fn_ref[...] = p[:, c0 + HY_WIDTH:c0 + HY_WIDTH + FN_WIDTH]
    fng_ref[...] = _silu(p[:, c0 + HY_WIDTH + FN_WIDTH:])


def _odd_in(x, g, scale, shift, w_bf):
    n, d = x.shape
    n_in = w_bf.shape[1]
    tm = _row_tile(n, 512)
    row = lambda w: pl.BlockSpec((tm, w), lambda i: (i, 0))
    vec = lambda w: pl.BlockSpec((1, w), lambda i: (0, 0))
    c0 = (HY_ORDER + 1) * HY_WIDTH
    return pl.pallas_call(
        _odd_in_kernel,
        out_shape=(jax.ShapeDtypeStruct((n, c0), F32), jax.ShapeDtypeStruct((n, HY_WIDTH), F32),
                   jax.ShapeDtypeStruct((n, FN_WIDTH), F32), jax.ShapeDtypeStruct((n, FN_WIDTH), F32)),
        grid=(n // tm,),
        in_specs=[row(d), vec(d), vec(d), vec(d), _const_spec((d, n_in))],
        out_specs=(row(c0), row(HY_WIDTH), row(FN_WIDTH), row(FN_WIDTH)),
        compiler_params=_params(("parallel",), VMEM_LIMIT),
        name="odd_in",
    )(x, g, scale, shift, w_bf)


def _conv3_kernel(u_ref, up_ref, un_ref, w_ref, b_ref, o_ref, *, tm):
    i = pl.program_id(0)
    n_tiles = pl.num_programs(0)
    u = u_ref[...]
    prev = jnp.where(i > 0, up_ref[7:8, :], 0.0)
    nxt = jnp.where(i < n_tiles - 1, un_ref[0:1, :], 0.0)
    e = jnp.concatenate([prev, u, nxt], axis=0)
    o_ref[...] = e[0:tm] * w_ref[0:1, :] + u * w_ref[1:2, :] + e[2:tm + 2] * w_ref[2:3, :] + b_ref[...]


def _conv3(u, w, b):
    n, c = u.shape
    tm = _row_tile(n, 512)
    hb = tm // 8
    last = n // 8 - 1
    kern = functools.partial(_conv3_kernel, tm=tm)
    return pl.pallas_call(
        kern,
        out_shape=jax.ShapeDtypeStruct((n, c), F32),
        grid=(n // tm,),
        in_specs=[pl.BlockSpec((tm, c), lambda i: (i, 0)),
                  pl.BlockSpec((8, c), lambda i: (jnp.maximum(i * hb - 1, 0), 0)),
                  pl.BlockSpec((8, c), lambda i: (jnp.minimum((i + 1) * hb, last), 0)),
                  pl.BlockSpec((3, c), lambda i: (0, 0)), pl.BlockSpec((1, c), lambda i: (0, 0))],
        out_specs=pl.BlockSpec((tm, c), lambda i: (i, 0)),
        compiler_params=_params(("parallel",), VMEM_LIMIT),
        name="conv3",
    )(u, u, u, w, b)


def _filter_kernel(emb_ref, embr_ref, w1_ref, b1_ref, w2_ref, b2_ref, w3_ref, fr_ref, dl_ref,
                   lo_ref, hi_ref, mass_ref, *, tm, n_rows):
    i = pl.program_id(0)
    fr = fr_ref[...]

    def ffn(e):
        a = jnp.sin(fr * (jnp.dot(e, w1_ref[...], precision=HIGHEST, preferred_element_type=F32) + b1_ref[...]))
        return jnp.sin(fr * (jnp.dot(a, w2_ref[...], precision=HIGHEST, preferred_element_type=F32)
                             + b2_ref[...]))

    a_f = ffn(emb_ref[...]).astype(BF16)
    a_b = ffn(embr_ref[...]).astype(BF16)
    j = i * tm + lax.broadcasted_iota(jnp.int32, (tm, 1), 0)
    pos_b = jnp.where(j == 0, 0, n_rows - j)
    dl = jnp.abs(dl_ref[...])
    inv = 1.0 / (n_rows - 1)
    decay_f = jnp.exp(-(j.astype(F32) * inv) * dl)
    decay_b = jnp.exp(-(pos_b.astype(F32) * inv) * dl)

    @pl.when(i == 0)
    def _():
        mass_ref[...] = jnp.zeros_like(mass_ref)

    for o in range(HY_ORDER):
        cf = (2 * o) * HY_WIDTH
        cb = (2 * o + 1) * HY_WIDTH
        hf = jnp.dot(a_f, w3_ref[:, cf:cf + HY_WIDTH], preferred_element_type=F32) * decay_f
        hb = jnp.dot(a_b, w3_ref[:, cb:cb + HY_WIDTH], preferred_element_type=F32) * decay_b
        os_ = slice(o * HY_WIDTH, (o + 1) * HY_WIDTH)
        lo_ref[:, os_] = hf
        hi_ref[:, os_] = jnp.where(j == 0, 0.0, -hb)
        mass_ref[:, os_] += (jnp.sum(jnp.abs(hf), axis=0, keepdims=True)
                             + jnp.sum(jnp.abs(hb), axis=0, keepdims=True))


def _hyena_filters(emb, emb_rev, w1p, b1, w2, b2, w3_bf, freq, deltas):
    n = emb.shape[0]
    tm = _row_tile(n, 512)
    nh = HY_ORDER * HY_WIDTH
    kern = functools.partial(_filter_kernel, tm=tm, n_rows=n)
    full = lambda a: pl.BlockSpec(a.shape, lambda i: (0,) * a.ndim)
    row = lambda w: pl.BlockSpec((tm, w), lambda i: (i, 0))
    return pl.pallas_call(
        kern,
        out_shape=(jax.ShapeDtypeStruct((n, nh), F32), jax.ShapeDtypeStruct((n, nh), F32),
                   jax.ShapeDtypeStruct((1, nh), F32)),
        grid=(n // tm,),
        in_specs=[row(emb.shape[1]), row(emb.shape[1]), full(w1p), full(b1), full(w2), full(b2),
                  full(w3_bf), full(freq), full(deltas)],
        out_specs=(row(nh), row(nh), pl.BlockSpec((1, nh), lambda i: (0, 0))),
        compiler_params=_params(("arbitrary",), VMEM_LIMIT),
        name="hyena_filters",
    )(emb, emb_rev, w1p, b1, w2, b2, w3_bf, freq, deltas)


DFT_PARTS = 1


def _table_parts(a):
    a32 = jnp.asarray(a.astype(np.float32))
    hi = a32.astype(BF16)
    if DFT_PARTS == 1:
        return hi[None]
    lo = (a32 - hi.astype(F32)).astype(BF16)
    return jnp.stack([hi, lo])


def _split_bf16(x):
    hi = x.astype(BF16)
    if DFT_PARTS == 1:
        return (hi,)
    return (hi, (x - hi.astype(F32)).astype(BF16))


def _ref_parts(ref, *idx):
    return tuple(ref[(q,) + idx] for q in range(ref.shape[0]))


def _dot_parts(lhs, rhs):
    acc = jnp.dot(lhs[0], rhs[0], preferred_element_type=F32)
    if len(lhs) > 1:
        acc = acc + jnp.dot(lhs[1], rhs[0], preferred_element_type=F32)
    if len(rhs) > 1:
        acc = acc + jnp.dot(lhs[0], rhs[1], preferred_element_type=F32)
    return acc


def _dft_tables(n_total, n2, half_shift, n1_used):
    n1_full = n_total // n2
    sh = 0.5 if half_shift else 0.0
    k1 = np.arange(DFT_N1, dtype=np.float64)[None, :, None] + sh
    n1 = np.arange(n1_used, dtype=np.float64)[None, None, :]
    nn2 = np.arange(n2, dtype=np.float64)[:, None, None]
    ph = -2.0 * np.pi * (n1 * k1 / n1_full + nn2 * k1 / n_total)
    fwd = np.stack([np.cos(ph), np.sin(ph)], axis=2).reshape(n2, 2 * DFT_N1, n1_used)
    if n1_used == DFT_N1:
        fwd = fwd.reshape(n2 // 2, 2, 2 * DFT_N1, DFT_N1).transpose(0, 2, 1, 3).reshape(
            n2 // 2, 2 * DFT_N1, 2 * DFT_N1)
    ph_i = ph[:, :, :DFT_N1]
    inv = np.stack([np.cos(ph_i), np.sin(ph_i)], axis=2).reshape(n2, 2 * DFT_N1, DFT_N1)
    inv = (2.0 / n_total) * inv.transpose(0, 2, 1)
    k2 = np.arange(n2, dtype=np.float64)
    ph2 = -2.0 * np.pi * np.outer(k2, k2) / n2
    fr, fi = np.cos(ph2), np.sin(ph2)
    big_fwd = np.block([[fr, -fi], [fi, fr]])
    big_inv = np.block([[fr, fi], [-fi, fr]])
    return _table_parts(fwd), _table_parts(inv), _table_parts(big_fwd), _table_parts(big_inv)


def _scratch_pitch(n2):
    return n2 + 8


def _stage1_fwd(x_refs, m_ref, a_sc, *, n2, pitch):
    def one(nn, lhs):
        xs = [xr[pl.ds(nn, DFT_N1, stride=n2), :] for xr in x_refs]
        xs = xs[0] if len(xs) == 1 else jnp.concatenate(xs, axis=0)
        a_sc[pl.ds(nn, 2 * DFT_N1, stride=pitch), :] = _dot_parts(lhs, _split_bf16(xs))

    if len(x_refs) == 1:
        def body(p, carry):
            ms = _ref_parts(m_ref, p)
            for half in range(2):
                one(2 * p + half, tuple(m[:, half * DFT_N1:(half + 1) * DFT_N1] for m in ms))
            return carry

        lax.fori_loop(0, n2 // 2, body, 0)
    else:
        def body(nn, carry):
            one(nn, _ref_parts(m_ref, nn))
            return carry

        lax.fori_loop(0, n2, body, 0)


def _load_cplx(a_sc, k1, *, n2, pitch):
    r0 = pl.multiple_of(k1 * 2 * pitch, 8)
    return jnp.concatenate([a_sc[pl.ds(r0, n2), :], a_sc[pl.ds(r0 + pitch, n2), :]], axis=0)


def _dft_fwd_kernel(*refs, n_x, n2, pitch, group, scaled):
    x_refs = refs[:n_x]
    rest = refs[n_x:]
    if scaled:
        mass_ref, rest = rest[0], rest[1:]
    m_ref, big_ref, o_ref, a_sc = rest
    j = pl.program_id(1)

    @pl.when(j == 0)
    def _():
        _stage1_fwd(x_refs, m_ref, a_sc, n2=n2, pitch=pitch)

    big = _ref_parts(big_ref)
    for g in range(group):
        blk = _load_cplx(a_sc, j * group + g, n2=n2, pitch=pitch)
        out = _dot_parts(big, _split_bf16(blk))
        if scaled:
            out = out * (1.0 / (mass_ref[...] + EPS))
        o_ref[g * 2 * n2:(g + 1) * 2 * n2, :] = out


def _dft_fwd(xs, width, m_fwd, big, mass=None):
    n = xs[0].shape[0]
    n2 = n // DFT_N1
    pitch = _scratch_pitch(n2)
    group = 8
    wt = LANES
    rows = 2 * DFT_N1 * n2
    kern = functools.partial(_dft_fwd_kernel, n_x=len(xs), n2=n2, pitch=pitch, group=group,
                             scaled=mass is not None)
    x_spec = pl.BlockSpec((n, wt), lambda i, j: (0, i), pipeline_mode=pl.Buffered(1))
    ops = list(xs)
    specs = [x_spec] * len(xs)
    if mass is not None:
        ops.append(mass)
        specs.append(pl.BlockSpec((1, wt), lambda i, j: (0, i)))
    return pl.pallas_call(
        kern,
        out_shape=jax.ShapeDtypeStruct((rows, width), F32),
        grid=(width // wt, DFT_N1 // group),
        in_specs=specs + [_const_spec(m_fwd.shape), _const_spec(big.shape)],
        out_specs=pl.BlockSpec((group * 2 * n2, wt), lambda i, j: (j, i)),
        scratch_shapes=[pltpu.VMEM((2 * DFT_N1 * pitch, wt), F32)],
        compiler_params=_params(("parallel", "arbitrary"), VMEM_LIMIT),
        name="dft_fwd",
    )(*ops, m_fwd, big)


def _hyena_conv_kernel(x_ref, k_ref, m_ref, minv_ref, bigf_ref, bigi_ref, y_ref, a_sc, *, n2, pitch, group):
    j = pl.program_id(1)

    @pl.when(j == 0)
    def _():
        _stage1_fwd([x_ref], m_ref, a_sc, n2=n2, pitch=pitch)

    bigf = _ref_parts(bigf_ref)
    bigi = _ref_parts(bigi_ref)
    for g in range(group):
        k1 = j * group + g
        z = _dot_parts(bigf, _split_bf16(_load_cplx(a_sc, k1, n2=n2, pitch=pitch)))
        zr, zi = z[0:n2], z[n2:2 * n2]
        kr = k_ref[g * 2 * n2:g * 2 * n2 + n2, :]
        ki = k_ref[g * 2 * n2 + n2:(g + 1) * 2 * n2, :]
        prod = jnp.concatenate([zr * kr - zi * ki, zr * ki + zi * kr], axis=0)
        b = _dot_parts(bigi, _split_bf16(prod))
        r0 = pl.multiple_of(k1 * 2 * pitch, 8)
        a_sc[pl.ds(r0, n2), :] = b[0:n2]
        a_sc[pl.ds(r0 + pitch, n2), :] = b[n2:2 * n2]

    @pl.when(j == pl.num_programs(1) - 1)
    def _():
        def body(nn, carry):
            bs = a_sc[pl.ds(nn, 2 * DFT_N1, stride=pitch), :]
            y_ref[pl.ds(nn, DFT_N1, stride=n2), :] = _dot_parts(_ref_parts(minv_ref, nn), _split_bf16(bs))
            return carry

        lax.fori_loop(0, n2, body, 0)


def _hyena_conv(x, col0, kspec, order, m_fwd, m_inv, big_fwd, big_inv):
    n = x.shape[0]
    n2 = n // DFT_N1
    pitch = _scratch_pitch(n2)
    group = 8
    wt = LANES
    c0 = col0 // wt
    ck = order * HY_WIDTH // wt
    kern = functools.partial(_hyena_conv_kernel, n2=n2, pitch=pitch, group=group)
    return pl.pallas_call(
        kern,
        out_shape=jax.ShapeDtypeStruct((n, HY_WIDTH), F32),
        grid=(HY_WIDTH // wt, DFT_N1 // group),
        in_specs=[pl.BlockSpec((n, wt), lambda i, j: (0, c0 + i), pipeline_mode=pl.Buffered(1)),
                  pl.BlockSpec((group * 2 * n2, wt), lambda i, j: (j, ck + i)),
                  _const_spec(m_fwd.shape), _const_spec(m_inv.shape),
                  _const_spec(big_fwd.shape), _const_spec(big_inv.shape)],
        out_specs=pl.BlockSpec((n, wt), lambda i, j: (0, i)),
        scratch_shapes=[pltpu.VMEM((2 * DFT_N1 * pitch, wt), F32)],
        compiler_params=_params(("parallel", "arbitrary"), VMEM_LIMIT),
        name="hyena_conv",
    )(x, kspec, m_fwd, m_inv, big_fwd, big_inv)


def _hy_gate_kernel(c_ref, g_ref, z_ref, s_ref, o_ref):
    z = z_ref[...]
    o_ref[...] = g_ref[...] * (c_ref[...] + s_ref[...] * z)


def _hy_gate(conv, uc, gate_col, z_col, skip):
    n, w = conv.shape
    tm = _row_tile(n, 512)
    gc = gate_col // w
    zc = z_col // w
    return pl.pallas_call(
        _hy_gate_kernel,
        out_shape=jax.ShapeDtypeStruct((n, w), F32),
        grid=(n // tm,),
        in_specs=[pl.BlockSpec((tm, w), lambda i: (i, 0)), pl.BlockSpec((tm, w), lambda i: (i, gc)),
                  pl.BlockSpec((tm, w), lambda i: (i, zc)), pl.BlockSpec((1, w), lambda i: (0, 0))],
        out_specs=pl.BlockSpec((tm, w), lambda i: (i, 0)),
        compiler_params=_params(("parallel",), VMEM_LIMIT),
        name="hy_gate",
    )(conv, uc, uc, skip)


def _chan_dft_kernel(u_ref, cs_ref, o_ref):
    o_ref[...] = jnp.dot(u_ref[...], cs_ref[...], precision=HIGHEST, preferred_element_type=F32)


def _chan_dft(u, cs):
    n, c = u.shape
    tm = _row_tile(n, 512)
    return pl.pallas_call(
        _chan_dft_kernel,
        out_shape=jax.ShapeDtypeStruct((n, 2 * c), F32),
        grid=(n // tm,),
        in_specs=[pl.BlockSpec((tm, c), lambda i: (i, 0)), _const_spec(cs.shape)],
        out_specs=pl.BlockSpec((tm, 2 * c), lambda i: (i, 0)),
        compiler_params=_params(("parallel",), VMEM_LIMIT),
        name="chan_dft",
    )(u, cs)


def _fn_out_kernel(p_ref, q_ref, w_ref, o_ref, *, kb, norm):
    w = w_ref[...]
    for jj in range(kb):
        r = (p_ref[:, 0, jj, :] + q_ref[:, 0, jj, :]) * norm
        o_ref[jj] = jnp.dot(r.astype(BF16), w, preferred_element_type=F32)


def _fn_out(spec, fn_w_bf, n):
    n2 = n // DFT_N1
    c = FN_WIDTH
    kb = 8
    s4 = spec.reshape(DFT_N1, 2, n2, 2 * c)
    kern = functools.partial(_fn_out_kernel, kb=kb, norm=1.0 / math.sqrt(n * c))
    out = pl.pallas_call(
        kern,
        out_shape=jax.ShapeDtypeStruct((n2, DFT_N1, c), F32),
        grid=(n2 // kb,),
        in_specs=[pl.BlockSpec((DFT_N1, 1, kb, c), lambda i: (0, 0, i, 0)),
                  pl.BlockSpec((DFT_N1, 1, kb, c), lambda i: (0, 1, i, 1)),
                  _const_spec((c, c))],
        out_specs=pl.BlockSpec((kb, DFT_N1, c), lambda i: (i, 0, 0)),
        compiler_params=_params(("parallel",), VMEM_LIMIT),
        name="fn_out",
    )(s4, s4, fn_w_bf)
    return out.reshape(n, c)


def _odd_out_kernel(c2_ref, x2_ref, z1_ref, s_ref, hyg_ref, yd_ref, fng_ref, x_ref, gate_ref, wo_ref, fg_ref,
                    o_ref):
    z1 = z1_ref[...]
    yc = x2_ref[...] * (c2_ref[...] + s_ref[...] * z1)
    y = jnp.dot((yc * hyg_ref[...]).astype(BF16), wo_ref[0:HY_WIDTH, :], preferred_element_type=F32)
    y = y + jnp.dot((yd_ref[...] * fng_ref[...]).astype(BF16), wo_ref[HY_WIDTH:, :],
                    preferred_element_type=F32)
    xo = x_ref[...] + gate_ref[...] * y
    o_ref[...] = xo * lax.rsqrt(jnp.mean(xo * xo, axis=-1, keepdims=True) + EPS) * fg_ref[...]


def _odd_out(c2, uc, z1, skip1, hy_gate, yd, fn_gate, x, gate, w_out_bf, final_g):
    n, d = x.shape
    tm = _row_tile(n, 512)
    row = lambda w: pl.BlockSpec((tm, w), lambda i: (i, 0))
    vec = lambda w: pl.BlockSpec((1, w), lambda i: (0, 0))
    return pl.pallas_call(
        _odd_out_kernel,
        out_shape=jax.ShapeDtypeStruct((n, d), F32),
        grid=(n // tm,),
        in_specs=[row(HY_WIDTH), pl.BlockSpec((tm, HY_WIDTH), lambda i: (i, 2)), row(HY_WIDTH), vec(HY_WIDTH),
                  row(HY_WIDTH), row(FN_WIDTH), row(FN_WIDTH), row(d), vec(d), _const_spec((d, d)), vec(d)],
        out_specs=row(d),
        compiler_params=_params(("parallel",), VMEM_LIMIT),
        name="odd_out",
    )(c2, uc, z1, skip1, hy_gate, yd, fn_gate, x, gate, w_out_bf, final_g)


def _rope_tables(n):
    t = jnp.arange(n, dtype=jnp.int32)
    row = (t // GRID_W).astype(F32)
    col = (t % GRID_W).astype(F32)
    inv_freq = ROPE_THETA ** (-jnp.arange(ROPE_FREQS, dtype=F32) / ROPE_FREQS)
    ang_r = row[:, None] * inv_freq
    ang_c = col[:, None] * inv_freq
    ang = jnp.concatenate([ang_r, ang_r, ang_c, ang_c], axis=-1)
    sign = jnp.tile(jnp.concatenate([-jnp.ones((ROPE_FREQS,), F32), jnp.ones((ROPE_FREQS,), F32)]), 2)
    return jnp.cos(ang), jnp.sin(ang) * sign


def _hyena_embedding(n):
    t = jnp.linspace(0.0, 1.0, n, dtype=F32)[:, None]
    w = 2.0 * math.pi * jnp.arange(n, dtype=F32)[:, None] / n
    f = jnp.linspace(1e-4, HY_BANDS - 1, HY_BANDS, dtype=F32)[None, :]
    emb = jnp.concatenate([t, jnp.cos(f * w), -jnp.sin(f * w)], axis=-1)
    return jnp.pad(emb, ((0, 0), (0, LANES - HY_EMB)))


def kernel(x, c, ctx, c_ctx, w_mod, b_mod, norm_g, ev_w_in, ev_w_out, pool_w, pool_scale, q_norm_g, k_norm_g,
           od_w_in, od_w_out, hy_conv_w, hy_conv_b, hy_w1, hy_b1, hy_w2, hy_b2, hy_w3, hy_freq, hy_skip, fn_w,
           final_g):
    n, d = x.shape[1], x.shape[2]
    nc = ctx.shape[1]
    x0 = x[0]
    ctx0 = ctx[0]
    vec = lambda a: a.reshape(1, -1)

    cond = jnp.zeros((8, d), F32).at[0].set(c[0]).at[1].set(c_ctx)
    mod = _modulation(cond, w_mod, b_mod)
    shift0, scale0, gate0 = (mod[0, 0:1, k * d:(k + 1) * d] for k in range(3))
    cshift0, cscale0 = mod[0, 1:2, 0:d], mod[0, 1:2, d:2 * d]
    shift1, scale1, gate1 = (mod[1, 0:1, k * d:(k + 1) * d] for k in range(3))

    w_in0 = ev_w_in[0].astype(BF16)
    cos, sin = _rope_tables(n)
    g0 = vec(norm_g[0])
    qg, kg = vec(q_norm_g[0]), vec(k_norm_g[0])
    a_val, a_gate, q, k, v, b_gate = _even_in(x0, g0, scale0, shift0, w_in0, qg, kg, cos, sin)
    ones = jnp.ones((nc, HEAD_DIM), F32)
    _, _, _, ck, cv, _ = _even_in(ctx0, g0, cscale0, cshift0, w_in0, qg, kg, ones, jnp.zeros_like(ones))
    yb = _attention(q, ck, cv, k, v, b_gate)
    pool_bd = jax.scipy.linalg.block_diag(*[pool_w[0, gi] for gi in range(pool_w.shape[1])]).astype(BF16)
    x1 = _even_out(a_val, a_gate, yb, x0, gate0, pool_bd, vec(pool_scale[0]), ev_w_out[0].astype(BF16))

    hy_in, hy_gate, fn_in, fn_gate = _odd_in(x1, vec(norm_g[1]), scale1, shift1, od_w_in[0].astype(BF16))
    uc = _conv3(hy_in, hy_conv_w[0], vec(hy_conv_b[0]))

    max_decay = math.log(HY_DECAY_TARGET) / HY_FAST_DECAY
    min_decay = math.log(HY_DECAY_TARGET) / HY_SLOW_DECAY
    deltas = jnp.linspace(min_decay, max_decay, HY_WIDTH, dtype=F32)[None, :]
    w1p = jnp.pad(hy_w1[0], ((0, LANES - HY_EMB), (0, 0)))
    emb = _hyena_embedding(n)
    emb_rev = jnp.concatenate([emb[0:1], emb[:0:-1]], axis=0)
    k_lo, k_hi, mass = _hyena_filters(emb, emb_rev, w1p, vec(hy_b1[0]), hy_w2[0], vec(hy_b2[0]),
                                      hy_w3[0].astype(BF16), vec(hy_freq[0]), deltas)

    n2 = n // DFT_N1
    m_full, _, big_fwd, big_inv = _dft_tables(2 * n, n2, True, 2 * DFT_N1)
    m_half, m_inv, _, _ = _dft_tables(2 * n, n2, True, DFT_N1)
    k_spec = _dft_fwd([k_lo, k_hi], HY_ORDER * HY_WIDTH, m_full, big_fwd, mass=mass)
    conv1 = _hyena_conv(uc, 0, k_spec, 0, m_half, m_inv, big_fwd, big_inv)
    z1 = _hy_gate(conv1, uc, HY_WIDTH, 0, hy_skip[0, 0:1])
    conv2 = _hyena_conv(z1, 0, k_spec, 1, m_half, m_inv, big_fwd, big_inv)

    ch = np.arange(FN_WIDTH, dtype=np.float64)
    ph = 2.0 * np.pi * np.outer(ch, ch) / FN_WIDTH
    cs = jnp.asarray(np.concatenate([np.cos(ph), np.sin(ph)], axis=1).astype(np.float32))
    pq = _chan_dft(fn_in, cs)
    f_half, _, f_big, _ = _dft_tables(n, n2, False, DFT_N1)
    pq_spec = _dft_fwd([pq], 2 * FN_WIDTH, f_half, f_big)
    yd = _fn_out(pq_spec, fn_w[0].astype(BF16), n)

    out = _odd_out(conv2, uc, z1, hy_skip[0, 1:2], hy_gate, yd, fn_gate, x1, gate1, od_w_out[0].astype(BF16),
                   vec(final_g))
    return out[None]
```

```python
import functools
import math

import numpy as np
import jax
import jax.numpy as jnp
from jax import lax
from jax.experimental import pallas as pl
from jax.experimental.pallas import tpu as pltpu

F32 = jnp.float32
BF16 = jnp.bfloat16
HIGHEST = lax.Precision.HIGHEST

EPS = 1e-6
GRID_W = 64
HEAD_DIM = 128
ROPE_FREQS = 32
ROPE_THETA = 10000.0
N_Q_HEADS = 6
N_KV_HEADS = 2
Q_PER_KV = N_Q_HEADS // N_KV_HEADS
POOL_WIDTH = 256
POOL_GROUP_DIM = 64
POOL_WINDOWS = (2, 4, 8, 16)
POOL_HALO = 8
ATT_WIDTH = N_Q_HEADS * HEAD_DIM
KV_WIDTH = N_KV_HEADS * HEAD_DIM
HY_WIDTH = 768
HY_ORDER = 2
HY_EMB = 33
HY_BANDS = 16
HY_HIDDEN = 64
FN_WIDTH = 256
HY_DECAY_TARGET = 1e-2
HY_FAST_DECAY = 0.3
HY_SLOW_DECAY = 1.5

LANES = 128
DFT_N1 = 128
VMEM_LIMIT = 56 * 1024 * 1024


def _row_tile(n, pref):
    t = min(pref, n)
    assert n % t == 0
    return t


def _silu(x):
    return x * jax.nn.sigmoid(x)


def _params(sem, vmem=None):
    return pltpu.CompilerParams(dimension_semantics=sem, vmem_limit_bytes=vmem)


def _const_spec(shape):
    nd = len(shape)
    return pl.BlockSpec(shape, lambda *_: (0,) * nd, pipeline_mode=pl.Buffered(1))


def _mod_kernel(cond_ref, w_ref, b_ref, o_ref):
    s = _silu(cond_ref[...])
    o_ref[0] = jnp.dot(s, w_ref[0], precision=HIGHEST, preferred_element_type=F32) + b_ref[0]


def _modulation(cond, w_mod, b_mod):
    depth, d, d3 = w_mod.shape
    tn = 1024
    return pl.pallas_call(
        _mod_kernel,
        out_shape=jax.ShapeDtypeStruct((depth, 8, d3), F32),
        grid=(depth, d3 // tn),
        in_specs=[pl.BlockSpec((8, d), lambda i, j: (0, 0)),
                  pl.BlockSpec((1, d, tn), lambda i, j: (i, 0, j)),
                  pl.BlockSpec((1, 1, tn), lambda i, j: (i, 0, j))],
        out_specs=pl.BlockSpec((1, 8, tn), lambda i, j: (i, 0, j)),
        compiler_params=_params(("arbitrary", "arbitrary")),
        name="modulation",
    )(cond, w_mod, b_mod.reshape(depth, 1, d3))


def _norm_mod(x, g, scale, shift):
    y = x * lax.rsqrt(jnp.mean(x * x, axis=-1, keepdims=True) + EPS)
    return (y * g) * (1.0 + scale) + shift


def _even_in_kernel(x_ref, g_ref, sc_ref, sh_ref, w_ref, qg_ref, kg_ref, cos_ref, sin_ref,
                    aval_ref, agate_ref, q_ref, k_ref, v_ref, bgate_ref):
    h = _norm_mod(x_ref[...], g_ref[...], sc_ref[...], sh_ref[...])
    p = jnp.dot(h.astype(BF16), w_ref[...], preferred_element_type=F32)
    aval_ref[...] = p[:, 0:POOL_WIDTH]
    agate_ref[...] = _silu(p[:, POOL_WIDTH:2 * POOL_WIDTH])
    cos = cos_ref[...]
    sin = sin_ref[...]
    lane = lax.broadcasted_iota(jnp.int32, cos.shape, 1)
    low_half = (lane % (2 * ROPE_FREQS)) < ROPE_FREQS

    def head(xh, g):
        y = xh * lax.rsqrt(jnp.mean(xh * xh, axis=-1, keepdims=True) + EPS) * g
        rot = jnp.where(low_half, pltpu.roll(y, HEAD_DIM - ROPE_FREQS, 1), pltpu.roll(y, ROPE_FREQS, 1))
        return y * cos + rot * sin

    q0 = 2 * POOL_WIDTH
    for hq in range(N_Q_HEADS):
        sl = slice(hq * HEAD_DIM, (hq + 1) * HEAD_DIM)
        q_ref[:, sl] = head(p[:, q0 + hq * HEAD_DIM:q0 + (hq + 1) * HEAD_DIM], qg_ref[...]).astype(BF16)
    k0 = q0 + ATT_WIDTH
    for hk in range(N_KV_HEADS):
        sl = slice(hk * HEAD_DIM, (hk + 1) * HEAD_DIM)
        k_ref[:, sl] = head(p[:, k0 + hk * HEAD_DIM:k0 + (hk + 1) * HEAD_DIM], kg_ref[...]).astype(BF16)
    v0 = k0 + KV_WIDTH
    v_ref[...] = p[:, v0:v0 + KV_WIDTH].astype(BF16)
    b0 = v0 + KV_WIDTH
    bgate_ref[...] = _silu(p[:, b0:b0 + ATT_WIDTH])


def _even_in(x, g, scale, shift, w_bf, qg, kg, cos, sin):
    n, d = x.shape
    n_in = w_bf.shape[1]
    tm = _row_tile(n, 512)
    row = lambda w: pl.BlockSpec((tm, w), lambda i: (i, 0))
    vec = lambda w: pl.BlockSpec((1, w), lambda i: (0, 0))
    return pl.pallas_call(
        _even_in_kernel,
        out_shape=(jax.ShapeDtypeStruct((n, POOL_WIDTH), F32), jax.ShapeDtypeStruct((n, POOL_WIDTH), F32),
                   jax.ShapeDtypeStruct((n, ATT_WIDTH), BF16), jax.ShapeDtypeStruct((n, KV_WIDTH), BF16),
                   jax.ShapeDtypeStruct((n, KV_WIDTH), BF16), jax.ShapeDtypeStruct((n, ATT_WIDTH), F32)),
        grid=(n // tm,),
        in_specs=[row(d), vec(d), vec(d), vec(d), _const_spec((d, n_in)), vec(HEAD_DIM), vec(HEAD_DIM),
                  row(HEAD_DIM), row(HEAD_DIM)],
        out_specs=(row(POOL_WIDTH), row(POOL_WIDTH), row(ATT_WIDTH), row(KV_WIDTH), row(KV_WIDTH),
                   row(ATT_WIDTH)),
        compiler_params=_params(("parallel",), VMEM_LIMIT),
        name="even_in",
    )(x, g, scale, shift, w_bf, qg, kg, cos, sin)


ATT_ROW_CHUNK = 32


def _attn_kernel(q_ref, ck_ref, cv_ref, k_ref, v_ref, bg_ref, o_ref, s_sc, p_sc, m_sc, al_sc, acc_sc,
                 *, tq, tk, n_kv_tiles):
    c = (HEAD_DIM ** -0.5) * math.log2(math.e)
    rows = Q_PER_KV * tq
    nc = ck_ref.shape[0]
    for h in range(N_KV_HEADS):
        hs = slice(h * HEAD_DIM, (h + 1) * HEAD_DIM)
        qs = jnp.concatenate(
            [q_ref[:, (Q_PER_KV * h + g) * HEAD_DIM:(Q_PER_KV * h + g + 1) * HEAD_DIM] for g in range(Q_PER_KV)],
            axis=0)

        def scores(buf, kt, width):
            s_sc[buf, :, 0:width] = lax.dot_general(qs, kt, (((1,), (1,)), ((), ())),
                                                    preferred_element_type=F32)

        def softmax_pv(buf, vt, width):
            for r in range(rows // ATT_ROW_CHUNK):
                rs = slice(r * ATT_ROW_CHUNK, (r + 1) * ATT_ROW_CHUNK)
                cols = [s_sc[buf, rs, j * LANES:(j + 1) * LANES] for j in range(width // LANES)]
                mx = functools.reduce(jnp.maximum, cols)
                m_old = m_sc[rs, :]
                m_new = jnp.maximum(m_old, jnp.max(mx, axis=-1, keepdims=True) * c)
                for j, sj in enumerate(cols):
                    p_sc[buf, rs, j * LANES:(j + 1) * LANES] = jnp.exp2(sj * c - m_new).astype(BF16)
                al_sc[rs, :] = jnp.exp2(m_old - m_new)
                m_sc[rs, :] = m_new
            v_ext = jnp.concatenate([vt, jnp.ones((width, HEAD_DIM), BF16)], axis=1)
            pv = jnp.dot(p_sc[buf, :, 0:width], v_ext, preferred_element_type=F32)
            al = al_sc[...]
            for j in range(2):
                js = slice(j * HEAD_DIM, (j + 1) * HEAD_DIM)
                acc_sc[:, js] = al * acc_sc[:, js] + pv[:, js]

        m_sc[...] = jnp.full(m_sc.shape, -1e30, F32)
        acc_sc[...] = jnp.zeros(acc_sc.shape, F32)
        scores(0, ck_ref[:, hs], nc)
        softmax_pv(0, cv_ref[:, hs], nc)
        scores(0, k_ref[0:tk, hs], tk)

        def body(i, carry):
            r0 = pl.multiple_of(2 * i * tk, tk)
            r1 = pl.multiple_of(r0 + tk, tk)
            r2 = pl.multiple_of(jnp.minimum(r0 + 2 * tk, (n_kv_tiles - 1) * tk), tk)
            scores(1, k_ref[pl.ds(r1, tk), hs], tk)
            softmax_pv(0, v_ref[pl.ds(r0, tk), hs], tk)
            scores(0, k_ref[pl.ds(r2, tk), hs], tk)
            softmax_pv(1, v_ref[pl.ds(r1, tk), hs], tk)
            return carry

        lax.fori_loop(0, n_kv_tiles // 2, body, 0)
        out = acc_sc[:, 0:HEAD_DIM] / acc_sc[:, HEAD_DIM:2 * HEAD_DIM]
        for g in range(Q_PER_KV):
            cs = slice((Q_PER_KV * h + g) * HEAD_DIM, (Q_PER_KV * h + g + 1) * HEAD_DIM)
            o_ref[:, cs] = (out[g * tq:(g + 1) * tq] * bg_ref[:, cs]).astype(o_ref.dtype)


def _attention(q, ck, cv, k, v, bgate):
    n = q.shape[0]
    nc = ck.shape[0]
    tq = _row_tile(n, 256)
    tk = _row_tile(n, 1024) if n >= 2048 else n // 2
    assert (n // tk) % 2 == 0 and nc <= tk and tk % LANES == 0 and nc % LANES == 0
    rows = Q_PER_KV * tq
    kern = functools.partial(_attn_kernel, tq=tq, tk=tk, n_kv_tiles=n // tk)
    return pl.pallas_call(
        kern,
        out_shape=jax.ShapeDtypeStruct((n, ATT_WIDTH), F32),
        grid=(n // tq,),
        in_specs=[pl.BlockSpec((tq, ATT_WIDTH), lambda i: (i, 0)),
                  _const_spec((nc, KV_WIDTH)), _const_spec((nc, KV_WIDTH)),
                  _const_spec((n, KV_WIDTH)), _const_spec((n, KV_WIDTH)),
                  pl.BlockSpec((tq, ATT_WIDTH), lambda i: (i, 0))],
        out_specs=pl.BlockSpec((tq, ATT_WIDTH), lambda i: (i, 0)),
        scratch_shapes=[pltpu.VMEM((2, rows, tk), F32), pltpu.VMEM((2, rows, tk), BF16),
                        pltpu.VMEM((rows, LANES), F32), pltpu.VMEM((rows, LANES), F32),
                        pltpu.VMEM((rows, 2 * HEAD_DIM), F32)],
        compiler_params=_params(("parallel",), VMEM_LIMIT),
        name="attention",
    )(q, ck, cv, k, v, bgate)


def _even_out_kernel(a_ref, ap_ref, an_ref, ag_ref, yb_ref, x_ref, gate_ref, pw_ref, ps_ref, wo_ref, o_ref,
                     *, tm, n_rows):
    i = pl.program_id(0)
    n_tiles = pl.num_programs(0)
    u = a_ref[...]
    prev = jnp.where(i > 0, ap_ref[...], 0.0)
    nxt = jnp.where(i < n_tiles - 1, an_ref[...], 0.0)
    e = jnp.concatenate([prev, u, nxt], axis=0)
    n = tm + 2 * POOL_HALO
    s2 = e[0:n - 1] + e[1:n]
    s4 = s2[0:n - 3] + s2[2:n - 1]
    s8 = s4[0:n - 7] + s4[4:n - 3]
    s16 = s8[0:n - 15] + s8[8:n - 7]
    lane = lax.broadcasted_iota(jnp.int32, (tm, POOL_WIDTH), 1)
    grp = lane // POOL_GROUP_DIM
    win = jnp.where(grp == 0, s2[7:7 + tm],
                    jnp.where(grp == 1, s4[6:6 + tm], jnp.where(grp == 2, s8[4:4 + tm], s16[0:tm])))
    half = jnp.where(grp == 0, 1, jnp.where(grp == 1, 2, jnp.where(grp == 2, 4, 8)))
    t = i * tm + lax.broadcasted_iota(jnp.int32, (tm, POOL_WIDTH), 0)
    cnt = (jnp.minimum(t + half, n_rows) - jnp.maximum(t - half, 0)).astype(F32)
    d = win / cnt - u
    ya = jnp.dot(d.astype(BF16), pw_ref[...], preferred_element_type=F32) * ps_ref[...]
    ya = ya * ag_ref[...]
    y = jnp.dot(ya.astype(BF16), wo_ref[0:POOL_WIDTH, :], preferred_element_type=F32)
    y = y + jnp.dot(yb_ref[...].astype(BF16), wo_ref[POOL_WIDTH:, :], preferred_element_type=F32)
    o_ref[...] = x_ref[...] + gate_ref[...] * y


def _even_out(a_val, a_gate, yb, x, gate, pool_bd, pool_scale, w_out_bf):
    n, d = x.shape
    tm = _row_tile(n, 512)
    hb = tm // POOL_HALO
    last = n // POOL_HALO - 1
    row = lambda w: pl.BlockSpec((tm, w), lambda i: (i, 0))
    vec = lambda w: pl.BlockSpec((1, w), lambda i: (0, 0))
    kern = functools.partial(_even_out_kernel, tm=tm, n_rows=n)
    return pl.pallas_call(
        kern,
        out_shape=jax.ShapeDtypeStruct((n, d), F32),
        grid=(n // tm,),
        in_specs=[row(POOL_WIDTH),
                  pl.BlockSpec((POOL_HALO, POOL_WIDTH), lambda i: (jnp.maximum(i * hb - 1, 0), 0)),
                  pl.BlockSpec((POOL_HALO, POOL_WIDTH), lambda i: (jnp.minimum((i + 1) * hb, last), 0)),
                  row(POOL_WIDTH), row(ATT_WIDTH), row(d), vec(d),
                  _const_spec((POOL_WIDTH, POOL_WIDTH)), vec(POOL_WIDTH), _const_spec((d, d))],
        out_specs=row(d),
        compiler_params=_params(("parallel",), VMEM_LIMIT),
        name="even_out",
    )(a_val, a_val, a_val, a_gate, yb, x, gate, pool_bd, pool_scale, w_out_bf)


def _odd_in_kernel(x_ref, g_ref, sc_ref, sh_ref, w_ref, hy_ref, hyg_ref, fn_ref, fng_ref):
    h = _norm_mod(x_ref[...], g_ref[...], sc_ref[...], sh_ref[...])
    p = jnp.dot(h.astype(BF16), w_ref[...], preferred_element_type=F32)
    c0 = (HY_ORDER + 1) * HY_WIDTH
    hy_ref[...] = p[:, 0:c0]
    hyg_ref[...] = _silu(p[:, c0:c0 + HY_WIDTH])
    fn_ref[...] = p[:, c0 + HY_WIDTH:c0 + HY_WIDTH + FN_WIDTH]
    fng_ref[...] = _silu(p[:, c0 + HY_WIDTH + FN_WIDTH:])


def _odd_in(x, g, scale, shift, w_bf):
    n, d = x.shape
    n_in = w_bf.shape[1]
    tm = _row_tile(n, 512)
    row = lambda w: pl.BlockSpec((tm, w), lambda i: (i, 0))
    vec = lambda w: pl.BlockSpec((1, w), lambda i: (0, 0))
    c0 = (HY_ORDER + 1) * HY_WIDTH
    return pl.pallas_call(
        _odd_in_kernel,
        out_shape=(jax.ShapeDtypeStruct((n, c0), F32), jax.ShapeDtypeStruct((n, HY_WIDTH), F32),
                   jax.ShapeDtypeStruct((n, FN_WIDTH), F32), jax.ShapeDtypeStruct((n, FN_WIDTH), F32)),
        grid=(n // tm,),
        in_specs=[row(d), vec(d), vec(d), vec(d), _const_spec((d, n_in))],
        out_specs=(row(c0), row(HY_WIDTH), row(FN_WIDTH), row(FN_WIDTH)),
        compiler_params=_params(("parallel",), VMEM_LIMIT),
        name="odd_in",
    )(x, g, scale, shift, w_bf)


def _conv3_kernel(u_ref, up_ref, un_ref, w_ref, b_ref, o_ref, *, tm):
    i = pl.program_id(0)
    n_tiles = pl.num_programs(0)
    u = u_ref[...]
    prev = jnp.where(i > 0, up_ref[7:8, :], 0.0)
    nxt = jnp.where(i < n_tiles - 1, un_ref[0:1, :], 0.0)
    e = jnp.concatenate([prev, u, nxt], axis=0)
    o_ref[...] = e[0:tm] * w_ref[0:1, :] + u * w_ref[1:2, :] + e[2:tm + 2] * w_ref[2:3, :] + b_ref[...]


def _conv3(u, w, b):
    n, c = u.shape
    tm = _row_tile(n, 512)
    hb = tm // 8
    last = n // 8 - 1
    kern = functools.partial(_conv3_kernel, tm=tm)
    return pl.pallas_call(
        kern,
        out_shape=jax.ShapeDtypeStruct((n, c), F32),
        grid=(n // tm,),
        in_specs=[pl.BlockSpec((tm, c), lambda i: (i, 0)),
                  pl.BlockSpec((8, c), lambda i: (jnp.maximum(i * hb - 1, 0), 0)),
                  pl.BlockSpec((8, c), lambda i: (jnp.minimum((i + 1) * hb, last), 0)),
                  pl.BlockSpec((3, c), lambda i: (0, 0)), pl.BlockSpec((1, c), lambda i: (0, 0))],
        out_specs=pl.BlockSpec((tm, c), lambda i: (i, 0)),
        compiler_params=_params(("parallel",), VMEM_LIMIT),
        name="conv3",
    )(u, u, u, w, b)


def _filter_kernel(emb_ref, embr_ref, w1_ref, b1_ref, w2_ref, b2_ref, w3_ref, fr_ref, dl_ref,
                   lo_ref, hi_ref, mass_ref, *, tm, n_rows):
    i = pl.program_id(0)
    fr = fr_ref[...]

    def ffn(e):
        a = jnp.sin(fr * (jnp.dot(e, w1_ref[...], precision=HIGHEST, preferred_element_type=F32) + b1_ref[...]))
        return jnp.sin(fr * (jnp.dot(a, w2_ref[...], precision=HIGHEST, preferred_element_type=F32)
                             + b2_ref[...]))

    a_f = ffn(emb_ref[...]).astype(BF16)
    a_b = ffn(embr_ref[...]).astype(BF16)
    j = i * tm + lax.broadcasted_iota(jnp.int32, (tm, 1), 0)
    pos_b = jnp.where(j == 0, 0, n_rows - j)
    dl = jnp.abs(dl_ref[...])
    inv = 1.0 / (n_rows - 1)
    decay_f = jnp.exp(-(j.astype(F32) * inv) * dl)
    decay_b = jnp.exp(-(pos_b.astype(F32) * inv) * dl)

    @pl.when(i == 0)
    def _():
        mass_ref[...] = jnp.zeros_like(mass_ref)

    for o in range(HY_ORDER):
        cf = (2 * o) * HY_WIDTH
        cb = (2 * o + 1) * HY_WIDTH
        hf = jnp.dot(a_f, w3_ref[:, cf:cf + HY_WIDTH], preferred_element_type=F32) * decay_f
        hb = jnp.dot(a_b, w3_ref[:, cb:cb + HY_WIDTH], preferred_element_type=F32) * decay_b
        os_ = slice(o * HY_WIDTH, (o + 1) * HY_WIDTH)
        lo_ref[:, os_] = hf
        hi_ref[:, os_] = jnp.where(j == 0, 0.0, -hb)
        mass_ref[:, os_] += (jnp.sum(jnp.abs(hf), axis=0, keepdims=True)
                             + jnp.sum(jnp.abs(hb), axis=0, keepdims=True))


def _hyena_filters(emb, emb_rev, w1p, b1, w2, b2, w3_bf, freq, deltas):
    n = emb.shape[0]
    tm = _row_tile(n, 512)
    nh = HY_ORDER * HY_WIDTH
    kern = functools.partial(_filter_kernel, tm=tm, n_rows=n)
    full = lambda a: pl.BlockSpec(a.shape, lambda i: (0,) * a.ndim)
    row = lambda w: pl.BlockSpec((tm, w), lambda i: (i, 0))
    return pl.pallas_call(
        kern,
        out_shape=(jax.ShapeDtypeStruct((n, nh), F32), jax.ShapeDtypeStruct((n, nh), F32),
                   jax.ShapeDtypeStruct((1, nh), F32)),
        grid=(n // tm,),
        in_specs=[row(emb.shape[1]), row(emb.shape[1]), full(w1p), full(b1), full(w2), full(b2),
                  full(w3_bf), full(freq), full(deltas)],
        out_specs=(row(nh), row(nh), pl.BlockSpec((1, nh), lambda i: (0, 0))),
        compiler_params=_params(("arbitrary",), VMEM_LIMIT),
        name="hyena_filters",
    )(emb, emb_rev, w1p, b1, w2, b2, w3_bf, freq, deltas)


DFT_PARTS = 1


def _table_parts(a):
    a32 = jnp.asarray(a.astype(np.float32))
    hi = a32.astype(BF16)
    if DFT_PARTS == 1:
        return hi[None]
    lo = (a32 - hi.astype(F32)).astype(BF16)
    return jnp.stack([hi, lo])


def _split_bf16(x):
    hi = x.astype(BF16)
    if DFT_PARTS == 1:
        return (hi,)
    return (hi, (x - hi.astype(F32)).astype(BF16))


def _ref_parts(ref, *idx):
    return tuple(ref[(q,) + idx] for q in range(ref.shape[0]))


def _dot_parts(lhs, rhs):
    acc = jnp.dot(lhs[0], rhs[0], preferred_element_type=F32)
    if len(lhs) > 1:
        acc = acc + jnp.dot(lhs[1], rhs[0], preferred_element_type=F32)
    if len(rhs) > 1:
        acc = acc + jnp.dot(lhs[0], rhs[1], preferred_element_type=F32)
    return acc


def _dft_tables(n_total, n2, half_shift, n1_used):
    n1_full = n_total // n2
    sh = 0.5 if half_shift else 0.0
    k1 = np.arange(DFT_N1, dtype=np.float64)[None, :, None] + sh
    n1 = np.arange(n1_used, dtype=np.float64)[None, None, :]
    nn2 = np.arange(n2, dtype=np.float64)[:, None, None]
    ph = -2.0 * np.pi * (n1 * k1 / n1_full + nn2 * k1 / n_total)
    fwd = np.stack([np.cos(ph), np.sin(ph)], axis=2).reshape(n2, 2 * DFT_N1, n1_used)
    ph_i = ph[:, :, :DFT_N1]
    inv = np.stack([np.cos(ph_i), np.sin(ph_i)], axis=2).reshape(n2, 2 * DFT_N1, DFT_N1)
    inv = (2.0 / n_total) * inv.transpose(0, 2, 1)
    k2 = np.arange(n2, dtype=np.float64)
    ph2 = -2.0 * np.pi * np.outer(k2, k2) / n2
    fr, fi = np.cos(ph2), np.sin(ph2)
    big_fwd = np.block([[fr, -fi], [fi, fr]])
    big_inv = np.block([[fr, fi], [-fi, fr]])
    return _table_parts(fwd), _table_parts(inv), _table_parts(big_fwd), _table_parts(big_inv)


def _scratch_pitch(n2):
    return n2 + 8


DFT_STEP = 16


def _stage1_fwd(x_refs, m_ref, a_sc, step, *, n2, pitch):
    for u in range(m_ref.shape[1]):
        nn = step * m_ref.shape[1] + u
        xs = [xr[pl.ds(nn, DFT_N1, stride=n2), :] for xr in x_refs]
        xs = xs[0] if len(xs) == 1 else jnp.concatenate(xs, axis=0)
        a_sc[pl.ds(nn, 2 * DFT_N1, stride=pitch), :] = _dot_parts(_ref_parts(m_ref, u), _split_bf16(xs))


def _load_cplx(a_sc, k1, *, n2, pitch):
    r0 = pl.multiple_of(k1 * 2 * pitch, 8)
    return jnp.concatenate([a_sc[pl.ds(r0, n2), :], a_sc[pl.ds(r0 + pitch, n2), :]], axis=0)


def _phase_steps(n2):
    n2g = min(DFT_STEP, n2)
    return n2g, n2 // n2g, DFT_N1 // DFT_STEP


def _dft_fwd_kernel(*refs, n_x, n2, pitch, ja, scaled):
    x_refs = refs[:n_x]
    rest = refs[n_x:]
    if scaled:
        mass_ref, rest = rest[0], rest[1:]
    m_ref, big_ref, o_ref, a_sc = rest
    j = pl.program_id(1)

    @pl.when(j < ja)
    def _():
        _stage1_fwd(x_refs, m_ref, a_sc, j, n2=n2, pitch=pitch)

    @pl.when(j >= ja)
    def _():
        big = _ref_parts(big_ref)
        for g in range(DFT_STEP):
            blk = _load_cplx(a_sc, (j - ja) * DFT_STEP + g, n2=n2, pitch=pitch)
            out = _dot_parts(big, _split_bf16(blk))
            if scaled:
                out = out * (1.0 / (mass_ref[...] + EPS))
            o_ref[g * 2 * n2:(g + 1) * 2 * n2, :] = out


def _dft_fwd(xs, width, m_fwd, big, mass=None):
    n = xs[0].shape[0]
    n2 = n // DFT_N1
    pitch = _scratch_pitch(n2)
    n2g, ja, jb = _phase_steps(n2)
    wt = LANES
    rows = 2 * DFT_N1 * n2
    kern = functools.partial(_dft_fwd_kernel, n_x=len(xs), n2=n2, pitch=pitch, ja=ja, scaled=mass is not None)
    x_spec = pl.BlockSpec((n, wt), lambda i, j: (0, i), pipeline_mode=pl.Buffered(1))
    ops = list(xs)
    specs = [x_spec] * len(xs)
    if mass is not None:
        ops.append(mass)
        specs.append(pl.BlockSpec((1, wt), lambda i, j: (0, i)))
    m_spec = pl.BlockSpec((m_fwd.shape[0], n2g) + m_fwd.shape[2:],
                          lambda i, j: (0, jnp.minimum(j, ja - 1), 0, 0))
    return pl.pallas_call(
        kern,
        out_shape=jax.ShapeDtypeStruct((rows, width), F32),
        grid=(width // wt, ja + jb),
        in_specs=specs + [m_spec, _const_spec(big.shape)],
        out_specs=pl.BlockSpec((DFT_STEP * 2 * n2, wt), lambda i, j: (jnp.maximum(j - ja, 0), i)),
        scratch_shapes=[pltpu.VMEM((2 * DFT_N1 * pitch, wt), F32)],
        compiler_params=_params(("parallel", "arbitrary"), VMEM_LIMIT),
        name="dft_fwd",
    )(*ops, m_fwd, big)


def _hyena_conv_kernel(x_ref, k_ref, m_ref, minv_ref, bigf_ref, bigi_ref, y_ref, a_sc, *, n2, pitch, ja, jb):
    j = pl.program_id(1)

    @pl.when(j < ja)
    def _():
        _stage1_fwd([x_ref], m_ref, a_sc, j, n2=n2, pitch=pitch)

    @pl.when((j >= ja) & (j < ja + jb))
    def _():
        bigf = _ref_parts(bigf_ref)
        bigi = _ref_parts(bigi_ref)
        for g in range(DFT_STEP):
            k1 = (j - ja) * DFT_STEP + g
            z = _dot_parts(bigf, _split_bf16(_load_cplx(a_sc, k1, n2=n2, pitch=pitch)))
            zr, zi = z[0:n2], z[n2:2 * n2]
            kr = k_ref[g * 2 * n2:g * 2 * n2 + n2, :]
            ki = k_ref[g * 2 * n2 + n2:(g + 1) * 2 * n2, :]
            prod = jnp.concatenate([zr * kr - zi * ki, zr * ki + zi * kr], axis=0)
            b = _dot_parts(bigi, _split_bf16(prod))
            r0 = pl.multiple_of(k1 * 2 * pitch, 8)
            a_sc[pl.ds(r0, n2), :] = b[0:n2]
            a_sc[pl.ds(r0 + pitch, n2), :] = b[n2:2 * n2]

    @pl.when(j >= ja + jb)
    def _():
        for u in range(minv_ref.shape[1]):
            nn = (j - ja - jb) * minv_ref.shape[1] + u
            bs = a_sc[pl.ds(nn, 2 * DFT_N1, stride=pitch), :]
            y_ref[pl.ds(nn, DFT_N1, stride=n2), :] = _dot_parts(_ref_parts(minv_ref, u), _split_bf16(bs))


def _hyena_conv(x, col0, kspec, order, m_fwd, m_inv, big_fwd, big_inv):
    n = x.shape[0]
    n2 = n // DFT_N1
    pitch = _scratch_pitch(n2)
    n2g, ja, jb = _phase_steps(n2)
    wt = LANES
    c0 = col0 // wt
    ck = order * HY_WIDTH // wt
    kern = functools.partial(_hyena_conv_kernel, n2=n2, pitch=pitch, ja=ja, jb=jb)
    return pl.pallas_call(
        kern,
        out_shape=jax.ShapeDtypeStruct((n, HY_WIDTH), F32),
        grid=(HY_WIDTH // wt, ja + jb + ja),
        in_specs=[pl.BlockSpec((n, wt), lambda i, j: (0, c0 + i), pipeline_mode=pl.Buffered(1)),
                  pl.BlockSpec((DFT_STEP * 2 * n2, wt), lambda i, j: (jnp.clip(j - ja, 0, jb - 1), ck + i)),
                  pl.BlockSpec((m_fwd.shape[0], n2g) + m_fwd.shape[2:],
                               lambda i, j: (0, jnp.minimum(j, ja - 1), 0, 0)),
                  pl.BlockSpec((m_inv.shape[0], n2g) + m_inv.shape[2:],
                               lambda i, j: (0, jnp.clip(j - ja - jb, 0, ja - 1), 0, 0)),
                  _const_spec(big_fwd.shape), _const_spec(big_inv.shape)],
        out_specs=pl.BlockSpec((n, wt), lambda i, j: (0, i)),
        scratch_shapes=[pltpu.VMEM((2 * DFT_N1 * pitch, wt), F32)],
        compiler_params=_params(("parallel", "arbitrary"), VMEM_LIMIT),
        name="hyena_conv",
    )(x, kspec, m_fwd, m_inv, big_fwd, big_inv)


def _hy_gate_kernel(c_ref, g_ref, z_ref, s_ref, o_ref):
    z = z_ref[...]
    o_ref[...] = g_ref[...] * (c_ref[...] + s_ref[...] * z)


def _hy_gate(conv, uc, gate_col, z_col, skip):
    n, w = conv.shape
    tm = _row_tile(n, 512)
    gc = gate_col // w
    zc = z_col // w
    return pl.pallas_call(
        _hy_gate_kernel,
        out_shape=jax.ShapeDtypeStruct((n, w), F32),
        grid=(n // tm,),
        in_specs=[pl.BlockSpec((tm, w), lambda i: (i, 0)), pl.BlockSpec((tm, w), lambda i: (i, gc)),
                  pl.BlockSpec((tm, w), lambda i: (i, zc)), pl.BlockSpec((1, w), lambda i: (0, 0))],
        out_specs=pl.BlockSpec((tm, w), lambda i: (i, 0)),
        compiler_params=_params(("parallel",), VMEM_LIMIT),
        name="hy_gate",
    )(conv, uc, uc, skip)


def _chan_dft_kernel(u_ref, cs_ref, o_ref):
    o_ref[...] = jnp.dot(u_ref[...], cs_ref[...], precision=HIGHEST, preferred_element_type=F32)


def _chan_dft(u, cs):
    n, c = u.shape
    tm = _row_tile(n, 512)
    return pl.pallas_call(
        _chan_dft_kernel,
        out_shape=jax.ShapeDtypeStruct((n, 2 * c), F32),
        grid=(n // tm,),
        in_specs=[pl.BlockSpec((tm, c), lambda i: (i, 0)), _const_spec(cs.shape)],
        out_specs=pl.BlockSpec((tm, 2 * c), lambda i: (i, 0)),
        compiler_params=_params(("parallel",), VMEM_LIMIT),
        name="chan_dft",
    )(u, cs)


def _fn_out_kernel(p_ref, q_ref, w_ref, o_ref, *, kb, norm):
    w = w_ref[...]
    for jj in range(kb):
        r = (p_ref[:, 0, jj, :] + q_ref[:, 0, jj, :]) * norm
        o_ref[jj] = jnp.dot(r.astype(BF16), w, preferred_element_type=F32)


def _fn_out(spec, fn_w_bf, n):
    n2 = n // DFT_N1
    c = FN_WIDTH
    kb = 8
    s4 = spec.reshape(DFT_N1, 2, n2, 2 * c)
    kern = functools.partial(_fn_out_kernel, kb=kb, norm=1.0 / math.sqrt(n * c))
    out = pl.pallas_call(
        kern,
        out_shape=jax.ShapeDtypeStruct((n2, DFT_N1, c), F32),
        grid=(n2 // kb,),
        in_specs=[pl.BlockSpec((DFT_N1, 1, kb, c), lambda i: (0, 0, i, 0)),
                  pl.BlockSpec((DFT_N1, 1, kb, c), lambda i: (0, 1, i, 1)),
                  _const_spec((c, c))],
        out_specs=pl.BlockSpec((kb, DFT_N1, c), lambda i: (i, 0, 0)),
        compiler_params=_params(("parallel",), VMEM_LIMIT),
        name="fn_out",
    )(s4, s4, fn_w_bf)
    return out.reshape(n, c)


def _odd_out_kernel(c2_ref, x2_ref, z1_ref, s_ref, hyg_ref, yd_ref, fng_ref, x_ref, gate_ref, wo_ref, fg_ref,
                    o_ref):
    z1 = z1_ref[...]
    yc = x2_ref[...] * (c2_ref[...] + s_ref[...] * z1)
    y = jnp.dot((yc * hyg_ref[...]).astype(BF16), wo_ref[0:HY_WIDTH, :], preferred_element_type=F32)
    y = y + jnp.dot((yd_ref[...] * fng_ref[...]).astype(BF16), wo_ref[HY_WIDTH:, :],
                    preferred_element_type=F32)
    xo = x_ref[...] + gate_ref[...] * y
    o_ref[...] = xo * lax.rsqrt(jnp.mean(xo * xo, axis=-1, keepdims=True) + EPS) * fg_ref[...]


def _odd_out(c2, uc, z1, skip1, hy_gate, yd, fn_gate, x, gate, w_out_bf, final_g):
    n, d = x.shape
    tm = _row_tile(n, 512)
    row = lambda w: pl.BlockSpec((tm, w), lambda i: (i, 0))
    vec = lambda w: pl.BlockSpec((1, w), lambda i: (0, 0))
    return pl.pallas_call(
        _odd_out_kernel,
        out_shape=jax.ShapeDtypeStruct((n, d), F32),
        grid=(n // tm,),
        in_specs=[row(HY_WIDTH), pl.BlockSpec((tm, HY_WIDTH), lambda i: (i, 2)), row(HY_WIDTH), vec(HY_WIDTH),
                  row(HY_WIDTH), row(FN_WIDTH), row(FN_WIDTH), row(d), vec(d), _const_spec((d, d)), vec(d)],
        out_specs=row(d),
        compiler_params=_params(("parallel",), VMEM_LIMIT),
        name="odd_out",
    )(c2, uc, z1, skip1, hy_gate, yd, fn_gate, x, gate, w_out_bf, final_g)


def _rope_tables(n):
    t = jnp.arange(n, dtype=jnp.int32)
    row = (t // GRID_W).astype(F32)
    col = (t % GRID_W).astype(F32)
    inv_freq = ROPE_THETA ** (-jnp.arange(ROPE_FREQS, dtype=F32) / ROPE_FREQS)
    ang_r = row[:, None] * inv_freq
    ang_c = col[:, None] * inv_freq
    ang = jnp.concatenate([ang_r, ang_r, ang_c, ang_c], axis=-1)
    sign = jnp.tile(jnp.concatenate([-jnp.ones((ROPE_FREQS,), F32), jnp.ones((ROPE_FREQS,), F32)]), 2)
    return jnp.cos(ang), jnp.sin(ang) * sign


def _hyena_embedding(n, pos):
    t = (pos.astype(F32) / (n - 1))[:, None]
    w = 2.0 * math.pi * pos.astype(F32)[:, None] / n
    f = jnp.linspace(1e-4, HY_BANDS - 1, HY_BANDS, dtype=F32)[None, :]
    emb = jnp.concatenate([t, jnp.cos(f * w), -jnp.sin(f * w)], axis=-1)
    return jnp.pad(emb, ((0, 0), (0, LANES - HY_EMB)))


def kernel(x, c, ctx, c_ctx, w_mod, b_mod, norm_g, ev_w_in, ev_w_out, pool_w, pool_scale, q_norm_g, k_norm_g,
           od_w_in, od_w_out, hy_conv_w, hy_conv_b, hy_w1, hy_b1, hy_w2, hy_b2, hy_w3, hy_freq, hy_skip, fn_w,
           final_g):
    n, d = x.shape[1], x.shape[2]
    nc = ctx.shape[1]
    x0 = x[0]
    ctx0 = ctx[0]
    vec = lambda a: a.reshape(1, -1)

    cond = jnp.zeros((8, d), F32).at[0].set(c[0]).at[1].set(c_ctx)
    mod = _modulation(cond, w_mod, b_mod)
    shift0, scale0, gate0 = (mod[0, 0:1, k * d:(k + 1) * d] for k in range(3))
    cshift0, cscale0 = mod[0, 1:2, 0:d], mod[0, 1:2, d:2 * d]
    shift1, scale1, gate1 = (mod[1, 0:1, k * d:(k + 1) * d] for k in range(3))

    w_in0 = ev_w_in[0].astype(BF16)
    cos, sin = _rope_tables(n)
    g0 = vec(norm_g[0])
    qg, kg = vec(q_norm_g[0]), vec(k_norm_g[0])
    a_val, a_gate, q, k, v, b_gate = _even_in(x0, g0, scale0, shift0, w_in0, qg, kg, cos, sin)
    ones = jnp.ones((nc, HEAD_DIM), F32)
    _, _, _, ck, cv, _ = _even_in(ctx0, g0, cscale0, cshift0, w_in0, qg, kg, ones, jnp.zeros_like(ones))
    yb = _attention(q, ck, cv, k, v, b_gate)
    pool_bd = jax.scipy.linalg.block_diag(*[pool_w[0, gi] for gi in range(pool_w.shape[1])]).astype(BF16)
    x1 = _even_out(a_val, a_gate, yb, x0, gate0, pool_bd, vec(pool_scale[0]), ev_w_out[0].astype(BF16))

    hy_in, hy_gate, fn_in, fn_gate = _odd_in(x1, vec(norm_g[1]), scale1, shift1, od_w_in[0].astype(BF16))
    uc = _conv3(hy_in, hy_conv_w[0], vec(hy_conv_b[0]))

    max_decay = math.log(HY_DECAY_TARGET) / HY_FAST_DECAY
    min_decay = math.log(HY_DECAY_TARGET) / HY_SLOW_DECAY
    deltas = jnp.linspace(min_decay, max_decay, HY_WIDTH, dtype=F32)[None, :]
    w1p = jnp.pad(hy_w1[0], ((0, LANES - HY_EMB), (0, 0)))
    pos = jnp.arange(n, dtype=jnp.int32)
    emb = _hyena_embedding(n, pos)
    emb_rev = _hyena_embedding(n, jnp.where(pos == 0, 0, n - pos))
    k_lo, k_hi, mass = _hyena_filters(emb, emb_rev, w1p, vec(hy_b1[0]), hy_w2[0], vec(hy_b2[0]),
                                      hy_w3[0].astype(BF16), vec(hy_freq[0]), deltas)

    n2 = n // DFT_N1
    m_full, _, big_fwd, big_inv = _dft_tables(2 * n, n2, True, 2 * DFT_N1)
    m_half, m_inv, _, _ = _dft_tables(2 * n, n2, True, DFT_N1)
    k_spec = _dft_fwd([k_lo, k_hi], HY_ORDER * HY_WIDTH, m_full, big_fwd, mass=mass)
    conv1 = _hyena_conv(uc, 0, k_spec, 0, m_half, m_inv, big_fwd, big_inv)
    z1 = _hy_gate(conv1, uc, HY_WIDTH, 0, hy_skip[0, 0:1])
    conv2 = _hyena_conv(z1, 0, k_spec, 1, m_half, m_inv, big_fwd, big_inv)

    ch = np.arange(FN_WIDTH, dtype=np.float64)
    ph = 2.0 * np.pi * np.outer(ch, ch) / FN_WIDTH
    cs = jnp.asarray(np.concatenate([np.cos(ph), np.sin(ph)], axis=1).astype(np.float32))
    pq = _chan_dft(fn_in, cs)
    f_half, _, f_big, _ = _dft_tables(n, n2, False, DFT_N1)
    pq_spec = _dft_fwd([pq], 2 * FN_WIDTH, f_half, f_big)
    yd = _fn_out(pq_spec, fn_w[0].astype(BF16), n)

    out = _odd_out(conv2, uc, z1, hy_skip[0, 1:2], hy_gate, yd, fn_gate, x1, gate1, od_w_out[0].astype(BF16),
                   vec(final_g))
    return out[None]
```

```python
import functools
import math

import numpy as np
import jax
import jax.numpy as jnp
from jax import lax
from jax.experimental import pallas as pl
from jax.experimental.pallas import tpu as pltpu

F32 = jnp.float32
BF16 = jnp.bfloat16
HIGHEST = lax.Precision.HIGHEST

EPS = 1e-6
GRID_W = 64
HEAD_DIM = 128
ROPE_FREQS = 32
ROPE_THETA = 10000.0
N_Q_HEADS = 6
N_KV_HEADS = 2
Q_PER_KV = N_Q_HEADS // N_KV_HEADS
POOL_WIDTH = 256
POOL_GROUP_DIM = 64
POOL_WINDOWS = (2, 4, 8, 16)
POOL_HALO = 8
ATT_WIDTH = N_Q_HEADS * HEAD_DIM
KV_WIDTH = N_KV_HEADS * HEAD_DIM
HY_WIDTH = 768
HY_ORDER = 2
HY_EMB = 33
HY_BANDS = 16
HY_HIDDEN = 64
FN_WIDTH = 256
HY_DECAY_TARGET = 1e-2
HY_FAST_DECAY = 0.3
HY_SLOW_DECAY = 1.5

LANES = 128
DFT_N1 = 128
VMEM_LIMIT = 56 * 1024 * 1024


def _row_tile(n, pref):
    t = min(pref, n)
    assert n % t == 0
    return t


def _silu(x):
    return x * jax.nn.sigmoid(x)


def _params(sem, vmem=None):
    return pltpu.CompilerParams(dimension_semantics=sem, vmem_limit_bytes=vmem)


def _const_spec(shape):
    nd = len(shape)
    return pl.BlockSpec(shape, lambda *_: (0,) * nd, pipeline_mode=pl.Buffered(1))


def _mod_kernel(cond_ref, w_ref, b_ref, o_ref):
    s = _silu(cond_ref[...])
    o_ref[0] = jnp.dot(s, w_ref[0], precision=HIGHEST, preferred_element_type=F32) + b_ref[0]


def _modulation(cond, w_mod, b_mod):
    depth, d, d3 = w_mod.shape
    tn = 1024
    return pl.pallas_call(
        _mod_kernel,
        out_shape=jax.ShapeDtypeStruct((depth, 8, d3), F32),
        grid=(depth, d3 // tn),
        in_specs=[pl.BlockSpec((8, d), lambda i, j: (0, 0)),
                  pl.BlockSpec((1, d, tn), lambda i, j: (i, 0, j)),
                  pl.BlockSpec((1, 1, tn), lambda i, j: (i, 0, j))],
        out_specs=pl.BlockSpec((1, 8, tn), lambda i, j: (i, 0, j)),
        compiler_params=_params(("arbitrary", "arbitrary")),
        name="modulation",
    )(cond, w_mod, b_mod.reshape(depth, 1, d3))


def _norm_mod(x, g, scale, shift):
    y = x * lax.rsqrt(jnp.mean(x * x, axis=-1, keepdims=True) + EPS)
    return (y * g) * (1.0 + scale) + shift


def _even_in_kernel(x_ref, g_ref, sc_ref, sh_ref, w_ref, qg_ref, kg_ref, cos_ref, sin_ref,
                    aval_ref, agate_ref, q_ref, k_ref, v_ref, bgate_ref):
    h = _norm_mod(x_ref[...], g_ref[...], sc_ref[...], sh_ref[...])
    p = jnp.dot(h.astype(BF16), w_ref[...], preferred_element_type=F32)
    aval_ref[...] = p[:, 0:POOL_WIDTH]
    agate_ref[...] = _silu(p[:, POOL_WIDTH:2 * POOL_WIDTH])
    cos = cos_ref[...]
    sin = sin_ref[...]
    lane = lax.broadcasted_iota(jnp.int32, cos.shape, 1)
    low_half = (lane % (2 * ROPE_FREQS)) < ROPE_FREQS

    def head(xh, g):
        y = xh * lax.rsqrt(jnp.mean(xh * xh, axis=-1, keepdims=True) + EPS) * g
        rot = jnp.where(low_half, pltpu.roll(y, HEAD_DIM - ROPE_FREQS, 1), pltpu.roll(y, ROPE_FREQS, 1))
        return y * cos + rot * sin

    q0 = 2 * POOL_WIDTH
    for hq in range(N_Q_HEADS):
        sl = slice(hq * HEAD_DIM, (hq + 1) * HEAD_DIM)
        q_ref[:, sl] = head(p[:, q0 + hq * HEAD_DIM:q0 + (hq + 1) * HEAD_DIM], qg_ref[...]).astype(BF16)
    k0 = q0 + ATT_WIDTH
    for hk in range(N_KV_HEADS):
        sl = slice(hk * HEAD_DIM, (hk + 1) * HEAD_DIM)
        k_ref[:, sl] = head(p[:, k0 + hk * HEAD_DIM:k0 + (hk + 1) * HEAD_DIM], kg_ref[...]).astype(BF16)
    v0 = k0 + KV_WIDTH
    v_ref[...] = p[:, v0:v0 + KV_WIDTH].astype(BF16)
    b0 = v0 + KV_WIDTH
    bgate_ref[...] = _silu(p[:, b0:b0 + ATT_WIDTH])


def _even_in(x, g, scale, shift, w_bf, qg, kg, cos, sin):
    n, d = x.shape
    n_in = w_bf.shape[1]
    tm = _row_tile(n, 512)
    row = lambda w: pl.BlockSpec((tm, w), lambda i: (i, 0))
    vec = lambda w: pl.BlockSpec((1, w), lambda i: (0, 0))
    return pl.pallas_call(
        _even_in_kernel,
        out_shape=(jax.ShapeDtypeStruct((n, POOL_WIDTH), F32), jax.ShapeDtypeStruct((n, POOL_WIDTH), F32),
                   jax.ShapeDtypeStruct((n, ATT_WIDTH), BF16), jax.ShapeDtypeStruct((n, KV_WIDTH), BF16),
                   jax.ShapeDtypeStruct((n, KV_WIDTH), BF16), jax.ShapeDtypeStruct((n, ATT_WIDTH), F32)),
        grid=(n // tm,),
        in_specs=[row(d), vec(d), vec(d), vec(d), _const_spec((d, n_in)), vec(HEAD_DIM), vec(HEAD_DIM),
                  row(HEAD_DIM), row(HEAD_DIM)],
        out_specs=(row(POOL_WIDTH), row(POOL_WIDTH), row(ATT_WIDTH), row(KV_WIDTH), row(KV_WIDTH),
                   row(ATT_WIDTH)),
        compiler_params=_params(("parallel",), VMEM_LIMIT),
        name="even_in",
    )(x, g, scale, shift, w_bf, qg, kg, cos, sin)


ATT_ROW_CHUNK = 32


def _attn_kernel(q_ref, k_ref, v_ref, bg_ref, o_ref, s_sc, p_sc, m_sc, al_sc, acc_sc, *, tq, tk, n_kv_tiles):
    c = (HEAD_DIM ** -0.5) * math.log2(math.e)
    rows = Q_PER_KV * tq
    for h in range(N_KV_HEADS):
        hs = slice(h * HEAD_DIM, (h + 1) * HEAD_DIM)
        qs = jnp.concatenate(
            [q_ref[:, (Q_PER_KV * h + g) * HEAD_DIM:(Q_PER_KV * h + g + 1) * HEAD_DIM] for g in range(Q_PER_KV)],
            axis=0)

        def scores(buf, kt, width):
            s_sc[buf, :, 0:width] = lax.dot_general(qs, kt, (((1,), (1,)), ((), ())),
                                                    preferred_element_type=F32)

        def softmax_pv(buf, vt, width):
            for r in range(rows // ATT_ROW_CHUNK):
                rs = slice(r * ATT_ROW_CHUNK, (r + 1) * ATT_ROW_CHUNK)
                cols = [s_sc[buf, rs, j * LANES:(j + 1) * LANES] for j in range(width // LANES)]
                mx = functools.reduce(jnp.maximum, cols)
                m_old = m_sc[rs, :]
                m_new = jnp.maximum(m_old, jnp.max(mx, axis=-1, keepdims=True) * c)
                for j, sj in enumerate(cols):
                    p_sc[buf, rs, j * LANES:(j + 1) * LANES] = jnp.exp2(sj * c - m_new).astype(BF16)
                al_sc[rs, :] = jnp.exp2(m_old - m_new)
                m_sc[rs, :] = m_new
            v_ext = jnp.concatenate([vt, jnp.ones((width, HEAD_DIM), BF16)], axis=1)
            pv = jnp.dot(p_sc[buf, :, 0:width], v_ext, preferred_element_type=F32)
            al = al_sc[...]
            for j in range(2):
                js = slice(j * HEAD_DIM, (j + 1) * HEAD_DIM)
                acc_sc[:, js] = al * acc_sc[:, js] + pv[:, js]

        m_sc[...] = jnp.full(m_sc.shape, -1e30, F32)
        acc_sc[...] = jnp.zeros(acc_sc.shape, F32)
        scores(0, k_ref[0:tk, hs], tk)

        def body(i, carry):
            r0 = pl.multiple_of(2 * i * tk, tk)
            r1 = pl.multiple_of(r0 + tk, tk)
            r2 = pl.multiple_of(r0 + 2 * tk, tk)
            scores(1, k_ref[pl.ds(r1, tk), hs], tk)
            softmax_pv(0, v_ref[pl.ds(r0, tk), hs], tk)
            scores(0, k_ref[pl.ds(r2, tk), hs], tk)
            softmax_pv(1, v_ref[pl.ds(r1, tk), hs], tk)
            return carry

        n_pairs = (n_kv_tiles - 1) // 2
        lax.fori_loop(0, n_pairs, body, 0)
        last = (n_kv_tiles - 1) * tk
        if n_kv_tiles % 2 == 1:
            softmax_pv(0, v_ref[last:last + tk, hs], tk)
        else:
            scores(1, k_ref[last:last + tk, hs], tk)
            softmax_pv(0, v_ref[last - tk:last, hs], tk)
            softmax_pv(1, v_ref[last:last + tk, hs], tk)
        out = acc_sc[:, 0:HEAD_DIM] / acc_sc[:, HEAD_DIM:2 * HEAD_DIM]
        for g in range(Q_PER_KV):
            cs = slice((Q_PER_KV * h + g) * HEAD_DIM, (Q_PER_KV * h + g + 1) * HEAD_DIM)
            o_ref[:, cs] = (out[g * tq:(g + 1) * tq] * bg_ref[:, cs]).astype(o_ref.dtype)


ATT_KEY_TILE = 1280


def _attention(q, k_all, v_all, bgate):
    n = q.shape[0]
    nk = k_all.shape[0]
    tq = _row_tile(n, 256)
    tk = max(t for t in range(LANES, ATT_KEY_TILE + 1, LANES) if nk % t == 0)
    rows = Q_PER_KV * tq
    kern = functools.partial(_attn_kernel, tq=tq, tk=tk, n_kv_tiles=nk // tk)
    return pl.pallas_call(
        kern,
        out_shape=jax.ShapeDtypeStruct((n, ATT_WIDTH), F32),
        grid=(n // tq,),
        in_specs=[pl.BlockSpec((tq, ATT_WIDTH), lambda i: (i, 0)),
                  _const_spec((nk, KV_WIDTH)), _const_spec((nk, KV_WIDTH)),
                  pl.BlockSpec((tq, ATT_WIDTH), lambda i: (i, 0))],
        out_specs=pl.BlockSpec((tq, ATT_WIDTH), lambda i: (i, 0)),
        scratch_shapes=[pltpu.VMEM((2, rows, tk), F32), pltpu.VMEM((2, rows, tk), BF16),
                        pltpu.VMEM((rows, LANES), F32), pltpu.VMEM((rows, LANES), F32),
                        pltpu.VMEM((rows, 2 * HEAD_DIM), F32)],
        compiler_params=_params(("parallel",), VMEM_LIMIT),
        name="attention",
    )(q, k_all, v_all, bgate)


def _even_out_kernel(a_ref, ap_ref, an_ref, ag_ref, yb_ref, x_ref, gate_ref, pw_ref, ps_ref, wo_ref, o_ref,
                     *, tm, n_rows):
    i = pl.program_id(0)
    n_tiles = pl.num_programs(0)
    u = a_ref[...]
    prev = jnp.where(i > 0, ap_ref[...], 0.0)
    nxt = jnp.where(i < n_tiles - 1, an_ref[...], 0.0)
    e = jnp.concatenate([prev, u, nxt], axis=0)
    n = tm + 2 * POOL_HALO
    s2 = e[0:n - 1] + e[1:n]
    s4 = s2[0:n - 3] + s2[2:n - 1]
    s8 = s4[0:n - 7] + s4[4:n - 3]
    s16 = s8[0:n - 15] + s8[8:n - 7]
    lane = lax.broadcasted_iota(jnp.int32, (tm, POOL_WIDTH), 1)
    grp = lane // POOL_GROUP_DIM
    win = jnp.where(grp == 0, s2[7:7 + tm],
                    jnp.where(grp == 1, s4[6:6 + tm], jnp.where(grp == 2, s8[4:4 + tm], s16[0:tm])))
    half = jnp.where(grp == 0, 1, jnp.where(grp == 1, 2, jnp.where(grp == 2, 4, 8)))
    t = i * tm + lax.broadcasted_iota(jnp.int32, (tm, POOL_WIDTH), 0)
    cnt = (jnp.minimum(t + half, n_rows) - jnp.maximum(t - half, 0)).astype(F32)
    d = win / cnt - u
    ya = jnp.dot(d.astype(BF16), pw_ref[...], preferred_element_type=F32) * ps_ref[...]
    ya = ya * ag_ref[...]
    y = jnp.dot(ya.astype(BF16), wo_ref[0:POOL_WIDTH, :], preferred_element_type=F32)
    y = y + jnp.dot(yb_ref[...].astype(BF16), wo_ref[POOL_WIDTH:, :], preferred_element_type=F32)
    o_ref[...] = x_ref[...] + gate_ref[...] * y


def _even_out(a_val, a_gate, yb, x, gate, pool_bd, pool_scale, w_out_bf):
    n, d = x.shape
    tm = _row_tile(n, 512)
    hb = tm // POOL_HALO
    last = n // POOL_HALO - 1
    row = lambda w: pl.BlockSpec((tm, w), lambda i: (i, 0))
    vec = lambda w: pl.BlockSpec((1, w), lambda i: (0, 0))
    kern = functools.partial(_even_out_kernel, tm=tm, n_rows=n)
    return pl.pallas_call(
        kern,
        out_shape=jax.ShapeDtypeStruct((n, d), F32),
        grid=(n // tm,),
        in_specs=[row(POOL_WIDTH),
                  pl.BlockSpec((POOL_HALO, POOL_WIDTH), lambda i: (jnp.maximum(i * hb - 1, 0), 0)),
                  pl.BlockSpec((POOL_HALO, POOL_WIDTH), lambda i: (jnp.minimum((i + 1) * hb, last), 0)),
                  row(POOL_WIDTH), row(ATT_WIDTH), row(d), vec(d),
                  _const_spec((POOL_WIDTH, POOL_WIDTH)), vec(POOL_WIDTH), _const_spec((d, d))],
        out_specs=row(d),
        compiler_params=_params(("parallel",), VMEM_LIMIT),
        name="even_out",
    )(a_val, a_val, a_val, a_gate, yb, x, gate, pool_bd, pool_scale, w_out_bf)


def _odd_in_kernel(x_ref, xp_ref, xn_ref, g_ref, sc_ref, sh_ref, w_ref, cw_ref, cb_ref, cs_ref,
                   uc_ref, hyg_ref, pq_ref, fng_ref, *, tm):
    i = pl.program_id(0)
    n_tiles = pl.num_programs(0)
    g, sc, sh = g_ref[...], sc_ref[...], sh_ref[...]
    p = jnp.dot(_norm_mod(x_ref[...], g, sc, sh).astype(BF16), w_ref[...], preferred_element_type=F32)
    c0 = (HY_ORDER + 1) * HY_WIDTH
    halo = _norm_mod(jnp.concatenate([xp_ref[...], xn_ref[...]], axis=0), g, sc, sh)
    ph = jnp.dot(halo.astype(BF16), w_ref[:, 0:c0], preferred_element_type=F32)
    prev = jnp.where(i > 0, ph[7:8, :], 0.0)
    nxt = jnp.where(i < n_tiles - 1, ph[8:9, :], 0.0)
    u = p[:, 0:c0]
    e = jnp.concatenate([prev, u, nxt], axis=0)
    uc_ref[...] = e[0:tm] * cw_ref[0:1, :] + u * cw_ref[1:2, :] + e[2:tm + 2] * cw_ref[2:3, :] + cb_ref[...]
    hyg_ref[...] = _silu(p[:, c0:c0 + HY_WIDTH])
    fn_in = p[:, c0 + HY_WIDTH:c0 + HY_WIDTH + FN_WIDTH]
    pq_ref[...] = _dot_parts(_split_bf16(fn_in), _ref_parts(cs_ref))
    fng_ref[...] = _silu(p[:, c0 + HY_WIDTH + FN_WIDTH:])


def _odd_in(x, g, scale, shift, w_bf, conv_w, conv_b, cs):
    n, d = x.shape
    n_in = w_bf.shape[1]
    tm = _row_tile(n, 512)
    hb = tm // 8
    last = n // 8 - 1
    row = lambda w: pl.BlockSpec((tm, w), lambda i: (i, 0))
    vec = lambda w: pl.BlockSpec((1, w), lambda i: (0, 0))
    c0 = (HY_ORDER + 1) * HY_WIDTH
    kern = functools.partial(_odd_in_kernel, tm=tm)
    return pl.pallas_call(
        kern,
        out_shape=(jax.ShapeDtypeStruct((n, c0), F32), jax.ShapeDtypeStruct((n, HY_WIDTH), F32),
                   jax.ShapeDtypeStruct((n, 2 * FN_WIDTH), F32), jax.ShapeDtypeStruct((n, FN_WIDTH), F32)),
        grid=(n // tm,),
        in_specs=[row(d),
                  pl.BlockSpec((8, d), lambda i: (jnp.maximum(i * hb - 1, 0), 0)),
                  pl.BlockSpec((8, d), lambda i: (jnp.minimum((i + 1) * hb, last), 0)),
                  vec(d), vec(d), vec(d), _const_spec((d, n_in)),
                  pl.BlockSpec((3, c0), lambda i: (0, 0)), vec(c0), _const_spec(cs.shape)],
        out_specs=(row(c0), row(HY_WIDTH), row(2 * FN_WIDTH), row(FN_WIDTH)),
        compiler_params=_params(("parallel",), VMEM_LIMIT),
        name="odd_in",
    )(x, x, x, g, scale, shift, w_bf, conv_w, conv_b, cs)


def _filter_kernel(emb_ref, w1_ref, b1_ref, w2_ref, b2_ref, w3f_ref, w3b_ref, fr_ref, dl_ref,
                   lo_ref, hi_ref, mass_ref, *, tm, n_rows):
    i = pl.program_id(0)
    fr = fr_ref[...]
    a = jnp.sin(fr * (jnp.dot(emb_ref[...], w1_ref[...], precision=HIGHEST, preferred_element_type=F32)
                      + b1_ref[...]))
    a = jnp.sin(fr * (jnp.dot(a, w2_ref[...], precision=HIGHEST, preferred_element_type=F32) + b2_ref[...]))
    a_f = a_b = a.astype(BF16)
    j = i * tm + lax.broadcasted_iota(jnp.int32, (tm, 1), 0)
    pos_b = jnp.where(j == 0, 0, n_rows - j)
    dl = jnp.abs(dl_ref[...])
    inv = 1.0 / (n_rows - 1)
    decay_f = jnp.exp(-(j.astype(F32) * inv) * dl)
    decay_b = jnp.exp(-(pos_b.astype(F32) * inv) * dl)

    @pl.when(i == 0)
    def _():
        mass_ref[...] = jnp.zeros_like(mass_ref)

    for o in range(HY_ORDER):
        os_ = slice(o * HY_WIDTH, (o + 1) * HY_WIDTH)
        hf = jnp.dot(a_f, w3f_ref[:, os_], preferred_element_type=F32) * decay_f
        hb = jnp.dot(a_b, w3b_ref[:, os_], preferred_element_type=F32) * decay_b
        lo_ref[:, os_] = hf
        hi_ref[:, os_] = jnp.where(j == 0, 0.0, -hb)
        mass_ref[:, os_] += (jnp.sum(jnp.abs(hf), axis=0, keepdims=True)
                             + jnp.sum(jnp.abs(hb), axis=0, keepdims=True))


def _hyena_filters(emb2, w1, b1, w2, b2, w3, freq, deltas):
    n = emb2.shape[0]
    tm = _row_tile(n, 512)
    nh = HY_ORDER * HY_WIDTH
    hh = HY_HIDDEN
    w1b = jnp.zeros((LANES, 2 * hh), F32).at[0:HY_EMB, 0:hh].set(w1).at[hh:hh + HY_EMB, hh:].set(w1)
    w2b = jnp.zeros((2 * hh, 2 * hh), F32).at[0:hh, 0:hh].set(w2).at[hh:, hh:].set(w2)
    w3r = w3.reshape(hh, HY_ORDER, 2, HY_WIDTH)
    zero = jnp.zeros((hh, nh), F32)
    w3f = jnp.concatenate([w3r[:, :, 0, :].reshape(hh, nh), zero], axis=0).astype(BF16)
    w3b = jnp.concatenate([zero, w3r[:, :, 1, :].reshape(hh, nh)], axis=0).astype(BF16)
    twice = lambda a: jnp.concatenate([a, a], axis=-1)
    ops = (emb2, w1b, twice(b1), w2b, twice(b2), w3f, w3b, twice(freq), deltas)
    kern = functools.partial(_filter_kernel, tm=tm, n_rows=n)
    full = lambda a: pl.BlockSpec(a.shape, lambda i: (0,) * a.ndim)
    row = lambda w: pl.BlockSpec((tm, w), lambda i: (i, 0))
    return pl.pallas_call(
        kern,
        out_shape=(jax.ShapeDtypeStruct((n, nh), F32), jax.ShapeDtypeStruct((n, nh), F32),
                   jax.ShapeDtypeStruct((1, nh), F32)),
        grid=(n // tm,),
        in_specs=[row(LANES)] + [full(a) for a in ops[1:]],
        out_specs=(row(nh), row(nh), pl.BlockSpec((1, nh), lambda i: (0, 0))),
        compiler_params=_params(("arbitrary",), VMEM_LIMIT),
        name="hyena_filters",
    )(*ops)


DFT_PARTS = 1


def _table_parts(a):
    a32 = jnp.asarray(a.astype(np.float32))
    hi = a32.astype(BF16)
    if DFT_PARTS == 1:
        return hi[None]
    lo = (a32 - hi.astype(F32)).astype(BF16)
    return jnp.stack([hi, lo])


def _split_bf16(x):
    hi = x.astype(BF16)
    if DFT_PARTS == 1:
        return (hi,)
    return (hi, (x - hi.astype(F32)).astype(BF16))


def _ref_parts(ref, *idx):
    return tuple(ref[(q,) + idx] for q in range(ref.shape[0]))


def _dot_parts(lhs, rhs):
    acc = jnp.dot(lhs[0], rhs[0], preferred_element_type=F32)
    if len(lhs) > 1:
        acc = acc + jnp.dot(lhs[1], rhs[0], preferred_element_type=F32)
    if len(rhs) > 1:
        acc = acc + jnp.dot(lhs[0], rhs[1], preferred_element_type=F32)
    return acc


def _dft_tables(n_total, n2, half_shift, n1_used):
    n1_full = n_total // n2
    sh = 0.5 if half_shift else 0.0
    k1 = np.arange(DFT_N1, dtype=np.float64)[None, :, None] + sh
    n1 = np.arange(n1_used, dtype=np.float64)[None, None, :]
    nn2 = np.arange(n2, dtype=np.float64)[:, None, None]
    ph = -2.0 * np.pi * (n1 * k1 / n1_full + nn2 * k1 / n_total)
    fwd = np.stack([np.cos(ph), np.sin(ph)], axis=2).reshape(n2, 2 * DFT_N1, n1_used)
    ph_i = ph[:, :, :DFT_N1]
    inv = np.stack([np.cos(ph_i), np.sin(ph_i)], axis=2).reshape(n2, 2 * DFT_N1, DFT_N1)
    inv = (2.0 / n_total) * inv.transpose(0, 2, 1)
    k2 = np.arange(n2, dtype=np.float64)
    ph2 = -2.0 * np.pi * np.outer(k2, k2) / n2
    fr, fi = np.cos(ph2), np.sin(ph2)
    big_fwd = np.block([[fr, -fi], [fi, fr]])
    big_inv = np.block([[fr, fi], [-fi, fr]])
    return _table_parts(fwd), _table_parts(inv), _table_parts(big_fwd), _table_parts(big_inv)


def _scratch_pitch(n2):
    return n2 + 8


DFT_STEP = 16


def _stage1_fwd(x_refs, m_ref, a_sc, step, *, n2, pitch):
    for u in range(m_ref.shape[1]):
        nn = step * m_ref.shape[1] + u
        xs = [xr[pl.ds(nn, DFT_N1, stride=n2), :] for xr in x_refs]
        xs = xs[0] if len(xs) == 1 else jnp.concatenate(xs, axis=0)
        a_sc[pl.ds(nn, 2 * DFT_N1, stride=pitch), :] = _dot_parts(_ref_parts(m_ref, u), _split_bf16(xs))


def _load_cplx(a_sc, k1, *, n2, pitch):
    r0 = pl.multiple_of(k1 * 2 * pitch, 8)
    return jnp.concatenate([a_sc[pl.ds(r0, n2), :], a_sc[pl.ds(r0 + pitch, n2), :]], axis=0)


def _phase_steps(n2):
    n2g = min(DFT_STEP, n2)
    return n2g, n2 // n2g, DFT_N1 // DFT_STEP


def _dft_fwd_kernel(*refs, n_x, n2, pitch, ja, scaled):
    x_refs = refs[:n_x]
    rest = refs[n_x:]
    if scaled:
        mass_ref, rest = rest[0], rest[1:]
    m_ref, big_ref, o_ref, a_sc = rest
    j = pl.program_id(1)

    @pl.when(j < ja)
    def _():
        _stage1_fwd(x_refs, m_ref, a_sc, j, n2=n2, pitch=pitch)

    @pl.when(j >= ja)
    def _():
        big = _ref_parts(big_ref)
        for g in range(DFT_STEP):
            blk = _load_cplx(a_sc, (j - ja) * DFT_STEP + g, n2=n2, pitch=pitch)
            out = _dot_parts(big, _split_bf16(blk))
            if scaled:
                out = out * (1.0 / (mass_ref[...] + EPS))
            o_ref[g * 2 * n2:(g + 1) * 2 * n2, :] = out


def _dft_fwd(xs, width, m_fwd, big, mass=None):
    n = xs[0].shape[0]
    n2 = n // DFT_N1
    pitch = _scratch_pitch(n2)
    n2g, ja, jb = _phase_steps(n2)
    wt = LANES
    rows = 2 * DFT_N1 * n2
    kern = functools.partial(_dft_fwd_kernel, n_x=len(xs), n2=n2, pitch=pitch, ja=ja, scaled=mass is not None)
    x_spec = pl.BlockSpec((n, wt), lambda i, j: (0, i), pipeline_mode=pl.Buffered(1))
    ops = list(xs)
    specs = [x_spec] * len(xs)
    if mass is not None:
        ops.append(mass)
        specs.append(pl.BlockSpec((1, wt), lambda i, j: (0, i)))
    m_spec = pl.BlockSpec((m_fwd.shape[0], n2g) + m_fwd.shape[2:],
                          lambda i, j: (0, jnp.minimum(j, ja - 1), 0, 0))
    return pl.pallas_call(
        kern,
        out_shape=jax.ShapeDtypeStruct((rows, width), F32),
        grid=(width // wt, ja + jb),
        in_specs=specs + [m_spec, _const_spec(big.shape)],
        out_specs=pl.BlockSpec((DFT_STEP * 2 * n2, wt), lambda i, j: (jnp.maximum(j - ja, 0), i)),
        scratch_shapes=[pltpu.VMEM((2 * DFT_N1 * pitch, wt), F32)],
        compiler_params=_params(("parallel", "arbitrary"), VMEM_LIMIT),
        name="dft_fwd",
    )(*ops, m_fwd, big)


def _hyena_conv_kernel(x_ref, k_ref, m_ref, minv_ref, bigf_ref, bigi_ref, y_ref, a_sc, *, n2, pitch, ja, jb):
    j = pl.program_id(1)

    @pl.when(j < ja)
    def _():
        _stage1_fwd([x_ref], m_ref, a_sc, j, n2=n2, pitch=pitch)

    @pl.when((j >= ja) & (j < ja + jb))
    def _():
        bigf = _ref_parts(bigf_ref)
        bigi = _ref_parts(bigi_ref)
        for g in range(DFT_STEP):
            k1 = (j - ja) * DFT_STEP + g
            z = _dot_parts(bigf, _split_bf16(_load_cplx(a_sc, k1, n2=n2, pitch=pitch)))
            zr, zi = z[0:n2], z[n2:2 * n2]
            kr = k_ref[g * 2 * n2:g * 2 * n2 + n2, :]
            ki = k_ref[g * 2 * n2 + n2:(g + 1) * 2 * n2, :]
            prod = jnp.concatenate([zr * kr - zi * ki, zr * ki + zi * kr], axis=0)
            b = _dot_parts(bigi, _split_bf16(prod))
            r0 = pl.multiple_of(k1 * 2 * pitch, 8)
            a_sc[pl.ds(r0, n2), :] = b[0:n2]
            a_sc[pl.ds(r0 + pitch, n2), :] = b[n2:2 * n2]

    @pl.when(j >= ja + jb)
    def _():
        for u in range(minv_ref.shape[1]):
            nn = (j - ja - jb) * minv_ref.shape[1] + u
            bs = a_sc[pl.ds(nn, 2 * DFT_N1, stride=pitch), :]
            y_ref[pl.ds(nn, DFT_N1, stride=n2), :] = _dot_parts(_ref_parts(minv_ref, u), _split_bf16(bs))


def _hyena_conv(x, col0, kspec, order, m_fwd, m_inv, big_fwd, big_inv):
    n = x.shape[0]
    n2 = n // DFT_N1
    pitch = _scratch_pitch(n2)
    n2g, ja, jb = _phase_steps(n2)
    wt = LANES
    c0 = col0 // wt
    ck = order * HY_WIDTH // wt
    kern = functools.partial(_hyena_conv_kernel, n2=n2, pitch=pitch, ja=ja, jb=jb)
    return pl.pallas_call(
        kern,
        out_shape=jax.ShapeDtypeStruct((n, HY_WIDTH), F32),
        grid=(HY_WIDTH // wt, ja + jb + ja),
        in_specs=[pl.BlockSpec((n, wt), lambda i, j: (0, c0 + i), pipeline_mode=pl.Buffered(1)),
                  pl.BlockSpec((DFT_STEP * 2 * n2, wt), lambda i, j: (jnp.clip(j - ja, 0, jb - 1), ck + i)),
                  pl.BlockSpec((m_fwd.shape[0], n2g) + m_fwd.shape[2:],
                               lambda i, j: (0, jnp.minimum(j, ja - 1), 0, 0)),
                  pl.BlockSpec((m_inv.shape[0], n2g) + m_inv.shape[2:],
                               lambda i, j: (0, jnp.clip(j - ja - jb, 0, ja - 1), 0, 0)),
                  _const_spec(big_fwd.shape), _const_spec(big_inv.shape)],
        out_specs=pl.BlockSpec((n, wt), lambda i, j: (0, i)),
        scratch_shapes=[pltpu.VMEM((2 * DFT_N1 * pitch, wt), F32)],
        compiler_params=_params(("parallel", "arbitrary"), VMEM_LIMIT),
        name="hyena_conv",
    )(x, kspec, m_fwd, m_inv, big_fwd, big_inv)


def _hy_gate_kernel(c_ref, g_ref, z_ref, s_ref, o_ref):
    z = z_ref[...]
    o_ref[...] = g_ref[...] * (c_ref[...] + s_ref[...] * z)


def _hy_gate(conv, uc, gate_col, z_col, skip):
    n, w = conv.shape
    tm = _row_tile(n, 512)
    gc = gate_col // w
    zc = z_col // w
    return pl.pallas_call(
        _hy_gate_kernel,
        out_shape=jax.ShapeDtypeStruct((n, w), F32),
        grid=(n // tm,),
        in_specs=[pl.BlockSpec((tm, w), lambda i: (i, 0)), pl.BlockSpec((tm, w), lambda i: (i, gc)),
                  pl.BlockSpec((tm, w), lambda i: (i, zc)), pl.BlockSpec((1, w), lambda i: (0, 0))],
        out_specs=pl.BlockSpec((tm, w), lambda i: (i, 0)),
        compiler_params=_params(("parallel",), VMEM_LIMIT),
        name="hy_gate",
    )(conv, uc, uc, skip)


def _fn_out_kernel(p_ref, q_ref, w_ref, o_ref, *, kb, norm):
    w = w_ref[...]
    for jj in range(kb):
        r = (p_ref[:, 0, jj, :] + q_ref[:, 0, jj, :]) * norm
        o_ref[jj] = jnp.dot(r.astype(BF16), w, preferred_element_type=F32)


def _fn_out(spec, fn_w_bf, n):
    n2 = n // DFT_N1
    c = FN_WIDTH
    kb = 8
    s4 = spec.reshape(DFT_N1, 2, n2, 2 * c)
    kern = functools.partial(_fn_out_kernel, kb=kb, norm=1.0 / math.sqrt(n * c))
    out = pl.pallas_call(
        kern,
        out_shape=jax.ShapeDtypeStruct((n2, DFT_N1, c), F32),
        grid=(n2 // kb,),
        in_specs=[pl.BlockSpec((DFT_N1, 1, kb, c), lambda i: (0, 0, i, 0)),
                  pl.BlockSpec((DFT_N1, 1, kb, c), lambda i: (0, 1, i, 1)),
                  _const_spec((c, c))],
        out_specs=pl.BlockSpec((kb, DFT_N1, c), lambda i: (i, 0, 0)),
        compiler_params=_params(("parallel",), VMEM_LIMIT),
        name="fn_out",
    )(s4, s4, fn_w_bf)
    return out.reshape(n, c)


def _odd_out_kernel(c2_ref, x2_ref, z1_ref, s_ref, hyg_ref, yd_ref, fng_ref, x_ref, gate_ref, wo_ref, fg_ref,
                    o_ref):
    z1 = z1_ref[...]
    yc = x2_ref[...] * (c2_ref[...] + s_ref[...] * z1)
    y = jnp.dot((yc * hyg_ref[...]).astype(BF16), wo_ref[0:HY_WIDTH, :], preferred_element_type=F32)
    y = y + jnp.dot((yd_ref[...] * fng_ref[...]).astype(BF16), wo_ref[HY_WIDTH:, :],
                    preferred_element_type=F32)
    xo = x_ref[...] + gate_ref[...] * y
    o_ref[...] = xo * lax.rsqrt(jnp.mean(xo * xo, axis=-1, keepdims=True) + EPS) * fg_ref[...]


def _odd_out(c2, uc, z1, skip1, hy_gate, yd, fn_gate, x, gate, w_out_bf, final_g):
    n, d = x.shape
    tm = _row_tile(n, 512)
    row = lambda w: pl.BlockSpec((tm, w), lambda i: (i, 0))
    vec = lambda w: pl.BlockSpec((1, w), lambda i: (0, 0))
    return pl.pallas_call(
        _odd_out_kernel,
        out_shape=jax.ShapeDtypeStruct((n, d), F32),
        grid=(n // tm,),
        in_specs=[row(HY_WIDTH), pl.BlockSpec((tm, HY_WIDTH), lambda i: (i, 2)), row(HY_WIDTH), vec(HY_WIDTH),
                  row(HY_WIDTH), row(FN_WIDTH), row(FN_WIDTH), row(d), vec(d), _const_spec((d, d)), vec(d)],
        out_specs=row(d),
        compiler_params=_params(("parallel",), VMEM_LIMIT),
        name="odd_out",
    )(c2, uc, z1, skip1, hy_gate, yd, fn_gate, x, gate, w_out_bf, final_g)


def _rope_tables(n):
    rows = n // GRID_W
    inv_freq = ROPE_THETA ** (-jnp.arange(ROPE_FREQS, dtype=F32) / ROPE_FREQS)
    ang_r = jnp.arange(rows, dtype=F32)[:, None] * inv_freq
    ang_c = jnp.arange(GRID_W, dtype=F32)[:, None] * inv_freq

    def table(fn, sign):
        r = jnp.broadcast_to(fn(ang_r)[:, None, :], (rows, GRID_W, ROPE_FREQS))
        c = jnp.broadcast_to(fn(ang_c)[None, :, :], (rows, GRID_W, ROPE_FREQS))
        return jnp.concatenate([sign * r, r, sign * c, c], axis=-1).reshape(n, HEAD_DIM)

    return table(jnp.cos, 1.0), table(jnp.sin, -1.0)


def _hyena_embedding(n, pos):
    t = (pos.astype(F32) / (n - 1))[:, None]
    w = 2.0 * math.pi * pos.astype(F32)[:, None] / n
    f = jnp.linspace(1e-4, HY_BANDS - 1, HY_BANDS, dtype=F32)[None, :]
    emb = jnp.concatenate([t, jnp.cos(f * w), -jnp.sin(f * w)], axis=-1)
    return jnp.pad(emb, ((0, 0), (0, HY_HIDDEN - HY_EMB)))


def kernel(x, c, ctx, c_ctx, w_mod, b_mod, norm_g, ev_w_in, ev_w_out, pool_w, pool_scale, q_norm_g, k_norm_g,
           od_w_in, od_w_out, hy_conv_w, hy_conv_b, hy_w1, hy_b1, hy_w2, hy_b2, hy_w3, hy_freq, hy_skip, fn_w,
           final_g):
    n, d = x.shape[1], x.shape[2]
    nc = ctx.shape[1]
    x0 = x[0]
    ctx0 = ctx[0]
    vec = lambda a: a.reshape(1, -1)

    cond = jnp.zeros((8, d), F32).at[0].set(c[0]).at[1].set(c_ctx)
    mod = _modulation(cond, w_mod, b_mod)
    shift0, scale0, gate0 = (mod[0, 0:1, k * d:(k + 1) * d] for k in range(3))
    cshift0, cscale0 = mod[0, 1:2, 0:d], mod[0, 1:2, d:2 * d]
    shift1, scale1, gate1 = (mod[1, 0:1, k * d:(k + 1) * d] for k in range(3))

    w_in0 = ev_w_in[0].astype(BF16)
    cos, sin = _rope_tables(n)
    g0 = vec(norm_g[0])
    qg, kg = vec(q_norm_g[0]), vec(k_norm_g[0])
    a_val, a_gate, q, k, v, b_gate = _even_in(x0, g0, scale0, shift0, w_in0, qg, kg, cos, sin)
    ones = jnp.ones((nc, HEAD_DIM), F32)
    _, _, _, ck, cv, _ = _even_in(ctx0, g0, cscale0, cshift0, w_in0, qg, kg, ones, jnp.zeros_like(ones))
    yb = _attention(q, jnp.concatenate([ck, k], axis=0), jnp.concatenate([cv, v], axis=0), b_gate)
    pool_bd = jax.scipy.linalg.block_diag(*[pool_w[0, gi] for gi in range(pool_w.shape[1])]).astype(BF16)
    x1 = _even_out(a_val, a_gate, yb, x0, gate0, pool_bd, vec(pool_scale[0]), ev_w_out[0].astype(BF16))

    ch = np.arange(FN_WIDTH, dtype=np.float64)
    ph = 2.0 * np.pi * np.outer(ch, ch) / FN_WIDTH
    cs = _table_parts(np.concatenate([np.cos(ph), np.sin(ph)], axis=1))
    uc, hy_gate, pq, fn_gate = _odd_in(x1, vec(norm_g[1]), scale1, shift1, od_w_in[0].astype(BF16),
                                       hy_conv_w[0], vec(hy_conv_b[0]), cs)

    max_decay = math.log(HY_DECAY_TARGET) / HY_FAST_DECAY
    min_decay = math.log(HY_DECAY_TARGET) / HY_SLOW_DECAY
    deltas = jnp.linspace(min_decay, max_decay, HY_WIDTH, dtype=F32)[None, :]
    pos = jnp.arange(n, dtype=jnp.int32)
    emb2 = jnp.concatenate([_hyena_embedding(n, pos),
                            _hyena_embedding(n, jnp.where(pos == 0, 0, n - pos))], axis=-1)
    k_lo, k_hi, mass = _hyena_filters(emb2, hy_w1[0], vec(hy_b1[0]), hy_w2[0], vec(hy_b2[0]), hy_w3[0],
                                      vec(hy_freq[0]), deltas)

    n2 = n // DFT_N1
    m_full, _, big_fwd, big_inv = _dft_tables(2 * n, n2, True, 2 * DFT_N1)
    m_half, m_inv, _, _ = _dft_tables(2 * n, n2, True, DFT_N1)
    k_spec = _dft_fwd([k_lo, k_hi], HY_ORDER * HY_WIDTH, m_full, big_fwd, mass=mass)
    conv1 = _hyena_conv(uc, 0, k_spec, 0, m_half, m_inv, big_fwd, big_inv)
    z1 = _hy_gate(conv1, uc, HY_WIDTH, 0, hy_skip[0, 0:1])
    conv2 = _hyena_conv(z1, 0, k_spec, 1, m_half, m_inv, big_fwd, big_inv)

    f_half, _, f_big, _ = _dft_tables(n, n2, False, DFT_N1)
    pq_spec = _dft_fwd([pq], 2 * FN_WIDTH, f_half, f_big)
    yd = _fn_out(pq_spec, fn_w[0].astype(BF16), n)

    out = _odd_out(conv2, uc, z1, hy_skip[0, 1:2], hy_gate, yd, fn_gate, x1, gate1, od_w_out[0].astype(BF16),
                   vec(final_g))
    return out[None]
```

```python
import functools
import math

import numpy as np
import jax
import jax.numpy as jnp
from jax import lax
from jax.experimental import pallas as pl
from jax.experimental.pallas import tpu as pltpu

F32 = jnp.float32
BF16 = jnp.bfloat16
HIGHEST = lax.Precision.HIGHEST

EPS = 1e-6
GRID_W = 64
HEAD_DIM = 128
ROPE_FREQS = 32
ROPE_THETA = 10000.0
N_Q_HEADS = 6
N_KV_HEADS = 2
Q_PER_KV = N_Q_HEADS // N_KV_HEADS
POOL_WIDTH = 256
POOL_GROUP_DIM = 64
POOL_WINDOWS = (2, 4, 8, 16)
POOL_HALO = 8
ATT_WIDTH = N_Q_HEADS * HEAD_DIM
KV_WIDTH = N_KV_HEADS * HEAD_DIM
HY_WIDTH = 768
HY_ORDER = 2
HY_EMB = 33
HY_BANDS = 16
HY_HIDDEN = 64
FN_WIDTH = 256
HY_DECAY_TARGET = 1e-2
HY_FAST_DECAY = 0.3
HY_SLOW_DECAY = 1.5

LANES = 128
DFT_N1 = 128
VMEM_LIMIT = 56 * 1024 * 1024


def _row_tile(n, pref):
    t = min(pref, n)
    assert n % t == 0
    return t


def _silu(x):
    return x * jax.nn.sigmoid(x)


def _params(sem, vmem=None, flags=None):
    return pltpu.CompilerParams(dimension_semantics=sem, vmem_limit_bytes=vmem, flags=flags)


def _const_spec(shape):
    nd = len(shape)
    return pl.BlockSpec(shape, lambda *_: (0,) * nd, pipeline_mode=pl.Buffered(1))


def _mod_kernel(cond_ref, w_ref, b_ref, o_ref):
    s = _silu(cond_ref[...])
    o_ref[0] = jnp.dot(s, w_ref[0], precision=HIGHEST, preferred_element_type=F32) + b_ref[0]


def _modulation(cond, w_mod, b_mod):
    depth, d, d3 = w_mod.shape
    tn = 1024
    return pl.pallas_call(
        _mod_kernel,
        out_shape=jax.ShapeDtypeStruct((depth, 8, d3), F32),
        grid=(depth, d3 // tn),
        in_specs=[pl.BlockSpec((8, d), lambda i, j: (0, 0)),
                  pl.BlockSpec((1, d, tn), lambda i, j: (i, 0, j)),
                  pl.BlockSpec((1, 1, tn), lambda i, j: (i, 0, j))],
        out_specs=pl.BlockSpec((1, 8, tn), lambda i, j: (i, 0, j)),
        compiler_params=_params(("arbitrary", "arbitrary")),
        name="modulation",
    )(cond, w_mod, b_mod.reshape(depth, 1, d3))


def _time_pitch(n):
    return n // DFT_N1 + 8


def _store_pitched(ref, cols, val, run, pitch):
    for b in range(val.shape[0] // run):
        ref[b * pitch:b * pitch + run, cols] = val[b * run:(b + 1) * run]
        ref[b * pitch + run:(b + 1) * pitch, cols] = jnp.zeros((pitch - run, val.shape[1]), val.dtype)


def _load_pitched(ref, run, pitch):
    return jnp.concatenate([ref[b * pitch:b * pitch + run, :] for b in range(ref.shape[0] // pitch)], axis=0)


def _norm_mod(x, g, scale, shift):
    y = x * lax.rsqrt(jnp.mean(x * x, axis=-1, keepdims=True) + EPS)
    return (y * g) * (1.0 + scale) + shift


def _even_in_kernel(x_ref, g_ref, sc_ref, sh_ref, w_ref, qg_ref, kg_ref, cos_ref, sin_ref,
                    aval_ref, agate_ref, q_ref, k_ref, v_ref, bgate_ref):
    h = _norm_mod(x_ref[...], g_ref[...], sc_ref[...], sh_ref[...])
    p = jnp.dot(h.astype(BF16), w_ref[...], preferred_element_type=F32)
    aval_ref[...] = p[:, 0:POOL_WIDTH]
    agate_ref[...] = _silu(p[:, POOL_WIDTH:2 * POOL_WIDTH])
    cos = cos_ref[...]
    sin = sin_ref[...]
    lane = lax.broadcasted_iota(jnp.int32, cos.shape, 1)
    low_half = (lane % (2 * ROPE_FREQS)) < ROPE_FREQS

    def head(xh, g):
        y = xh * lax.rsqrt(jnp.mean(xh * xh, axis=-1, keepdims=True) + EPS) * g
        rot = jnp.where(low_half, pltpu.roll(y, HEAD_DIM - ROPE_FREQS, 1), pltpu.roll(y, ROPE_FREQS, 1))
        return y * cos + rot * sin

    q0 = 2 * POOL_WIDTH
    for hq in range(N_Q_HEADS):
        sl = slice(hq * HEAD_DIM, (hq + 1) * HEAD_DIM)
        q_ref[:, sl] = head(p[:, q0 + hq * HEAD_DIM:q0 + (hq + 1) * HEAD_DIM], qg_ref[...]).astype(BF16)
    k0 = q0 + ATT_WIDTH
    for hk in range(N_KV_HEADS):
        sl = slice(hk * HEAD_DIM, (hk + 1) * HEAD_DIM)
        k_ref[:, sl] = head(p[:, k0 + hk * HEAD_DIM:k0 + (hk + 1) * HEAD_DIM], kg_ref[...]).astype(BF16)
    v0 = k0 + KV_WIDTH
    v_ref[...] = p[:, v0:v0 + KV_WIDTH].astype(BF16)
    b0 = v0 + KV_WIDTH
    bgate_ref[...] = _silu(p[:, b0:b0 + ATT_WIDTH])


def _even_in(x, g, scale, shift, w_bf, qg, kg, cos, sin):
    n, d = x.shape
    n_in = w_bf.shape[1]
    tm = _row_tile(n, 512)
    row = lambda w: pl.BlockSpec((tm, w), lambda i: (i, 0))
    vec = lambda w: pl.BlockSpec((1, w), lambda i: (0, 0))
    return pl.pallas_call(
        _even_in_kernel,
        out_shape=(jax.ShapeDtypeStruct((n, POOL_WIDTH), F32), jax.ShapeDtypeStruct((n, POOL_WIDTH), F32),
                   jax.ShapeDtypeStruct((n, ATT_WIDTH), BF16), jax.ShapeDtypeStruct((n, KV_WIDTH), BF16),
                   jax.ShapeDtypeStruct((n, KV_WIDTH), BF16), jax.ShapeDtypeStruct((n, ATT_WIDTH), F32)),
        grid=(n // tm,),
        in_specs=[row(d), vec(d), vec(d), vec(d), _const_spec((d, n_in)), vec(HEAD_DIM), vec(HEAD_DIM),
                  row(HEAD_DIM), row(HEAD_DIM)],
        out_specs=(row(POOL_WIDTH), row(POOL_WIDTH), row(ATT_WIDTH), row(KV_WIDTH), row(KV_WIDTH),
                   row(ATT_WIDTH)),
        compiler_params=_params(("parallel",), VMEM_LIMIT),
        name="even_in",
    )(x, g, scale, shift, w_bf, qg, kg, cos, sin)


ATT_UNROLL = 4
ATT_ROW_CHUNK = 16


def _attn_kernel(q_ref, k_ref, v_ref, bg_ref, o_ref, s_sc, p_sc, m_sc, al_sc, acc_sc, *, tq, tk, n_kv_tiles):
    c = (HEAD_DIM ** -0.5) * math.log2(math.e)
    rows = Q_PER_KV * tq
    for h in range(N_KV_HEADS):
        hs = slice(h * HEAD_DIM, (h + 1) * HEAD_DIM)
        qs = jnp.concatenate(
            [q_ref[:, (Q_PER_KV * h + g) * HEAD_DIM:(Q_PER_KV * h + g + 1) * HEAD_DIM] for g in range(Q_PER_KV)],
            axis=0)

        def scores(buf, t):
            kt = k_ref[pl.ds(t * tk if isinstance(t, int) else pl.multiple_of(t * tk, tk), tk), hs]
            s_sc[buf] = lax.dot_general(qs, kt, (((1,), (1,)), ((), ())), preferred_element_type=F32)

        def softmax(buf):
            for r in range(rows // ATT_ROW_CHUNK):
                rs = slice(r * ATT_ROW_CHUNK, (r + 1) * ATT_ROW_CHUNK)
                cols = [s_sc[buf, rs, j * LANES:(j + 1) * LANES] for j in range(tk // LANES)]
                mx = functools.reduce(jnp.maximum, cols)
                m_old = m_sc[rs, :]
                m_new = jnp.maximum(m_old, jnp.max(mx, axis=-1, keepdims=True) * c)
                for j, sj in enumerate(cols):
                    p_sc[buf, rs, j * LANES:(j + 1) * LANES] = jnp.exp2(sj * c - m_new).astype(BF16)
                al_sc[buf, rs, :] = jnp.exp2(m_old - m_new)
                m_sc[rs, :] = m_new

        def accumulate(buf, t):
            vt = v_ref[pl.ds(t * tk if isinstance(t, int) else pl.multiple_of(t * tk, tk), tk), hs]
            v_ext = jnp.concatenate([vt, jnp.ones((tk, HEAD_DIM), BF16)], axis=1)
            pv = jnp.dot(p_sc[buf], v_ext, preferred_element_type=F32)
            al = al_sc[buf]
            for j in range(2):
                js = slice(j * HEAD_DIM, (j + 1) * HEAD_DIM)
                acc_sc[:, js] = al * acc_sc[:, js] + pv[:, js]

        def slot(t, buf, with_scores=True):
            if with_scores:
                scores(1 - buf, t + 1)
            softmax(buf)
            accumulate(buf, t)

        m_sc[...] = jnp.full(m_sc.shape, -1e30, F32)
        acc_sc[...] = jnp.zeros(acc_sc.shape, F32)
        scores(0, 0)

        def body(i, carry):
            for u in range(ATT_UNROLL):
                slot(ATT_UNROLL * i + u, u % 2)
            return carry

        n_full = (n_kv_tiles - 1) // ATT_UNROLL
        lax.fori_loop(0, n_full, body, 0)
        for t in range(ATT_UNROLL * n_full, n_kv_tiles):
            slot(t, t % 2, with_scores=t + 1 < n_kv_tiles)
        out = acc_sc[:, 0:HEAD_DIM] / acc_sc[:, HEAD_DIM:2 * HEAD_DIM]
        for g in range(Q_PER_KV):
            cs = slice((Q_PER_KV * h + g) * HEAD_DIM, (Q_PER_KV * h + g + 1) * HEAD_DIM)
            o_ref[:, cs] = (out[g * tq:(g + 1) * tq] * bg_ref[:, cs]).astype(o_ref.dtype)


ATT_KEY_TILE = 1280


def _attention(q, k_all, v_all, bgate):
    n = q.shape[0]
    nk = k_all.shape[0]
    tq = _row_tile(n, 256)
    tk = max(t for t in range(LANES, ATT_KEY_TILE + 1, LANES) if nk % t == 0)
    rows = Q_PER_KV * tq
    kern = functools.partial(_attn_kernel, tq=tq, tk=tk, n_kv_tiles=nk // tk)
    return pl.pallas_call(
        kern,
        out_shape=jax.ShapeDtypeStruct((n, ATT_WIDTH), F32),
        grid=(n // tq,),
        in_specs=[pl.BlockSpec((tq, ATT_WIDTH), lambda i: (i, 0)),
                  _const_spec((nk, KV_WIDTH)), _const_spec((nk, KV_WIDTH)),
                  pl.BlockSpec((tq, ATT_WIDTH), lambda i: (i, 0))],
        out_specs=pl.BlockSpec((tq, ATT_WIDTH), lambda i: (i, 0)),
        scratch_shapes=[pltpu.VMEM((2, rows, tk), F32), pltpu.VMEM((2, rows, tk), BF16),
                        pltpu.VMEM((rows, LANES), F32), pltpu.VMEM((2, rows, LANES), F32),
                        pltpu.VMEM((rows, 2 * HEAD_DIM), F32)],
        compiler_params=_params(("parallel",), VMEM_LIMIT),
        name="attention",
    )(q, k_all, v_all, bgate)


def _even_out_kernel(a_ref, ap_ref, an_ref, ag_ref, yb_ref, x_ref, gate_ref, pw_ref, ps_ref, wo_ref, o_ref,
                     *, tm, n_rows):
    i = pl.program_id(0)
    n_tiles = pl.num_programs(0)
    u = a_ref[...]
    prev = jnp.where(i > 0, ap_ref[...], 0.0)
    nxt = jnp.where(i < n_tiles - 1, an_ref[...], 0.0)
    e = jnp.concatenate([prev, u, nxt], axis=0)
    n = tm + 2 * POOL_HALO
    s2 = e[0:n - 1] + e[1:n]
    s4 = s2[0:n - 3] + s2[2:n - 1]
    s8 = s4[0:n - 7] + s4[4:n - 3]
    s16 = s8[0:n - 15] + s8[8:n - 7]
    lane = lax.broadcasted_iota(jnp.int32, (tm, POOL_WIDTH), 1)
    grp = lane // POOL_GROUP_DIM
    win = jnp.where(grp == 0, s2[7:7 + tm],
                    jnp.where(grp == 1, s4[6:6 + tm], jnp.where(grp == 2, s8[4:4 + tm], s16[0:tm])))
    half = jnp.where(grp == 0, 1, jnp.where(grp == 1, 2, jnp.where(grp == 2, 4, 8)))
    t = i * tm + lax.broadcasted_iota(jnp.int32, (tm, POOL_WIDTH), 0)
    cnt = (jnp.minimum(t + half, n_rows) - jnp.maximum(t - half, 0)).astype(F32)
    d = win / cnt - u
    ya = jnp.dot(d.astype(BF16), pw_ref[...], preferred_element_type=F32) * ps_ref[...]
    ya = ya * ag_ref[...]
    y = jnp.dot(ya.astype(BF16), wo_ref[0:POOL_WIDTH, :], preferred_element_type=F32)
    y = y + jnp.dot(yb_ref[...].astype(BF16), wo_ref[POOL_WIDTH:, :], preferred_element_type=F32)
    o_ref[...] = x_ref[...] + gate_ref[...] * y


def _even_out(a_val, a_gate, yb, x, gate, pool_bd, pool_scale, w_out_bf):
    n, d = x.shape
    tm = _row_tile(n, 512)
    hb = tm // POOL_HALO
    last = n // POOL_HALO - 1
    row = lambda w: pl.BlockSpec((tm, w), lambda i: (i, 0))
    vec = lambda w: pl.BlockSpec((1, w), lambda i: (0, 0))
    kern = functools.partial(_even_out_kernel, tm=tm, n_rows=n)
    return pl.pallas_call(
        kern,
        out_shape=jax.ShapeDtypeStruct((n, d), F32),
        grid=(n // tm,),
        in_specs=[row(POOL_WIDTH),
                  pl.BlockSpec((POOL_HALO, POOL_WIDTH), lambda i: (jnp.maximum(i * hb - 1, 0), 0)),
                  pl.BlockSpec((POOL_HALO, POOL_WIDTH), lambda i: (jnp.minimum((i + 1) * hb, last), 0)),
                  row(POOL_WIDTH), row(ATT_WIDTH), row(d), vec(d),
                  _const_spec((POOL_WIDTH, POOL_WIDTH)), vec(POOL_WIDTH), _const_spec((d, d))],
        out_specs=row(d),
        compiler_params=_params(("parallel",), VMEM_LIMIT),
        name="even_out",
    )(a_val, a_val, a_val, a_gate, yb, x, gate, pool_bd, pool_scale, w_out_bf)


def _odd_in_kernel(x_ref, xp_ref, xn_ref, g_ref, sc_ref, sh_ref, w_ref, cw_ref, cb_ref, cs_ref,
                   uc_ref, hyg_ref, pq_ref, fng_ref, *, tm, run, pitch):
    i = pl.program_id(0)
    n_tiles = pl.num_programs(0)
    g, sc, sh = g_ref[...], sc_ref[...], sh_ref[...]
    p = jnp.dot(_norm_mod(x_ref[...], g, sc, sh).astype(BF16), w_ref[...], preferred_element_type=F32)
    c0 = (HY_ORDER + 1) * HY_WIDTH
    halo = _norm_mod(jnp.concatenate([xp_ref[...], xn_ref[...]], axis=0), g, sc, sh)
    ph = jnp.dot(halo.astype(BF16), w_ref[:, 0:c0], preferred_element_type=F32)
    prev = jnp.where(i > 0, ph[7:8, :], 0.0)
    nxt = jnp.where(i < n_tiles - 1, ph[8:9, :], 0.0)
    u = p[:, 0:c0]
    e = jnp.concatenate([prev, u, nxt], axis=0)
    uc = e[0:tm] * cw_ref[0:1, :] + u * cw_ref[1:2, :] + e[2:tm + 2] * cw_ref[2:3, :] + cb_ref[...]
    _store_pitched(uc_ref, slice(None), uc, run, pitch)
    hyg_ref[...] = _silu(p[:, c0:c0 + HY_WIDTH])
    fn_in = p[:, c0 + HY_WIDTH:c0 + HY_WIDTH + FN_WIDTH]
    _store_pitched(pq_ref, slice(None), _dot_parts(_split_bf16(fn_in), _ref_parts(cs_ref)), run, pitch)
    fng_ref[...] = _silu(p[:, c0 + HY_WIDTH + FN_WIDTH:])


def _odd_in(x, g, scale, shift, w_bf, conv_w, conv_b, cs):
    n, d = x.shape
    n_in = w_bf.shape[1]
    tm = _row_tile(n, 512)
    hb = tm // 8
    last = n // 8 - 1
    row = lambda w: pl.BlockSpec((tm, w), lambda i: (i, 0))
    vec = lambda w: pl.BlockSpec((1, w), lambda i: (0, 0))
    c0 = (HY_ORDER + 1) * HY_WIDTH
    run, pitch = n // DFT_N1, _time_pitch(n)
    assert tm % run == 0
    tmp = tm // run * pitch
    prow = lambda w: pl.BlockSpec((tmp, w), lambda i: (i, 0))
    kern = functools.partial(_odd_in_kernel, tm=tm, run=run, pitch=pitch)
    return pl.pallas_call(
        kern,
        out_shape=(jax.ShapeDtypeStruct((DFT_N1 * pitch, c0), F32), jax.ShapeDtypeStruct((n, HY_WIDTH), F32),
                   jax.ShapeDtypeStruct((DFT_N1 * pitch, 2 * FN_WIDTH), F32),
                   jax.ShapeDtypeStruct((n, FN_WIDTH), F32)),
        grid=(n // tm,),
        in_specs=[row(d),
                  pl.BlockSpec((8, d), lambda i: (jnp.maximum(i * hb - 1, 0), 0)),
                  pl.BlockSpec((8, d), lambda i: (jnp.minimum((i + 1) * hb, last), 0)),
                  vec(d), vec(d), vec(d), _const_spec((d, n_in)),
                  pl.BlockSpec((3, c0), lambda i: (0, 0)), vec(c0), _const_spec(cs.shape)],
        out_specs=(prow(c0), row(HY_WIDTH), prow(2 * FN_WIDTH), row(FN_WIDTH)),
        compiler_params=_params(("parallel",), VMEM_LIMIT),
        name="odd_in",
    )(x, x, x, g, scale, shift, w_bf, conv_w, conv_b, cs)


def _filter_kernel(emb_ref, w1_ref, b1_ref, w2_ref, b2_ref, w3f_ref, w3b_ref, fr_ref, dl_ref,
                   lo_ref, hi_ref, mass_ref, *, tm, n_rows, run, pitch):
    i = pl.program_id(0)
    fr = fr_ref[...]
    a = jnp.sin(fr * (jnp.dot(emb_ref[...], w1_ref[...], precision=HIGHEST, preferred_element_type=F32)
                      + b1_ref[...]))
    a = jnp.sin(fr * (jnp.dot(a, w2_ref[...], precision=HIGHEST, preferred_element_type=F32) + b2_ref[...]))
    a_f = a_b = a.astype(BF16)
    j = i * tm + lax.broadcasted_iota(jnp.int32, (tm, 1), 0)
    pos_b = jnp.where(j == 0, 0, n_rows - j)
    dl = jnp.abs(dl_ref[...])
    inv = 1.0 / (n_rows - 1)
    decay_f = jnp.exp(-(j.astype(F32) * inv) * dl)
    decay_b = jnp.exp(-(pos_b.astype(F32) * inv) * dl)

    @pl.when(i == 0)
    def _():
        mass_ref[...] = jnp.zeros_like(mass_ref)

    for o in range(HY_ORDER):
        os_ = slice(o * HY_WIDTH, (o + 1) * HY_WIDTH)
        hf = jnp.dot(a_f, w3f_ref[:, os_], preferred_element_type=F32) * decay_f
        hb = jnp.dot(a_b, w3b_ref[:, os_], preferred_element_type=F32) * decay_b
        _store_pitched(lo_ref, os_, hf, run, pitch)
        _store_pitched(hi_ref, os_, jnp.where(j == 0, 0.0, -hb), run, pitch)
        mass_ref[:, os_] += (jnp.sum(jnp.abs(hf), axis=0, keepdims=True)
                             + jnp.sum(jnp.abs(hb), axis=0, keepdims=True))


def _hyena_filters(emb2, w1, b1, w2, b2, w3, freq, deltas):
    n = emb2.shape[0]
    tm = _row_tile(n, 512)
    nh = HY_ORDER * HY_WIDTH
    hh = HY_HIDDEN
    w1b = jnp.zeros((LANES, 2 * hh), F32).at[0:HY_EMB, 0:hh].set(w1).at[hh:hh + HY_EMB, hh:].set(w1)
    w2b = jnp.zeros((2 * hh, 2 * hh), F32).at[0:hh, 0:hh].set(w2).at[hh:, hh:].set(w2)
    w3r = w3.reshape(hh, HY_ORDER, 2, HY_WIDTH)
    zero = jnp.zeros((hh, nh), F32)
    w3f = jnp.concatenate([w3r[:, :, 0, :].reshape(hh, nh), zero], axis=0).astype(BF16)
    w3b = jnp.concatenate([zero, w3r[:, :, 1, :].reshape(hh, nh)], axis=0).astype(BF16)
    twice = lambda a: jnp.concatenate([a, a], axis=-1)
    ops = (emb2, w1b, twice(b1), w2b, twice(b2), w3f, w3b, twice(freq), deltas)
    run, pitch = n // DFT_N1, _time_pitch(n)
    assert tm % run == 0
    prow = pl.BlockSpec((tm // run * pitch, nh), lambda i: (i, 0))
    kern = functools.partial(_filter_kernel, tm=tm, n_rows=n, run=run, pitch=pitch)
    full = lambda a: pl.BlockSpec(a.shape, lambda i: (0,) * a.ndim)
    row = lambda w: pl.BlockSpec((tm, w), lambda i: (i, 0))
    return pl.pallas_call(
        kern,
        out_shape=(jax.ShapeDtypeStruct((DFT_N1 * pitch, nh), F32), jax.ShapeDtypeStruct((DFT_N1 * pitch, nh), F32),
                   jax.ShapeDtypeStruct((1, nh), F32)),
        grid=(n // tm,),
        in_specs=[row(LANES)] + [full(a) for a in ops[1:]],
        out_specs=(prow, prow, pl.BlockSpec((1, nh), lambda i: (0, 0))),
        compiler_params=_params(("arbitrary",), VMEM_LIMIT),
        name="hyena_filters",
    )(*ops)


DFT_PARTS = 1


def _table_parts(a):
    a32 = jnp.asarray(a.astype(np.float32))
    hi = a32.astype(BF16)
    if DFT_PARTS == 1:
        return hi[None]
    lo = (a32 - hi.astype(F32)).astype(BF16)
    return jnp.stack([hi, lo])


def _split_bf16(x):
    hi = x.astype(BF16)
    if DFT_PARTS == 1:
        return (hi,)
    return (hi, (x - hi.astype(F32)).astype(BF16))


def _ref_parts(ref, *idx):
    return tuple(ref[(q,) + idx] for q in range(ref.shape[0]))


def _dot_parts(lhs, rhs):
    acc = jnp.dot(lhs[0], rhs[0], preferred_element_type=F32)
    if len(lhs) > 1:
        acc = acc + jnp.dot(lhs[1], rhs[0], preferred_element_type=F32)
    if len(rhs) > 1:
        acc = acc + jnp.dot(lhs[0], rhs[1], preferred_element_type=F32)
    return acc


def _dft_tables(n_total, n2, half_shift, n1_used):
    n1_full = n_total // n2
    sh = 0.5 if half_shift else 0.0
    k1 = np.arange(DFT_N1, dtype=np.float64)[None, :, None] + sh
    n1 = np.arange(n1_used, dtype=np.float64)[None, None, :]
    nn2 = np.arange(n2, dtype=np.float64)[:, None, None]
    ph = -2.0 * np.pi * (n1 * k1 / n1_full + nn2 * k1 / n_total)
    fwd = np.stack([np.cos(ph), np.sin(ph)], axis=2).reshape(n2, 2 * DFT_N1, n1_used)
    ph_i = ph[:, :, :DFT_N1]
    inv = np.stack([np.cos(ph_i), np.sin(ph_i)], axis=2).reshape(n2, 2 * DFT_N1, DFT_N1)
    inv = (2.0 / n_total) * inv.transpose(0, 2, 1)
    k2 = np.arange(n2, dtype=np.float64)
    ph2 = -2.0 * np.pi * np.outer(k2, k2) / n2
    fr, fi = np.cos(ph2), np.sin(ph2)
    big_fwd = np.block([[fr, -fi], [fi, fr]])
    big_inv = np.block([[fr, fi], [-fi, fr]])
    return _table_parts(fwd), _table_parts(inv), _table_parts(big_fwd), _table_parts(big_inv)


DFT_STEP = 16


def _stage1_fwd(x_refs, m_ref, a_sc, step, *, pitch):
    for u in range(m_ref.shape[1]):
        nn = step * m_ref.shape[1] + u
        xs = [xr[pl.ds(nn, DFT_N1, stride=pitch), :] for xr in x_refs]
        xs = xs[0] if len(xs) == 1 else jnp.concatenate(xs, axis=0)
        a_sc[pl.ds(nn, 2 * DFT_N1, stride=pitch), :] = _dot_parts(_ref_parts(m_ref, u), _split_bf16(xs))


def _load_cplx(a_sc, k1, *, n2, pitch):
    r0 = pl.multiple_of(k1 * 2 * pitch, 8)
    return jnp.concatenate([a_sc[pl.ds(r0, n2), :], a_sc[pl.ds(r0 + pitch, n2), :]], axis=0)


def _phase_steps(n2):
    n2g = min(DFT_STEP, n2)
    return n2g, n2 // n2g, DFT_N1 // DFT_STEP


def _dft_fwd_kernel(*refs, n_x, n2, pitch, ja, scaled):
    x_refs = refs[:n_x]
    rest = refs[n_x:]
    if scaled:
        mass_ref, rest = rest[0], rest[1:]
    m_ref, big_ref, o_ref, a_sc = rest
    j = pl.program_id(1)

    @pl.when(j < ja)
    def _():
        _stage1_fwd(x_refs, m_ref, a_sc, j, pitch=pitch)

    @pl.when(j >= ja)
    def _():
        big = _ref_parts(big_ref)
        for g in range(DFT_STEP):
            blk = _load_cplx(a_sc, (j - ja) * DFT_STEP + g, n2=n2, pitch=pitch)
            out = _dot_parts(big, _split_bf16(blk))
            if scaled:
                out = out * (1.0 / (mass_ref[...] + EPS))
            o_ref[g * 2 * n2:(g + 1) * 2 * n2, :] = out.astype(o_ref.dtype)


def _dft_fwd(xs, width, m_fwd, big, mass=None, out_dtype=F32):
    n2 = xs[0].shape[0] // DFT_N1 - 8
    pitch = n2 + 8
    n = DFT_N1 * pitch
    n2g, ja, jb = _phase_steps(n2)
    wt = LANES
    rows = 2 * DFT_N1 * n2
    kern = functools.partial(_dft_fwd_kernel, n_x=len(xs), n2=n2, pitch=pitch, ja=ja, scaled=mass is not None)
    x_spec = pl.BlockSpec((n, wt), lambda i, j: (0, i), pipeline_mode=pl.Buffered(1))
    ops = list(xs)
    specs = [x_spec] * len(xs)
    if mass is not None:
        ops.append(mass)
        specs.append(pl.BlockSpec((1, wt), lambda i, j: (0, i)))
    m_spec = pl.BlockSpec((m_fwd.shape[0], n2g) + m_fwd.shape[2:],
                          lambda i, j: (0, jnp.minimum(j, ja - 1), 0, 0))
    return pl.pallas_call(
        kern,
        out_shape=jax.ShapeDtypeStruct((rows, width), out_dtype),
        grid=(width // wt, ja + jb),
        in_specs=specs + [m_spec, _const_spec(big.shape)],
        out_specs=pl.BlockSpec((DFT_STEP * 2 * n2, wt), lambda i, j: (jnp.maximum(j - ja, 0), i)),
        scratch_shapes=[pltpu.VMEM((2 * DFT_N1 * pitch, wt), F32)],
        compiler_params=_params(("parallel", "arbitrary"), VMEM_LIMIT),
        name="dft_fwd",
    )(*ops, m_fwd, big)


def _hyena_conv_kernel(x_ref, g_ref, s_ref, k_ref, m_ref, minv_ref, bigf_ref, bigi_ref, y_ref, a_sc,
                       *, n2, pitch, ja, jb):
    j = pl.program_id(1)

    @pl.when(j < ja)
    def _():
        _stage1_fwd([x_ref], m_ref, a_sc, j, pitch=pitch)

    @pl.when((j >= ja) & (j < ja + jb))
    def _():
        bigf = _ref_parts(bigf_ref)
        bigi = _ref_parts(bigi_ref)
        for g in range(DFT_STEP):
            k1 = (j - ja) * DFT_STEP + g
            z = _dot_parts(bigf, _split_bf16(_load_cplx(a_sc, k1, n2=n2, pitch=pitch)))
            zr, zi = z[0:n2], z[n2:2 * n2]
            kr = k_ref[g * 2 * n2:g * 2 * n2 + n2, :].astype(F32)
            ki = k_ref[g * 2 * n2 + n2:(g + 1) * 2 * n2, :].astype(F32)
            prod = jnp.concatenate([zr * kr - zi * ki, zr * ki + zi * kr], axis=0)
            b = _dot_parts(bigi, _split_bf16(prod))
            r0 = pl.multiple_of(k1 * 2 * pitch, 8)
            a_sc[pl.ds(r0, n2), :] = b[0:n2]
            a_sc[pl.ds(r0 + pitch, n2), :] = b[n2:2 * n2]

    @pl.when(j == ja + jb)
    def _():
        for r in range(pitch - n2):
            y_ref[pl.ds(n2 + r, DFT_N1, stride=pitch), :] = jnp.zeros((DFT_N1, y_ref.shape[1]), F32)

    @pl.when(j >= ja + jb)
    def _():
        skip = s_ref[...]
        for u in range(minv_ref.shape[1]):
            nn = (j - ja - jb) * minv_ref.shape[1] + u
            bs = a_sc[pl.ds(nn, 2 * DFT_N1, stride=pitch), :]
            conv = _dot_parts(_ref_parts(minv_ref, u), _split_bf16(bs))
            rows = pl.ds(nn, DFT_N1, stride=pitch)
            y_ref[rows, :] = g_ref[rows, :] * (conv + skip * x_ref[rows, :])


def _hyena_conv(x, x_col, gate, gate_col, skip, kspec, order, m_fwd, m_inv, big_fwd, big_inv):
    n2 = x.shape[0] // DFT_N1 - 8
    pitch = n2 + 8
    n = DFT_N1 * pitch
    n2g, ja, jb = _phase_steps(n2)
    wt = LANES
    c0 = x_col // wt
    cg = gate_col // wt
    ck = order * HY_WIDTH // wt
    kern = functools.partial(_hyena_conv_kernel, n2=n2, pitch=pitch, ja=ja, jb=jb)
    return pl.pallas_call(
        kern,
        out_shape=jax.ShapeDtypeStruct((n, HY_WIDTH), F32),
        grid=(HY_WIDTH // wt, ja + jb + ja),
        in_specs=[pl.BlockSpec((n, wt), lambda i, j: (0, c0 + i), pipeline_mode=pl.Buffered(1)),
                  pl.BlockSpec((n, wt), lambda i, j: (0, cg + i), pipeline_mode=pl.Buffered(1)),
                  pl.BlockSpec((1, wt), lambda i, j: (0, i)),
                  pl.BlockSpec((DFT_STEP * 2 * n2, wt), lambda i, j: (jnp.clip(j - ja, 0, jb - 1), ck + i)),
                  pl.BlockSpec((m_fwd.shape[0], n2g) + m_fwd.shape[2:],
                               lambda i, j: (0, jnp.minimum(j, ja - 1), 0, 0)),
                  pl.BlockSpec((m_inv.shape[0], n2g) + m_inv.shape[2:],
                               lambda i, j: (0, jnp.clip(j - ja - jb, 0, ja - 1), 0, 0)),
                  _const_spec(big_fwd.shape), _const_spec(big_inv.shape)],
        out_specs=pl.BlockSpec((n, wt), lambda i, j: (0, i), pipeline_mode=pl.Buffered(1)),
        scratch_shapes=[pltpu.VMEM((2 * DFT_N1 * pitch, wt), F32)],
        compiler_params=_params(("parallel", "arbitrary"), VMEM_LIMIT),
        name="hyena_conv",
    )(x, gate, skip, kspec, m_fwd, m_inv, big_fwd, big_inv)


def _fn_out_kernel(p_ref, q_ref, w_ref, o_ref, *, kb, norm):
    w = w_ref[...]
    for jj in range(kb):
        r = (p_ref[:, 0, jj, :] + q_ref[:, 0, jj, :]) * norm
        o_ref[jj] = jnp.dot(r.astype(BF16), w, preferred_element_type=F32)


def _fn_out(spec, fn_w_bf, n):
    n2 = n // DFT_N1
    c = FN_WIDTH
    kb = 8
    s4 = spec.reshape(DFT_N1, 2, n2, 2 * c)
    kern = functools.partial(_fn_out_kernel, kb=kb, norm=1.0 / math.sqrt(n * c))
    out = pl.pallas_call(
        kern,
        out_shape=jax.ShapeDtypeStruct((n2, DFT_N1, c), F32),
        grid=(n2 // kb,),
        in_specs=[pl.BlockSpec((DFT_N1, 1, kb, c), lambda i: (0, 0, i, 0)),
                  pl.BlockSpec((DFT_N1, 1, kb, c), lambda i: (0, 1, i, 1)),
                  _const_spec((c, c))],
        out_specs=pl.BlockSpec((kb, DFT_N1, c), lambda i: (i, 0, 0)),
        compiler_params=_params(("parallel",), VMEM_LIMIT),
        name="fn_out",
    )(s4, s4, fn_w_bf)
    return out.reshape(n, c)


def _odd_out_kernel(yc_ref, hyg_ref, yd_ref, fng_ref, x_ref, gate_ref, wo_ref, fg_ref, o_ref, *, run, pitch):
    yc = _load_pitched(yc_ref, run, pitch)
    y = jnp.dot((yc * hyg_ref[...]).astype(BF16), wo_ref[0:HY_WIDTH, :], preferred_element_type=F32)
    y = y + jnp.dot((yd_ref[...] * fng_ref[...]).astype(BF16), wo_ref[HY_WIDTH:, :],
                    preferred_element_type=F32)
    xo = x_ref[...] + gate_ref[...] * y
    o_ref[...] = xo * lax.rsqrt(jnp.mean(xo * xo, axis=-1, keepdims=True) + EPS) * fg_ref[...]


def _odd_out(yc, hy_gate, yd, fn_gate, x, gate, w_out_bf, final_g):
    n, d = x.shape
    tm = _row_tile(n, 512)
    run, pitch = n // DFT_N1, _time_pitch(n)
    row = lambda w: pl.BlockSpec((tm, w), lambda i: (i, 0))
    vec = lambda w: pl.BlockSpec((1, w), lambda i: (0, 0))
    kern = functools.partial(_odd_out_kernel, run=run, pitch=pitch)
    return pl.pallas_call(
        kern,
        out_shape=jax.ShapeDtypeStruct((n, d), F32),
        grid=(n // tm,),
        in_specs=[pl.BlockSpec((tm // run * pitch, HY_WIDTH), lambda i: (i, 0)),
                  row(HY_WIDTH), row(FN_WIDTH), row(FN_WIDTH), row(d), vec(d), _const_spec((d, d)), vec(d)],
        out_specs=row(d),
        compiler_params=_params(("parallel",), VMEM_LIMIT),
        name="odd_out",
    )(yc, hy_gate, yd, fn_gate, x, gate, w_out_bf, final_g)


def _rope_tables(n):
    rows = n // GRID_W
    inv_freq = ROPE_THETA ** (-jnp.arange(ROPE_FREQS, dtype=F32) / ROPE_FREQS)
    ang_r = jnp.arange(rows, dtype=F32)[:, None] * inv_freq
    ang_c = jnp.arange(GRID_W, dtype=F32)[:, None] * inv_freq

    def table(fn, sign):
        r = jnp.broadcast_to(fn(ang_r)[:, None, :], (rows, GRID_W, ROPE_FREQS))
        c = jnp.broadcast_to(fn(ang_c)[None, :, :], (rows, GRID_W, ROPE_FREQS))
        return jnp.concatenate([sign * r, r, sign * c, c], axis=-1).reshape(n, HEAD_DIM)

    return table(jnp.cos, 1.0), table(jnp.sin, -1.0)


def _hyena_embedding(n, pos):
    t = (pos.astype(F32) / (n - 1))[:, None]
    w = 2.0 * math.pi * pos.astype(F32)[:, None] / n
    f = jnp.linspace(1e-4, HY_BANDS - 1, HY_BANDS, dtype=F32)[None, :]
    emb = jnp.concatenate([t, jnp.cos(f * w), -jnp.sin(f * w)], axis=-1)
    return jnp.pad(emb, ((0, 0), (0, HY_HIDDEN - HY_EMB)))


def kernel(x, c, ctx, c_ctx, w_mod, b_mod, norm_g, ev_w_in, ev_w_out, pool_w, pool_scale, q_norm_g, k_norm_g,
           od_w_in, od_w_out, hy_conv_w, hy_conv_b, hy_w1, hy_b1, hy_w2, hy_b2, hy_w3, hy_freq, hy_skip, fn_w,
           final_g):
    n, d = x.shape[1], x.shape[2]
    nc = ctx.shape[1]
    x0 = x[0]
    ctx0 = ctx[0]
    vec = lambda a: a.reshape(1, -1)

    cond = jnp.zeros((8, d), F32).at[0].set(c[0]).at[1].set(c_ctx)
    mod = _modulation(cond, w_mod, b_mod)
    shift0, scale0, gate0 = (mod[0, 0:1, k * d:(k + 1) * d] for k in range(3))
    cshift0, cscale0 = mod[0, 1:2, 0:d], mod[0, 1:2, d:2 * d]
    shift1, scale1, gate1 = (mod[1, 0:1, k * d:(k + 1) * d] for k in range(3))

    w_in0 = ev_w_in[0].astype(BF16)
    cos, sin = _rope_tables(n)
    g0 = vec(norm_g[0])
    qg, kg = vec(q_norm_g[0]), vec(k_norm_g[0])
    a_val, a_gate, q, k, v, b_gate = _even_in(x0, g0, scale0, shift0, w_in0, qg, kg, cos, sin)
    ones = jnp.ones((nc, HEAD_DIM), F32)
    _, _, _, ck, cv, _ = _even_in(ctx0, g0, cscale0, cshift0, w_in0, qg, kg, ones, jnp.zeros_like(ones))
    yb = _attention(q, jnp.concatenate([ck, k], axis=0), jnp.concatenate([cv, v], axis=0), b_gate)
    pool_bd = jax.scipy.linalg.block_diag(*[pool_w[0, gi] for gi in range(pool_w.shape[1])]).astype(BF16)
    x1 = _even_out(a_val, a_gate, yb, x0, gate0, pool_bd, vec(pool_scale[0]), ev_w_out[0].astype(BF16))

    ch = np.arange(FN_WIDTH, dtype=np.float64)
    ph = 2.0 * np.pi * np.outer(ch, ch) / FN_WIDTH
    cs = _table_parts(np.concatenate([np.cos(ph), np.sin(ph)], axis=1))
    uc, hy_gate, pq, fn_gate = _odd_in(x1, vec(norm_g[1]), scale1, shift1, od_w_in[0].astype(BF16),
                                       hy_conv_w[0], vec(hy_conv_b[0]), cs)

    max_decay = math.log(HY_DECAY_TARGET) / HY_FAST_DECAY
    min_decay = math.log(HY_DECAY_TARGET) / HY_SLOW_DECAY
    deltas = jnp.linspace(min_decay, max_decay, HY_WIDTH, dtype=F32)[None, :]
    pos = jnp.arange(n, dtype=jnp.int32)
    emb2 = jnp.concatenate([_hyena_embedding(n, pos),
                            _hyena_embedding(n, jnp.where(pos == 0, 0, n - pos))], axis=-1)
    k_lo, k_hi, mass = _hyena_filters(emb2, hy_w1[0], vec(hy_b1[0]), hy_w2[0], vec(hy_b2[0]), hy_w3[0],
                                      vec(hy_freq[0]), deltas)

    n2 = n // DFT_N1
    m_full, _, big_fwd, big_inv = _dft_tables(2 * n, n2, True, 2 * DFT_N1)
    m_half, m_inv, _, _ = _dft_tables(2 * n, n2, True, DFT_N1)
    k_spec = _dft_fwd([k_lo, k_hi], HY_ORDER * HY_WIDTH, m_full, big_fwd, mass=mass, out_dtype=BF16)
    z1 = _hyena_conv(uc, 0, uc, HY_WIDTH, hy_skip[0, 0:1], k_spec, 0, m_half, m_inv, big_fwd, big_inv)
    yc = _hyena_conv(z1, 0, uc, 2 * HY_WIDTH, hy_skip[0, 1:2], k_spec, 1, m_half, m_inv, big_fwd, big_inv)

    f_half, _, f_big, _ = _dft_tables(n, n2, False, DFT_N1)
    pq_spec = _dft_fwd([pq], 2 * FN_WIDTH, f_half, f_big)
    yd = _fn_out(pq_spec, fn_w[0].astype(BF16), n)

    out = _odd_out(yc, hy_gate, yd, fn_gate, x1, gate1, od_w_out[0].astype(BF16), vec(final_g))
    return out[None]
```

```python
import functools
import math

import numpy as np
import jax
import jax.numpy as jnp
from jax import lax
from jax.experimental import pallas as pl
from jax.experimental.pallas import tpu as pltpu

F32 = jnp.float32
BF16 = jnp.bfloat16
HIGHEST = lax.Precision.HIGHEST

EPS = 1e-6
GRID_W = 64
HEAD_DIM = 128
ROPE_FREQS = 32
ROPE_THETA = 10000.0
N_Q_HEADS = 6
N_KV_HEADS = 2
Q_PER_KV = N_Q_HEADS // N_KV_HEADS
POOL_WIDTH = 256
POOL_GROUP_DIM = 64
POOL_WINDOWS = (2, 4, 8, 16)
POOL_HALO = 8
ATT_WIDTH = N_Q_HEADS * HEAD_DIM
KV_WIDTH = N_KV_HEADS * HEAD_DIM
HY_WIDTH = 768
HY_ORDER = 2
HY_EMB = 33
HY_BANDS = 16
HY_HIDDEN = 64
FN_WIDTH = 256
HY_DECAY_TARGET = 1e-2
HY_FAST_DECAY = 0.3
HY_SLOW_DECAY = 1.5

LANES = 128
DFT_N1 = 128
VMEM_LIMIT = 56 * 1024 * 1024
CONV_VMEM_LIMIT = 62 * 1024 * 1024


def _row_tile(n, pref):
    t = min(pref, n)
    assert n % t == 0
    return t


def _silu(x):
    return x * jax.nn.sigmoid(x)


def _params(sem, vmem=None, flags=None):
    return pltpu.CompilerParams(dimension_semantics=sem, vmem_limit_bytes=vmem, flags=flags)


def _const_spec(shape):
    nd = len(shape)
    return pl.BlockSpec(shape, lambda *_: (0,) * nd, pipeline_mode=pl.Buffered(1))


def _mod_kernel(cond_ref, w_ref, b_ref, o_ref):
    s = _silu(cond_ref[...])
    o_ref[0] = jnp.dot(s, w_ref[0], precision=HIGHEST, preferred_element_type=F32) + b_ref[0]


def _modulation(cond, w_mod, b_mod):
    depth, d, d3 = w_mod.shape
    tn = 1024
    return pl.pallas_call(
        _mod_kernel,
        out_shape=jax.ShapeDtypeStruct((depth, 8, d3), F32),
        grid=(depth, d3 // tn),
        in_specs=[pl.BlockSpec((8, d), lambda i, j: (0, 0)),
                  pl.BlockSpec((1, d, tn), lambda i, j: (i, 0, j)),
                  pl.BlockSpec((1, 1, tn), lambda i, j: (i, 0, j))],
        out_specs=pl.BlockSpec((1, 8, tn), lambda i, j: (i, 0, j)),
        compiler_params=_params(("arbitrary", "arbitrary")),
        name="modulation",
    )(cond, w_mod, b_mod.reshape(depth, 1, d3))


def _time_pitch(n):
    return n // DFT_N1 + 8


def _store_pitched(ref, cols, val, run, pitch):
    for b in range(val.shape[0] // run):
        ref[b * pitch:b * pitch + run, cols] = val[b * run:(b + 1) * run]
        ref[b * pitch + run:(b + 1) * pitch, cols] = jnp.zeros((pitch - run, val.shape[1]), val.dtype)


def _load_pitched(ref, run, pitch):
    return jnp.concatenate([ref[b * pitch:b * pitch + run, :] for b in range(ref.shape[0] // pitch)], axis=0)


def _norm_mod(x, g, scale, shift):
    y = x * lax.rsqrt(jnp.mean(x * x, axis=-1, keepdims=True) + EPS)
    return (y * g) * (1.0 + scale) + shift


def _even_in_kernel(x_ref, g_ref, sc_ref, sh_ref, w_ref, qg_ref, kg_ref, cos_ref, sin_ref,
                    aval_ref, agate_ref, q_ref, k_ref, v_ref, bgate_ref):
    half = x_ref.shape[0] // 2
    for s in range(2):
        rs = slice(s * half, (s + 1) * half)
        h = _norm_mod(x_ref[rs, :], g_ref[...], sc_ref[...], sh_ref[...])
        p = jnp.dot(h.astype(BF16), w_ref[...], preferred_element_type=F32)
        aval_ref[rs, :] = p[:, 0:POOL_WIDTH]
        agate_ref[rs, :] = _silu(p[:, POOL_WIDTH:2 * POOL_WIDTH])
        cos = cos_ref[rs, :]
        sin = sin_ref[rs, :]
        lane = lax.broadcasted_iota(jnp.int32, cos.shape, 1)
        low_half = (lane % (2 * ROPE_FREQS)) < ROPE_FREQS

        def head(xh, g):
            y = xh * lax.rsqrt(jnp.mean(xh * xh, axis=-1, keepdims=True) + EPS) * g
            rot = jnp.where(low_half, pltpu.roll(y, HEAD_DIM - ROPE_FREQS, 1), pltpu.roll(y, ROPE_FREQS, 1))
            return y * cos + rot * sin

        q0 = 2 * POOL_WIDTH
        for hq in range(N_Q_HEADS):
            sl = slice(hq * HEAD_DIM, (hq + 1) * HEAD_DIM)
            q_ref[rs, sl] = head(p[:, q0 + hq * HEAD_DIM:q0 + (hq + 1) * HEAD_DIM], qg_ref[...]).astype(BF16)
        k0 = q0 + ATT_WIDTH
        for hk in range(N_KV_HEADS):
            sl = slice(hk * HEAD_DIM, (hk + 1) * HEAD_DIM)
            k_ref[rs, sl] = head(p[:, k0 + hk * HEAD_DIM:k0 + (hk + 1) * HEAD_DIM], kg_ref[...]).astype(BF16)
        v0 = k0 + KV_WIDTH
        v_ref[rs, :] = p[:, v0:v0 + KV_WIDTH].astype(BF16)
        b0 = v0 + KV_WIDTH
        bgate_ref[rs, :] = _silu(p[:, b0:b0 + ATT_WIDTH])


def _even_in(x, g, scale, shift, w_bf, qg, kg, cos, sin):
    n, d = x.shape
    n_in = w_bf.shape[1]
    tm = _row_tile(n, 512)
    row = lambda w: pl.BlockSpec((tm, w), lambda i: (i, 0))
    vec = lambda w: pl.BlockSpec((1, w), lambda i: (0, 0))
    return pl.pallas_call(
        _even_in_kernel,
        out_shape=(jax.ShapeDtypeStruct((n, POOL_WIDTH), F32), jax.ShapeDtypeStruct((n, POOL_WIDTH), F32),
                   jax.ShapeDtypeStruct((n, ATT_WIDTH), BF16), jax.ShapeDtypeStruct((n, KV_WIDTH), BF16),
                   jax.ShapeDtypeStruct((n, KV_WIDTH), BF16), jax.ShapeDtypeStruct((n, ATT_WIDTH), F32)),
        grid=(n // tm,),
        in_specs=[row(d), vec(d), vec(d), vec(d), _const_spec((d, n_in)), vec(HEAD_DIM), vec(HEAD_DIM),
                  row(HEAD_DIM), row(HEAD_DIM)],
        out_specs=(row(POOL_WIDTH), row(POOL_WIDTH), row(ATT_WIDTH), row(KV_WIDTH), row(KV_WIDTH),
                   row(ATT_WIDTH)),
        compiler_params=_params(("parallel",), VMEM_LIMIT),
        name="even_in",
    )(x, g, scale, shift, w_bf, qg, kg, cos, sin)


ATT_UNROLL = 4
ATT_ROW_CHUNK = 16


def _attn_kernel(q_ref, k_ref, v_ref, bg_ref, o_ref, s_sc, p_sc, m_sc, al_sc, acc_sc, *, tq, tk, n_kv_tiles):
    c = (HEAD_DIM ** -0.5) * math.log2(math.e)
    rows = Q_PER_KV * tq
    for h in range(N_KV_HEADS):
        hs = slice(h * HEAD_DIM, (h + 1) * HEAD_DIM)
        qs = jnp.concatenate(
            [q_ref[:, (Q_PER_KV * h + g) * HEAD_DIM:(Q_PER_KV * h + g + 1) * HEAD_DIM] for g in range(Q_PER_KV)],
            axis=0)

        def scores(buf, t):
            kt = k_ref[pl.ds(t * tk if isinstance(t, int) else pl.multiple_of(t * tk, tk), tk), hs]
            s_sc[buf] = lax.dot_general(qs, kt, (((1,), (1,)), ((), ())), preferred_element_type=F32)

        def softmax(buf):
            for r in range(rows // ATT_ROW_CHUNK):
                rs = slice(r * ATT_ROW_CHUNK, (r + 1) * ATT_ROW_CHUNK)
                cols = [s_sc[buf, rs, j * LANES:(j + 1) * LANES] for j in range(tk // LANES)]
                mx = functools.reduce(jnp.maximum, cols)
                m_old = m_sc[rs, :]
                m_new = jnp.maximum(m_old, jnp.max(mx, axis=-1, keepdims=True) * c)
                for j, sj in enumerate(cols):
                    p_sc[buf, rs, j * LANES:(j + 1) * LANES] = jnp.exp2(sj * c - m_new).astype(BF16)
                al_sc[buf, rs, :] = jnp.exp2(m_old - m_new)
                m_sc[rs, :] = m_new

        def accumulate(buf, t):
            vt = v_ref[pl.ds(t * tk if isinstance(t, int) else pl.multiple_of(t * tk, tk), tk), hs]
            v_ext = jnp.concatenate([vt, jnp.ones((tk, HEAD_DIM), BF16)], axis=1)
            pv = jnp.dot(p_sc[buf], v_ext, preferred_element_type=F32)
            al = al_sc[buf]
            for j in range(2):
                js = slice(j * HEAD_DIM, (j + 1) * HEAD_DIM)
                acc_sc[:, js] = al * acc_sc[:, js] + pv[:, js]

        def slot(t, buf, with_scores=True):
            if with_scores:
                scores(1 - buf, t + 1)
            softmax(buf)
            accumulate(buf, t)

        m_sc[...] = jnp.full(m_sc.shape, -1e30, F32)
        acc_sc[...] = jnp.zeros(acc_sc.shape, F32)
        scores(0, 0)

        def body(i, carry):
            for u in range(ATT_UNROLL):
                slot(ATT_UNROLL * i + u, u % 2)
            return carry

        n_full = (n_kv_tiles - 1) // ATT_UNROLL
        lax.fori_loop(0, n_full, body, 0)
        for t in range(ATT_UNROLL * n_full, n_kv_tiles):
            slot(t, t % 2, with_scores=t + 1 < n_kv_tiles)
        out = acc_sc[:, 0:HEAD_DIM] / acc_sc[:, HEAD_DIM:2 * HEAD_DIM]
        for g in range(Q_PER_KV):
            cs = slice((Q_PER_KV * h + g) * HEAD_DIM, (Q_PER_KV * h + g + 1) * HEAD_DIM)
            o_ref[:, cs] = (out[g * tq:(g + 1) * tq] * bg_ref[:, cs]).astype(o_ref.dtype)


ATT_KEY_TILE = 1280


def _attention(q, k_all, v_all, bgate):
    n = q.shape[0]
    nk = k_all.shape[0]
    tq = _row_tile(n, 256)
    tk = max(t for t in range(LANES, ATT_KEY_TILE + 1, LANES) if nk % t == 0)
    rows = Q_PER_KV * tq
    kern = functools.partial(_attn_kernel, tq=tq, tk=tk, n_kv_tiles=nk // tk)
    return pl.pallas_call(
        kern,
        out_shape=jax.ShapeDtypeStruct((n, ATT_WIDTH), BF16),
        grid=(n // tq,),
        in_specs=[pl.BlockSpec((tq, ATT_WIDTH), lambda i: (i, 0)),
                  _const_spec((nk, KV_WIDTH)), _const_spec((nk, KV_WIDTH)),
                  pl.BlockSpec((tq, ATT_WIDTH), lambda i: (i, 0))],
        out_specs=pl.BlockSpec((tq, ATT_WIDTH), lambda i: (i, 0)),
        scratch_shapes=[pltpu.VMEM((2, rows, tk), F32), pltpu.VMEM((2, rows, tk), BF16),
                        pltpu.VMEM((rows, LANES), F32), pltpu.VMEM((2, rows, LANES), F32),
                        pltpu.VMEM((rows, 2 * HEAD_DIM), F32)],
        compiler_params=_params(("parallel",), VMEM_LIMIT),
        name="attention",
    )(q, k_all, v_all, bgate)


def _even_out_kernel(a_ref, ap_ref, an_ref, ag_ref, yb_ref, x_ref, gate_ref, pw_ref, ps_ref, wo_ref, o_ref,
                     *, tm, n_rows):
    i = pl.program_id(0)
    n_tiles = pl.num_programs(0)
    u = a_ref[...]
    prev = jnp.where(i > 0, ap_ref[...], 0.0)
    nxt = jnp.where(i < n_tiles - 1, an_ref[...], 0.0)
    e = jnp.concatenate([prev, u, nxt], axis=0)
    n = tm + 2 * POOL_HALO
    s2 = e[0:n - 1] + e[1:n]
    s4 = s2[0:n - 3] + s2[2:n - 1]
    s8 = s4[0:n - 7] + s4[4:n - 3]
    s16 = s8[0:n - 15] + s8[8:n - 7]
    lane = lax.broadcasted_iota(jnp.int32, (tm, POOL_WIDTH), 1)
    grp = lane // POOL_GROUP_DIM
    win = jnp.where(grp == 0, s2[7:7 + tm],
                    jnp.where(grp == 1, s4[6:6 + tm], jnp.where(grp == 2, s8[4:4 + tm], s16[0:tm])))
    half = jnp.where(grp == 0, 1, jnp.where(grp == 1, 2, jnp.where(grp == 2, 4, 8)))
    t = i * tm + lax.broadcasted_iota(jnp.int32, (tm, POOL_WIDTH), 0)
    cnt = (jnp.minimum(t + half, n_rows) - jnp.maximum(t - half, 0)).astype(F32)
    d = win / cnt - u
    ya = jnp.dot(d.astype(BF16), pw_ref[...], preferred_element_type=F32) * ps_ref[...]
    ya = ya * ag_ref[...]
    y = jnp.dot(ya.astype(BF16), wo_ref[0:POOL_WIDTH, :], preferred_element_type=F32)
    y = y + jnp.dot(yb_ref[...].astype(BF16), wo_ref[POOL_WIDTH:, :], preferred_element_type=F32)
    o_ref[...] = x_ref[...] + gate_ref[...] * y


def _even_out(a_val, a_gate, yb, x, gate, pool_bd, pool_scale, w_out_bf):
    n, d = x.shape
    tm = _row_tile(n, 512)
    hb = tm // POOL_HALO
    last = n // POOL_HALO - 1
    row = lambda w: pl.BlockSpec((tm, w), lambda i: (i, 0))
    vec = lambda w: pl.BlockSpec((1, w), lambda i: (0, 0))
    kern = functools.partial(_even_out_kernel, tm=tm, n_rows=n)
    return pl.pallas_call(
        kern,
        out_shape=jax.ShapeDtypeStruct((n, d), F32),
        grid=(n // tm,),
        in_specs=[row(POOL_WIDTH),
                  pl.BlockSpec((POOL_HALO, POOL_WIDTH), lambda i: (jnp.maximum(i * hb - 1, 0), 0)),
                  pl.BlockSpec((POOL_HALO, POOL_WIDTH), lambda i: (jnp.minimum((i + 1) * hb, last), 0)),
                  row(POOL_WIDTH), row(ATT_WIDTH), row(d), vec(d),
                  _const_spec((POOL_WIDTH, POOL_WIDTH)), vec(POOL_WIDTH), _const_spec((d, d))],
        out_specs=row(d),
        compiler_params=_params(("parallel",), VMEM_LIMIT),
        name="even_out",
    )(a_val, a_val, a_val, a_gate, yb, x, gate, pool_bd, pool_scale, w_out_bf)


def _odd_in_kernel(x_ref, xp_ref, xn_ref, g_ref, sc_ref, sh_ref, w_ref, cw_ref, cb_ref, cs_ref,
                   uc_ref, hyg_ref, pq_ref, fng_ref, *, tm, run, pitch):
    i = pl.program_id(0)
    n_tiles = pl.num_programs(0)
    g, sc, sh = g_ref[...], sc_ref[...], sh_ref[...]
    p = jnp.dot(_norm_mod(x_ref[...], g, sc, sh).astype(BF16), w_ref[...], preferred_element_type=F32)
    c0 = (HY_ORDER + 1) * HY_WIDTH
    halo = _norm_mod(jnp.concatenate([xp_ref[...], xn_ref[...]], axis=0), g, sc, sh)
    ph = jnp.dot(halo.astype(BF16), w_ref[:, 0:c0], preferred_element_type=F32)
    prev = jnp.where(i > 0, ph[7:8, :], 0.0)
    nxt = jnp.where(i < n_tiles - 1, ph[8:9, :], 0.0)
    u = p[:, 0:c0]
    e = jnp.concatenate([prev, u, nxt], axis=0)
    uc = e[0:tm] * cw_ref[0:1, :] + u * cw_ref[1:2, :] + e[2:tm + 2] * cw_ref[2:3, :] + cb_ref[...]
    _store_pitched(uc_ref, slice(None), uc, run, pitch)
    hyg_ref[...] = _silu(p[:, c0:c0 + HY_WIDTH])
    fn_in = p[:, c0 + HY_WIDTH:c0 + HY_WIDTH + FN_WIDTH]
    _store_pitched(pq_ref, slice(None), _dot_parts(_split_bf16(fn_in), _ref_parts(cs_ref)), run, pitch)
    fng_ref[...] = _silu(p[:, c0 + HY_WIDTH + FN_WIDTH:])


def _odd_in(x, g, scale, shift, w_bf, conv_w, conv_b, cs):
    n, d = x.shape
    n_in = w_bf.shape[1]
    tm = _row_tile(n, 512)
    hb = tm // 8
    last = n // 8 - 1
    row = lambda w: pl.BlockSpec((tm, w), lambda i: (i, 0))
    vec = lambda w: pl.BlockSpec((1, w), lambda i: (0, 0))
    c0 = (HY_ORDER + 1) * HY_WIDTH
    run, pitch = n // DFT_N1, _time_pitch(n)
    assert tm % run == 0
    tmp = tm // run * pitch
    prow = lambda w: pl.BlockSpec((tmp, w), lambda i: (i, 0))
    kern = functools.partial(_odd_in_kernel, tm=tm, run=run, pitch=pitch)
    return pl.pallas_call(
        kern,
        out_shape=(jax.ShapeDtypeStruct((DFT_N1 * pitch, c0), F32), jax.ShapeDtypeStruct((n, HY_WIDTH), F32),
                   jax.ShapeDtypeStruct((DFT_N1 * pitch, 2 * FN_WIDTH), F32),
                   jax.ShapeDtypeStruct((n, FN_WIDTH), F32)),
        grid=(n // tm,),
        in_specs=[row(d),
                  pl.BlockSpec((8, d), lambda i: (jnp.maximum(i * hb - 1, 0), 0)),
                  pl.BlockSpec((8, d), lambda i: (jnp.minimum((i + 1) * hb, last), 0)),
                  vec(d), vec(d), vec(d), _const_spec((d, n_in)),
                  pl.BlockSpec((3, c0), lambda i: (0, 0)), vec(c0), _const_spec(cs.shape)],
        out_specs=(prow(c0), row(HY_WIDTH), prow(2 * FN_WIDTH), row(FN_WIDTH)),
        compiler_params=_params(("parallel",), VMEM_LIMIT),
        name="odd_in",
    )(x, x, x, g, scale, shift, w_bf, conv_w, conv_b, cs)


def _filter_kernel(emb_ref, w1_ref, b1_ref, w2_ref, b2_ref, w3f_ref, w3b_ref, fr_ref, dl_ref,
                   lo_ref, hi_ref, mass_ref, *, tm, n_rows, run, pitch):
    i = pl.program_id(0)
    fr = fr_ref[...]
    a = jnp.sin(fr * (jnp.dot(emb_ref[...], w1_ref[...], precision=HIGHEST, preferred_element_type=F32)
                      + b1_ref[...]))
    a = jnp.sin(fr * (jnp.dot(a, w2_ref[...], precision=HIGHEST, preferred_element_type=F32) + b2_ref[...]))
    a_f = a_b = a.astype(BF16)
    j = i * tm + lax.broadcasted_iota(jnp.int32, (tm, 1), 0)
    pos_b = jnp.where(j == 0, 0, n_rows - j)
    dl = jnp.abs(dl_ref[...])
    inv = 1.0 / (n_rows - 1)
    decay_f = jnp.exp(-(j.astype(F32) * inv) * dl)
    decay_b = jnp.exp(-(pos_b.astype(F32) * inv) * dl)

    @pl.when(i == 0)
    def _():
        mass_ref[...] = jnp.zeros_like(mass_ref)

    for o in range(HY_ORDER):
        os_ = slice(o * HY_WIDTH, (o + 1) * HY_WIDTH)
        hf = jnp.dot(a_f, w3f_ref[:, os_], preferred_element_type=F32) * decay_f
        hb = jnp.dot(a_b, w3b_ref[:, os_], preferred_element_type=F32) * decay_b
        _store_pitched(lo_ref, os_, hf, run, pitch)
        _store_pitched(hi_ref, os_, jnp.where(j == 0, 0.0, -hb), run, pitch)
        mass_ref[:, os_] += (jnp.sum(jnp.abs(hf), axis=0, keepdims=True)
                             + jnp.sum(jnp.abs(hb), axis=0, keepdims=True))


def _hyena_filters(emb2, w1, b1, w2, b2, w3, freq, deltas):
    n = emb2.shape[0]
    tm = _row_tile(n, 512)
    nh = HY_ORDER * HY_WIDTH
    hh = HY_HIDDEN
    w1b = jnp.zeros((LANES, 2 * hh), F32).at[0:HY_EMB, 0:hh].set(w1).at[hh:hh + HY_EMB, hh:].set(w1)
    w2b = jnp.zeros((2 * hh, 2 * hh), F32).at[0:hh, 0:hh].set(w2).at[hh:, hh:].set(w2)
    w3r = w3.reshape(hh, HY_ORDER, 2, HY_WIDTH)
    zero = jnp.zeros((hh, nh), F32)
    w3f = jnp.concatenate([w3r[:, :, 0, :].reshape(hh, nh), zero], axis=0).astype(BF16)
    w3b = jnp.concatenate([zero, w3r[:, :, 1, :].reshape(hh, nh)], axis=0).astype(BF16)
    twice = lambda a: jnp.concatenate([a, a], axis=-1)
    ops = (emb2, w1b, twice(b1), w2b, twice(b2), w3f, w3b, twice(freq), deltas)
    run, pitch = n // DFT_N1, _time_pitch(n)
    assert tm % run == 0
    prow = pl.BlockSpec((tm // run * pitch, nh), lambda i: (i, 0))
    kern = functools.partial(_filter_kernel, tm=tm, n_rows=n, run=run, pitch=pitch)
    full = lambda a: pl.BlockSpec(a.shape, lambda i: (0,) * a.ndim)
    row = lambda w: pl.BlockSpec((tm, w), lambda i: (i, 0))
    return pl.pallas_call(
        kern,
        out_shape=(jax.ShapeDtypeStruct((DFT_N1 * pitch, nh), F32), jax.ShapeDtypeStruct((DFT_N1 * pitch, nh), F32),
                   jax.ShapeDtypeStruct((1, nh), F32)),
        grid=(n // tm,),
        in_specs=[row(LANES)] + [full(a) for a in ops[1:]],
        out_specs=(prow, prow, pl.BlockSpec((1, nh), lambda i: (0, 0))),
        compiler_params=_params(("arbitrary",), VMEM_LIMIT),
        name="hyena_filters",
    )(*ops)


DFT_PARTS = 1


def _table_parts(a):
    a32 = jnp.asarray(a.astype(np.float32))
    hi = a32.astype(BF16)
    if DFT_PARTS == 1:
        return hi[None]
    lo = (a32 - hi.astype(F32)).astype(BF16)
    return jnp.stack([hi, lo])


def _split_bf16(x):
    hi = x.astype(BF16)
    if DFT_PARTS == 1:
        return (hi,)
    return (hi, (x - hi.astype(F32)).astype(BF16))


def _ref_parts(ref, *idx):
    return tuple(ref[(q,) + idx] for q in range(ref.shape[0]))


def _dot_parts(lhs, rhs):
    acc = jnp.dot(lhs[0], rhs[0], preferred_element_type=F32)
    if len(lhs) > 1:
        acc = acc + jnp.dot(lhs[1], rhs[0], preferred_element_type=F32)
    if len(rhs) > 1:
        acc = acc + jnp.dot(lhs[0], rhs[1], preferred_element_type=F32)
    return acc


def _dft_tables(n_total, n2, half_shift, n1_used):
    n1_full = n_total // n2
    sh = 0.5 if half_shift else 0.0
    k1 = np.arange(DFT_N1, dtype=np.float64)[None, :, None] + sh
    n1 = np.arange(n1_used, dtype=np.float64)[None, None, :]
    nn2 = np.arange(n2, dtype=np.float64)[:, None, None]
    ph = -2.0 * np.pi * (n1 * k1 / n1_full + nn2 * k1 / n_total)
    fwd = np.stack([np.cos(ph), np.sin(ph)], axis=2).reshape(n2, 2 * DFT_N1, n1_used)
    ph_i = ph[:, :, :DFT_N1]
    inv = np.stack([np.cos(ph_i), np.sin(ph_i)], axis=2).reshape(n2, 2 * DFT_N1, DFT_N1)
    inv = (2.0 / n_total) * inv.transpose(0, 2, 1)
    k2 = np.arange(n2, dtype=np.float64)
    ph2 = -2.0 * np.pi * np.outer(k2, k2) / n2
    fr, fi = np.cos(ph2), np.sin(ph2)
    big_fwd = np.block([[fr, -fi], [fi, fr]])
    big_inv = np.block([[fr, fi], [-fi, fr]])
    return _table_parts(fwd), _table_parts(inv), _table_parts(big_fwd), _table_parts(big_inv)


DFT_STEP = 16


def _stage1_fwd(x_refs, m_ref, a_sc, step, *, pitch):
    for u in range(m_ref.shape[1]):
        nn = step * m_ref.shape[1] + u
        xs = [xr[pl.ds(nn, DFT_N1, stride=pitch), :] for xr in x_refs]
        xs = xs[0] if len(xs) == 1 else jnp.concatenate(xs, axis=0)
        a_sc[pl.ds(nn, 2 * DFT_N1, stride=pitch), :] = _dot_parts(_ref_parts(m_ref, u), _split_bf16(xs))


def _load_cplx(a_sc, k1, *, n2, pitch):
    r0 = pl.multiple_of(k1 * 2 * pitch, 8)
    return jnp.concatenate([a_sc[pl.ds(r0, n2), :], a_sc[pl.ds(r0 + pitch, n2), :]], axis=0)


def _phase_steps(n2):
    n2g = min(DFT_STEP, n2)
    return n2g, n2 // n2g, DFT_N1 // DFT_STEP


def _dft_fwd_kernel(*refs, n_x, n2, pitch, ja, scaled):
    x_refs = refs[:n_x]
    rest = refs[n_x:]
    if scaled:
        mass_ref, rest = rest[0], rest[1:]
    m_ref, big_ref, o_ref, a_sc = rest
    j = pl.program_id(1)

    @pl.when(j < ja)
    def _():
        _stage1_fwd(x_refs, m_ref, a_sc, j, pitch=pitch)

    @pl.when(j >= ja)
    def _():
        big = _ref_parts(big_ref)
        for g in range(DFT_STEP):
            blk = _load_cplx(a_sc, (j - ja) * DFT_STEP + g, n2=n2, pitch=pitch)
            out = _dot_parts(big, _split_bf16(blk))
            if scaled:
                out = out * (1.0 / (mass_ref[...] + EPS))
            o_ref[g * 2 * n2:(g + 1) * 2 * n2, :] = out.astype(o_ref.dtype)


def _dft_fwd(xs, width, m_fwd, big, mass=None, out_dtype=F32):
    n2 = xs[0].shape[0] // DFT_N1 - 8
    pitch = n2 + 8
    n = DFT_N1 * pitch
    n2g, ja, jb = _phase_steps(n2)
    wt = LANES
    rows = 2 * DFT_N1 * n2
    kern = functools.partial(_dft_fwd_kernel, n_x=len(xs), n2=n2, pitch=pitch, ja=ja, scaled=mass is not None)
    x_spec = pl.BlockSpec((n, wt), lambda i, j: (0, i), pipeline_mode=pl.Buffered(1))
    ops = list(xs)
    specs = [x_spec] * len(xs)
    if mass is not None:
        ops.append(mass)
        specs.append(pl.BlockSpec((1, wt), lambda i, j: (0, i)))
    m_spec = pl.BlockSpec((m_fwd.shape[0], n2g) + m_fwd.shape[2:],
                          lambda i, j: (0, jnp.minimum(j, ja - 1), 0, 0))
    return pl.pallas_call(
        kern,
        out_shape=jax.ShapeDtypeStruct((rows, width), out_dtype),
        grid=(width // wt, ja + jb),
        in_specs=specs + [m_spec, _const_spec(big.shape)],
        out_specs=pl.BlockSpec((DFT_STEP * 2 * n2, wt), lambda i, j: (jnp.maximum(j - ja, 0), i)),
        scratch_shapes=[pltpu.VMEM((2 * DFT_N1 * pitch, wt), F32)],
        compiler_params=_params(("parallel", "arbitrary"), VMEM_LIMIT),
        name="dft_fwd",
    )(*ops, m_fwd, big)


def _hyena_conv_kernel(x_ref, g_ref, s_ref, k_ref, m_ref, minv_ref, bigf_ref, bigi_ref, y_ref, a_sc,
                       *, n2, pitch, ja, jb):
    j = pl.program_id(1)

    @pl.when(j < ja)
    def _():
        _stage1_fwd([x_ref], m_ref, a_sc, j, pitch=pitch)

    @pl.when((j >= ja) & (j < ja + jb))
    def _():
        bigf = _ref_parts(bigf_ref)
        bigi = _ref_parts(bigi_ref)
        for g in range(DFT_STEP):
            k1 = (j - ja) * DFT_STEP + g
            z = _dot_parts(bigf, _split_bf16(_load_cplx(a_sc, k1, n2=n2, pitch=pitch)))
            zr, zi = z[0:n2], z[n2:2 * n2]
            kr = k_ref[g * 2 * n2:g * 2 * n2 + n2, :].astype(F32)
            ki = k_ref[g * 2 * n2 + n2:(g + 1) * 2 * n2, :].astype(F32)
            prod = jnp.concatenate([zr * kr - zi * ki, zr * ki + zi * kr], axis=0)
            b = _dot_parts(bigi, _split_bf16(prod))
            r0 = pl.multiple_of(k1 * 2 * pitch, 8)
            a_sc[pl.ds(r0, n2), :] = b[0:n2]
            a_sc[pl.ds(r0 + pitch, n2), :] = b[n2:2 * n2]

    @pl.when(j == ja + jb)
    def _():
        for r in range(pitch - n2):
            y_ref[pl.ds(n2 + r, DFT_N1, stride=pitch), :] = jnp.zeros((DFT_N1, y_ref.shape[1]), F32)

    @pl.when(j >= ja + jb)
    def _():
        skip = s_ref[...]
        for u in range(minv_ref.shape[1]):
            nn = (j - ja - jb) * minv_ref.shape[1] + u
            bs = a_sc[pl.ds(nn, 2 * DFT_N1, stride=pitch), :]
            conv = _dot_parts(_ref_parts(minv_ref, u), _split_bf16(bs))
            rows = pl.ds(nn, DFT_N1, stride=pitch)
            y_ref[rows, :] = g_ref[rows, :] * (conv + skip * x_ref[rows, :])


def _hyena_conv(x, x_col, gate, gate_col, skip, kspec, order, m_fwd, m_inv, big_fwd, big_inv):
    n2 = x.shape[0] // DFT_N1 - 8
    pitch = n2 + 8
    n = DFT_N1 * pitch
    n2g, ja, jb = _phase_steps(n2)
    wt = LANES
    c0 = x_col // wt
    cg = gate_col // wt
    ck = order * HY_WIDTH // wt
    kern = functools.partial(_hyena_conv_kernel, n2=n2, pitch=pitch, ja=ja, jb=jb)
    return pl.pallas_call(
        kern,
        out_shape=jax.ShapeDtypeStruct((n, HY_WIDTH), F32),
        grid=(HY_WIDTH // wt, ja + jb + ja),
        in_specs=[pl.BlockSpec((n, wt), lambda i, j: (0, c0 + i)),
                  pl.BlockSpec((n, wt), lambda i, j: (0, cg + i), pipeline_mode=pl.Buffered(1)),
                  pl.BlockSpec((1, wt), lambda i, j: (0, i)),
                  pl.BlockSpec((DFT_STEP * 2 * n2, wt), lambda i, j: (jnp.clip(j - ja, 0, jb - 1), ck + i)),
                  pl.BlockSpec((m_fwd.shape[0], n2g) + m_fwd.shape[2:],
                               lambda i, j: (0, jnp.minimum(j, ja - 1), 0, 0)),
                  pl.BlockSpec((m_inv.shape[0], n2g) + m_inv.shape[2:],
                               lambda i, j: (0, jnp.clip(j - ja - jb, 0, ja - 1), 0, 0)),
                  _const_spec(big_fwd.shape), _const_spec(big_inv.shape)],
        out_specs=pl.BlockSpec((n, wt), lambda i, j: (0, i), pipeline_mode=pl.Buffered(1)),
        scratch_shapes=[pltpu.VMEM((2 * DFT_N1 * pitch, wt), F32)],
        compiler_params=_params(("parallel", "arbitrary"), CONV_VMEM_LIMIT),
        name="hyena_conv",
    )(x, gate, skip, kspec, m_fwd, m_inv, big_fwd, big_inv)


def _fn_out_kernel(p_ref, q_ref, w_ref, o_ref, *, kb, norm):
    w = w_ref[...]
    for jj in range(kb):
        r = (p_ref[:, 0, jj, :] + q_ref[:, 0, jj, :]) * norm
        o_ref[jj] = jnp.dot(r.astype(BF16), w, preferred_element_type=F32)


def _fn_out(spec, fn_w_bf, n):
    n2 = n // DFT_N1
    c = FN_WIDTH
    kb = 8
    s4 = spec.reshape(DFT_N1, 2, n2, 2 * c)
    kern = functools.partial(_fn_out_kernel, kb=kb, norm=1.0 / math.sqrt(n * c))
    out = pl.pallas_call(
        kern,
        out_shape=jax.ShapeDtypeStruct((n2, DFT_N1, c), F32),
        grid=(n2 // kb,),
        in_specs=[pl.BlockSpec((DFT_N1, 1, kb, c), lambda i: (0, 0, i, 0)),
                  pl.BlockSpec((DFT_N1, 1, kb, c), lambda i: (0, 1, i, 1)),
                  _const_spec((c, c))],
        out_specs=pl.BlockSpec((kb, DFT_N1, c), lambda i: (i, 0, 0)),
        compiler_params=_params(("parallel",), VMEM_LIMIT),
        name="fn_out",
    )(s4, s4, fn_w_bf)
    return out.reshape(n, c)


def _odd_out_kernel(yc_ref, hyg_ref, yd_ref, fng_ref, x_ref, gate_ref, wo_ref, fg_ref, o_ref, *, run, pitch):
    yc = _load_pitched(yc_ref, run, pitch)
    y = jnp.dot((yc * hyg_ref[...]).astype(BF16), wo_ref[0:HY_WIDTH, :], preferred_element_type=F32)
    y = y + jnp.dot((yd_ref[...] * fng_ref[...]).astype(BF16), wo_ref[HY_WIDTH:, :],
                    preferred_element_type=F32)
    xo = x_ref[...] + gate_ref[...] * y
    o_ref[...] = xo * lax.rsqrt(jnp.mean(xo * xo, axis=-1, keepdims=True) + EPS) * fg_ref[...]


def _odd_out(yc, hy_gate, yd, fn_gate, x, gate, w_out_bf, final_g):
    n, d = x.shape
    tm = _row_tile(n, 512)
    run, pitch = n // DFT_N1, _time_pitch(n)
    row = lambda w: pl.BlockSpec((tm, w), lambda i: (i, 0))
    vec = lambda w: pl.BlockSpec((1, w), lambda i: (0, 0))
    kern = functools.partial(_odd_out_kernel, run=run, pitch=pitch)
    return pl.pallas_call(
        kern,
        out_shape=jax.ShapeDtypeStruct((n, d), F32),
        grid=(n // tm,),
        in_specs=[pl.BlockSpec((tm // run * pitch, HY_WIDTH), lambda i: (i, 0)),
                  row(HY_WIDTH), row(FN_WIDTH), row(FN_WIDTH), row(d), vec(d), _const_spec((d, d)), vec(d)],
        out_specs=row(d),
        compiler_params=_params(("parallel",), VMEM_LIMIT),
        name="odd_out",
    )(yc, hy_gate, yd, fn_gate, x, gate, w_out_bf, final_g)


def _rope_tables(n):
    rows = n // GRID_W
    inv_freq = ROPE_THETA ** (-np.arange(ROPE_FREQS, dtype=np.float64) / ROPE_FREQS)
    ang_r = np.arange(rows, dtype=np.float64)[:, None] * inv_freq
    ang_c = np.arange(GRID_W, dtype=np.float64)[:, None] * inv_freq

    def table(fn, sign):
        r = np.broadcast_to(fn(ang_r)[:, None, :], (rows, GRID_W, ROPE_FREQS))
        c = np.broadcast_to(fn(ang_c)[None, :, :], (rows, GRID_W, ROPE_FREQS))
        return jnp.asarray(np.concatenate([sign * r, r, sign * c, c], axis=-1).reshape(n, HEAD_DIM)
                           .astype(np.float32))

    return table(np.cos, 1.0), table(np.sin, -1.0)


def _hyena_embedding(n, pos):
    pos = pos.astype(np.float64)
    t = (pos / (n - 1))[:, None]
    w = 2.0 * np.pi * pos[:, None] / n
    f = np.linspace(1e-4, HY_BANDS - 1, HY_BANDS)[None, :]
    emb = np.concatenate([t, np.cos(f * w), -np.sin(f * w)], axis=-1)
    return np.pad(emb, ((0, 0), (0, HY_HIDDEN - HY_EMB)))


def kernel(x, c, ctx, c_ctx, w_mod, b_mod, norm_g, ev_w_in, ev_w_out, pool_w, pool_scale, q_norm_g, k_norm_g,
           od_w_in, od_w_out, hy_conv_w, hy_conv_b, hy_w1, hy_b1, hy_w2, hy_b2, hy_w3, hy_freq, hy_skip, fn_w,
           final_g):
    n, d = x.shape[1], x.shape[2]
    nc = ctx.shape[1]
    x0 = x[0]
    ctx0 = ctx[0]
    vec = lambda a: a.reshape(1, -1)

    cond = jnp.zeros((8, d), F32).at[0].set(c[0]).at[1].set(c_ctx)
    mod = _modulation(cond, w_mod, b_mod)
    shift0, scale0, gate0 = (mod[0, 0:1, k * d:(k + 1) * d] for k in range(3))
    cshift0, cscale0 = mod[0, 1:2, 0:d], mod[0, 1:2, d:2 * d]
    shift1, scale1, gate1 = (mod[1, 0:1, k * d:(k + 1) * d] for k in range(3))

    w_in0 = ev_w_in[0].astype(BF16)
    cos, sin = _rope_tables(n)
    g0 = vec(norm_g[0])
    qg, kg = vec(q_norm_g[0]), vec(k_norm_g[0])
    a_val, a_gate, q, k, v, b_gate = _even_in(x0, g0, scale0, shift0, w_in0, qg, kg, cos, sin)
    ones = jnp.ones((nc, HEAD_DIM), F32)
    _, _, _, ck, cv, _ = _even_in(ctx0, g0, cscale0, cshift0, w_in0, qg, kg, ones, jnp.zeros_like(ones))
    yb = _attention(q, jnp.concatenate([ck, k], axis=0), jnp.concatenate([cv, v], axis=0), b_gate)
    pool_bd = jax.scipy.linalg.block_diag(*[pool_w[0, gi] for gi in range(pool_w.shape[1])]).astype(BF16)
    x1 = _even_out(a_val, a_gate, yb, x0, gate0, pool_bd, vec(pool_scale[0]), ev_w_out[0].astype(BF16))

    ch = np.arange(FN_WIDTH, dtype=np.float64)
    ph = 2.0 * np.pi * np.outer(ch, ch) / FN_WIDTH
    cs = _table_parts(np.concatenate([np.cos(ph), np.sin(ph)], axis=1))
    uc, hy_gate, pq, fn_gate = _odd_in(x1, vec(norm_g[1]), scale1, shift1, od_w_in[0].astype(BF16),
                                       hy_conv_w[0], vec(hy_conv_b[0]), cs)

    max_decay = math.log(HY_DECAY_TARGET) / HY_FAST_DECAY
    min_decay = math.log(HY_DECAY_TARGET) / HY_SLOW_DECAY
    deltas = jnp.linspace(min_decay, max_decay, HY_WIDTH, dtype=F32)[None, :]
    pos = np.arange(n)
    emb2 = jnp.asarray(np.concatenate([_hyena_embedding(n, pos),
                                       _hyena_embedding(n, np.where(pos == 0, 0, n - pos))], axis=-1)
                       .astype(np.float32))
    k_lo, k_hi, mass = _hyena_filters(emb2, hy_w1[0], vec(hy_b1[0]), hy_w2[0], vec(hy_b2[0]), hy_w3[0],
                                      vec(hy_freq[0]), deltas)

    n2 = n // DFT_N1
    m_full, _, big_fwd, big_inv = _dft_tables(2 * n, n2, True, 2 * DFT_N1)
    m_half, m_inv, _, _ = _dft_tables(2 * n, n2, True, DFT_N1)
    k_spec = _dft_fwd([k_lo, k_hi], HY_ORDER * HY_WIDTH, m_full, big_fwd, mass=mass, out_dtype=BF16)
    z1 = _hyena_conv(uc, 0, uc, HY_WIDTH, hy_skip[0, 0:1], k_spec, 0, m_half, m_inv, big_fwd, big_inv)
    yc = _hyena_conv(z1, 0, uc, 2 * HY_WIDTH, hy_skip[0, 1:2], k_spec, 1, m_half, m_inv, big_fwd, big_inv)

    f_half, _, f_big, _ = _dft_tables(n, n2, False, DFT_N1)
    pq_spec = _dft_fwd([pq], 2 * FN_WIDTH, f_half, f_big)
    yd = _fn_out(pq_spec, fn_w[0].astype(BF16), n)

    out = _odd_out(yc, hy_gate, yd, fn_gate, x1, gate1, od_w_out[0].astype(BF16), vec(final_g))
    return out[None]
```

```python
import functools
import math

import numpy as np
import jax
import jax.numpy as jnp
from jax import lax
from jax.experimental import pallas as pl
from jax.experimental.pallas import tpu as pltpu

F32 = jnp.float32
BF16 = jnp.bfloat16
HIGHEST = lax.Precision.HIGHEST

EPS = 1e-6
GRID_W = 64
HEAD_DIM = 128
ROPE_FREQS = 32
ROPE_THETA = 10000.0
N_Q_HEADS = 6
N_KV_HEADS = 2
Q_PER_KV = N_Q_HEADS // N_KV_HEADS
POOL_WIDTH = 256
POOL_GROUP_DIM = 64
POOL_WINDOWS = (2, 4, 8, 16)
POOL_HALO = 8
ATT_WIDTH = N_Q_HEADS * HEAD_DIM
KV_WIDTH = N_KV_HEADS * HEAD_DIM
HY_WIDTH = 768
HY_ORDER = 2
HY_EMB = 33
HY_BANDS = 16
HY_HIDDEN = 64
FN_WIDTH = 256
HY_DECAY_TARGET = 1e-2
HY_FAST_DECAY = 0.3
HY_SLOW_DECAY = 1.5

LANES = 128
DFT_N1 = 128
VMEM_LIMIT = 56 * 1024 * 1024
CONV_VMEM_LIMIT = 62 * 1024 * 1024


def _row_tile(n, pref):
    t = min(pref, n)
    assert n % t == 0
    return t


def _silu(x):
    return x * jax.nn.sigmoid(x)


def _params(sem, vmem=None, flags=None):
    return pltpu.CompilerParams(dimension_semantics=sem, vmem_limit_bytes=vmem, flags=flags)


def _const_spec(shape):
    nd = len(shape)
    return pl.BlockSpec(shape, lambda *_: (0,) * nd, pipeline_mode=pl.Buffered(1))


def _mod_kernel(cond_ref, w_ref, b_ref, o_ref):
    s = _silu(cond_ref[...])
    o_ref[0] = jnp.dot(s, w_ref[0], precision=HIGHEST, preferred_element_type=F32) + b_ref[0]


def _modulation(cond, w_mod, b_mod):
    depth, d, d3 = w_mod.shape
    tn = 1024
    return pl.pallas_call(
        _mod_kernel,
        out_shape=jax.ShapeDtypeStruct((depth, 8, d3), F32),
        grid=(depth, d3 // tn),
        in_specs=[pl.BlockSpec((8, d), lambda i, j: (0, 0)),
                  pl.BlockSpec((1, d, tn), lambda i, j: (i, 0, j)),
                  pl.BlockSpec((1, 1, tn), lambda i, j: (i, 0, j))],
        out_specs=pl.BlockSpec((1, 8, tn), lambda i, j: (i, 0, j)),
        compiler_params=_params(("arbitrary", "arbitrary")),
        name="modulation",
    )(cond, w_mod, b_mod.reshape(depth, 1, d3))


def _time_pitch(n):
    return n // DFT_N1 + 8


def _store_pitched(ref, cols, val, run, pitch, first=0):
    for b in range(val.shape[0] // run):
        r0 = (first + b) * pitch
        ref[r0:r0 + run, cols] = val[b * run:(b + 1) * run]
        ref[r0 + run:r0 + pitch, cols] = jnp.zeros((pitch - run, val.shape[1]), val.dtype)


def _load_pitched(ref, run, pitch):
    return jnp.concatenate([ref[b * pitch:b * pitch + run, :] for b in range(ref.shape[0] // pitch)], axis=0)


def _norm_mod(x, g, scale, shift):
    y = x * lax.rsqrt(jnp.mean(x * x, axis=-1, keepdims=True) + EPS)
    return (y * g) * (1.0 + scale) + shift


def _even_in_kernel(x_ref, g_ref, sc_ref, sh_ref, w_ref, qg_ref, kg_ref, cos_ref, sin_ref,
                    aval_ref, agate_ref, q_ref, k_ref, v_ref, bgate_ref):
    half = x_ref.shape[0] // 2
    for s in range(2):
        rs = slice(s * half, (s + 1) * half)
        h = _norm_mod(x_ref[rs, :], g_ref[...], sc_ref[...], sh_ref[...])
        p = jnp.dot(h.astype(BF16), w_ref[...], preferred_element_type=F32)
        aval_ref[rs, :] = p[:, 0:POOL_WIDTH]
        agate_ref[rs, :] = _silu(p[:, POOL_WIDTH:2 * POOL_WIDTH])
        cos = cos_ref[rs, :]
        sin = sin_ref[rs, :]
        lane = lax.broadcasted_iota(jnp.int32, cos.shape, 1)
        low_half = (lane % (2 * ROPE_FREQS)) < ROPE_FREQS

        def head(xh, g):
            y = xh * lax.rsqrt(jnp.mean(xh * xh, axis=-1, keepdims=True) + EPS) * g
            rot = jnp.where(low_half, pltpu.roll(y, HEAD_DIM - ROPE_FREQS, 1), pltpu.roll(y, ROPE_FREQS, 1))
            return y * cos + rot * sin

        q0 = 2 * POOL_WIDTH
        for hq in range(N_Q_HEADS):
            sl = slice(hq * HEAD_DIM, (hq + 1) * HEAD_DIM)
            q_ref[rs, sl] = head(p[:, q0 + hq * HEAD_DIM:q0 + (hq + 1) * HEAD_DIM], qg_ref[...]).astype(BF16)
        k0 = q0 + ATT_WIDTH
        for hk in range(N_KV_HEADS):
            sl = slice(hk * HEAD_DIM, (hk + 1) * HEAD_DIM)
            k_ref[rs, sl] = head(p[:, k0 + hk * HEAD_DIM:k0 + (hk + 1) * HEAD_DIM], kg_ref[...]).astype(BF16)
        v0 = k0 + KV_WIDTH
        v_ref[rs, :] = p[:, v0:v0 + KV_WIDTH].astype(BF16)
        b0 = v0 + KV_WIDTH
        bgate_ref[rs, :] = _silu(p[:, b0:b0 + ATT_WIDTH]).astype(bgate_ref.dtype)


def _even_in(x, g, scale, shift, w_bf, qg, kg, cos, sin):
    n, d = x.shape
    n_in = w_bf.shape[1]
    tm = _row_tile(n, 512)
    row = lambda w: pl.BlockSpec((tm, w), lambda i: (i, 0))
    vec = lambda w: pl.BlockSpec((1, w), lambda i: (0, 0))
    return pl.pallas_call(
        _even_in_kernel,
        out_shape=(jax.ShapeDtypeStruct((n, POOL_WIDTH), F32), jax.ShapeDtypeStruct((n, POOL_WIDTH), F32),
                   jax.ShapeDtypeStruct((n, ATT_WIDTH), BF16), jax.ShapeDtypeStruct((n, KV_WIDTH), BF16),
                   jax.ShapeDtypeStruct((n, KV_WIDTH), BF16), jax.ShapeDtypeStruct((n, ATT_WIDTH), BF16)),
        grid=(n // tm,),
        in_specs=[row(d), vec(d), vec(d), vec(d), _const_spec((d, n_in)), vec(HEAD_DIM), vec(HEAD_DIM),
                  row(HEAD_DIM), row(HEAD_DIM)],
        out_specs=(row(POOL_WIDTH), row(POOL_WIDTH), row(ATT_WIDTH), row(KV_WIDTH), row(KV_WIDTH),
                   row(ATT_WIDTH)),
        compiler_params=_params(("parallel",), VMEM_LIMIT),
        name="even_in",
    )(x, g, scale, shift, w_bf, qg, kg, cos, sin)


ATT_UNROLL = 4
ATT_ROW_CHUNK = 16


def _attn_kernel(q_ref, k_ref, v_ref, bg_ref, o_ref, s_sc, p_sc, m_sc, al_sc, acc_sc, *, tq, tk, n_kv_tiles):
    c = (HEAD_DIM ** -0.5) * math.log2(math.e)
    rows = Q_PER_KV * tq
    for h in range(N_KV_HEADS):
        hs = slice(h * HEAD_DIM, (h + 1) * HEAD_DIM)
        qs = jnp.concatenate(
            [q_ref[:, (Q_PER_KV * h + g) * HEAD_DIM:(Q_PER_KV * h + g + 1) * HEAD_DIM] for g in range(Q_PER_KV)],
            axis=0)

        def scores(buf, t):
            kt = k_ref[pl.ds(t * tk if isinstance(t, int) else pl.multiple_of(t * tk, tk), tk), hs]
            s_sc[buf] = lax.dot_general(qs, kt, (((1,), (1,)), ((), ())), preferred_element_type=F32)

        def softmax(buf):
            for r in range(rows // ATT_ROW_CHUNK):
                rs = slice(r * ATT_ROW_CHUNK, (r + 1) * ATT_ROW_CHUNK)
                cols = [s_sc[buf, rs, j * LANES:(j + 1) * LANES] for j in range(tk // LANES)]
                mx = functools.reduce(jnp.maximum, cols)
                m_old = m_sc[rs, :]
                m_new = jnp.maximum(m_old, jnp.max(mx, axis=-1, keepdims=True) * c)
                for j, sj in enumerate(cols):
                    p_sc[buf, rs, j * LANES:(j + 1) * LANES] = jnp.exp2(sj * c - m_new).astype(BF16)
                al_sc[buf, rs, :] = jnp.exp2(m_old - m_new)
                m_sc[rs, :] = m_new

        def accumulate(buf, t):
            vt = v_ref[pl.ds(t * tk if isinstance(t, int) else pl.multiple_of(t * tk, tk), tk), hs]
            v_ext = jnp.concatenate([vt, jnp.ones((tk, HEAD_DIM), BF16)], axis=1)
            pv = jnp.dot(p_sc[buf], v_ext, preferred_element_type=F32)
            al = al_sc[buf]
            for j in range(2):
                js = slice(j * HEAD_DIM, (j + 1) * HEAD_DIM)
                acc_sc[:, js] = al * acc_sc[:, js] + pv[:, js]

        def slot(t, buf, with_scores=True):
            if with_scores:
                scores(1 - buf, t + 1)
            softmax(buf)
            accumulate(buf, t)

        m_sc[...] = jnp.full(m_sc.shape, -1e30, F32)
        acc_sc[...] = jnp.zeros(acc_sc.shape, F32)
        scores(0, 0)

        def body(i, carry):
            for u in range(ATT_UNROLL):
                slot(ATT_UNROLL * i + u, u % 2)
            return carry

        n_full = (n_kv_tiles - 1) // ATT_UNROLL
        lax.fori_loop(0, n_full, body, 0)
        for t in range(ATT_UNROLL * n_full, n_kv_tiles):
            slot(t, t % 2, with_scores=t + 1 < n_kv_tiles)
        out = acc_sc[:, 0:HEAD_DIM] / acc_sc[:, HEAD_DIM:2 * HEAD_DIM]
        for g in range(Q_PER_KV):
            cs = slice((Q_PER_KV * h + g) * HEAD_DIM, (Q_PER_KV * h + g + 1) * HEAD_DIM)
            o_ref[:, cs] = (out[g * tq:(g + 1) * tq] * bg_ref[:, cs]).astype(o_ref.dtype)


ATT_KEY_TILE = 1280


def _attention(q, k_all, v_all, bgate):
    n = q.shape[0]
    nk = k_all.shape[0]
    tq = _row_tile(n, 256)
    tk = max(t for t in range(LANES, ATT_KEY_TILE + 1, LANES) if nk % t == 0)
    rows = Q_PER_KV * tq
    kern = functools.partial(_attn_kernel, tq=tq, tk=tk, n_kv_tiles=nk // tk)
    return pl.pallas_call(
        kern,
        out_shape=jax.ShapeDtypeStruct((n, ATT_WIDTH), BF16),
        grid=(n // tq,),
        in_specs=[pl.BlockSpec((tq, ATT_WIDTH), lambda i: (i, 0)),
                  _const_spec((nk, KV_WIDTH)), _const_spec((nk, KV_WIDTH)),
                  pl.BlockSpec((tq, ATT_WIDTH), lambda i: (i, 0))],
        out_specs=pl.BlockSpec((tq, ATT_WIDTH), lambda i: (i, 0)),
        scratch_shapes=[pltpu.VMEM((2, rows, tk), F32), pltpu.VMEM((2, rows, tk), BF16),
                        pltpu.VMEM((rows, LANES), F32), pltpu.VMEM((2, rows, LANES), F32),
                        pltpu.VMEM((rows, 2 * HEAD_DIM), F32)],
        compiler_params=_params(("parallel",), VMEM_LIMIT),
        name="attention",
    )(q, k_all, v_all, bgate)


def _even_out_kernel(a_ref, ap_ref, an_ref, ag_ref, yb_ref, x_ref, gate_ref, pw_ref, ps_ref, wo_ref, o_ref,
                     *, tm, n_rows):
    i = pl.program_id(0)
    n_tiles = pl.num_programs(0)
    u = a_ref[...]
    prev = jnp.where(i > 0, ap_ref[...], 0.0)
    nxt = jnp.where(i < n_tiles - 1, an_ref[...], 0.0)
    e = jnp.concatenate([prev, u, nxt], axis=0)
    n = tm + 2 * POOL_HALO
    s2 = e[0:n - 1] + e[1:n]
    s4 = s2[0:n - 3] + s2[2:n - 1]
    s8 = s4[0:n - 7] + s4[4:n - 3]
    s16 = s8[0:n - 15] + s8[8:n - 7]
    lane = lax.broadcasted_iota(jnp.int32, (tm, POOL_WIDTH), 1)
    grp = lane // POOL_GROUP_DIM
    win = jnp.where(grp == 0, s2[7:7 + tm],
                    jnp.where(grp == 1, s4[6:6 + tm], jnp.where(grp == 2, s8[4:4 + tm], s16[0:tm])))
    half = jnp.where(grp == 0, 1, jnp.where(grp == 1, 2, jnp.where(grp == 2, 4, 8)))
    t = i * tm + lax.broadcasted_iota(jnp.int32, (tm, POOL_WIDTH), 0)
    cnt = (jnp.minimum(t + half, n_rows) - jnp.maximum(t - half, 0)).astype(F32)
    d = win / cnt - u
    ya = jnp.dot(d.astype(BF16), pw_ref[...], preferred_element_type=F32) * ps_ref[...]
    ya = ya * ag_ref[...]
    y = jnp.dot(ya.astype(BF16), wo_ref[0:POOL_WIDTH, :], preferred_element_type=F32)
    y = y + jnp.dot(yb_ref[...].astype(BF16), wo_ref[POOL_WIDTH:, :], preferred_element_type=F32)
    o_ref[...] = x_ref[...] + gate_ref[...] * y


def _even_out(a_val, a_gate, yb, x, gate, pool_bd, pool_scale, w_out_bf):
    n, d = x.shape
    tm = _row_tile(n, 512)
    hb = tm // POOL_HALO
    last = n // POOL_HALO - 1
    row = lambda w: pl.BlockSpec((tm, w), lambda i: (i, 0))
    vec = lambda w: pl.BlockSpec((1, w), lambda i: (0, 0))
    kern = functools.partial(_even_out_kernel, tm=tm, n_rows=n)
    return pl.pallas_call(
        kern,
        out_shape=jax.ShapeDtypeStruct((n, d), F32),
        grid=(n // tm,),
        in_specs=[row(POOL_WIDTH),
                  pl.BlockSpec((POOL_HALO, POOL_WIDTH), lambda i: (jnp.maximum(i * hb - 1, 0), 0)),
                  pl.BlockSpec((POOL_HALO, POOL_WIDTH), lambda i: (jnp.minimum((i + 1) * hb, last), 0)),
                  row(POOL_WIDTH), row(ATT_WIDTH), row(d), vec(d),
                  _const_spec((POOL_WIDTH, POOL_WIDTH)), vec(POOL_WIDTH), _const_spec((d, d))],
        out_specs=row(d),
        compiler_params=_params(("parallel",), VMEM_LIMIT),
        name="even_out",
    )(a_val, a_val, a_val, a_gate, yb, x, gate, pool_bd, pool_scale, w_out_bf)


def _odd_in_kernel(x_ref, xp_ref, xn_ref, g_ref, sc_ref, sh_ref, w_ref, cw_ref, cb_ref, cs_ref,
                   uc_ref, hyg_ref, pq_ref, fng_ref, *, tm, run, pitch):
    i = pl.program_id(0)
    n_tiles = pl.num_programs(0)
    g, sc, sh = g_ref[...], sc_ref[...], sh_ref[...]
    c0 = (HY_ORDER + 1) * HY_WIDTH
    halo = _norm_mod(jnp.concatenate([xp_ref[...], xn_ref[...]], axis=0), g, sc, sh)
    ph = jnp.dot(halo.astype(BF16), w_ref[:, 0:c0], preferred_element_type=F32)
    half = tm // 2
    ps = [jnp.dot(_norm_mod(x_ref[s * half:(s + 1) * half, :], g, sc, sh).astype(BF16), w_ref[...],
                  preferred_element_type=F32) for s in range(2)]
    edge = [jnp.where(i > 0, ph[7:8, :], 0.0), ps[0][half - 1:half, 0:c0], ps[1][0:1, 0:c0],
            jnp.where(i < n_tiles - 1, ph[8:9, :], 0.0)]
    for s, p in enumerate(ps):
        rs = slice(s * half, (s + 1) * half)
        u = p[:, 0:c0]
        e = jnp.concatenate([edge[s], u, edge[s + 2]], axis=0)
        uc = e[0:half] * cw_ref[0:1, :] + u * cw_ref[1:2, :] + e[2:half + 2] * cw_ref[2:3, :] + cb_ref[...]
        _store_pitched(uc_ref, slice(None), uc, run, pitch, first=s * (half // run))
        hyg_ref[rs, :] = _silu(p[:, c0:c0 + HY_WIDTH]).astype(hyg_ref.dtype)
        fn_in = p[:, c0 + HY_WIDTH:c0 + HY_WIDTH + FN_WIDTH]
        _store_pitched(pq_ref, slice(None), _dot_parts(_split_bf16(fn_in), _ref_parts(cs_ref)), run, pitch,
                       first=s * (half // run))
        fng_ref[rs, :] = _silu(p[:, c0 + HY_WIDTH + FN_WIDTH:])


def _odd_in(x, g, scale, shift, w_bf, conv_w, conv_b, cs):
    n, d = x.shape
    n_in = w_bf.shape[1]
    tm = _row_tile(n, 512)
    hb = tm // 8
    last = n // 8 - 1
    row = lambda w: pl.BlockSpec((tm, w), lambda i: (i, 0))
    vec = lambda w: pl.BlockSpec((1, w), lambda i: (0, 0))
    c0 = (HY_ORDER + 1) * HY_WIDTH
    run, pitch = n // DFT_N1, _time_pitch(n)
    assert tm % run == 0
    tmp = tm // run * pitch
    prow = lambda w: pl.BlockSpec((tmp, w), lambda i: (i, 0))
    kern = functools.partial(_odd_in_kernel, tm=tm, run=run, pitch=pitch)
    return pl.pallas_call(
        kern,
        out_shape=(jax.ShapeDtypeStruct((DFT_N1 * pitch, c0), F32), jax.ShapeDtypeStruct((n, HY_WIDTH), BF16),
                   jax.ShapeDtypeStruct((DFT_N1 * pitch, 2 * FN_WIDTH), F32),
                   jax.ShapeDtypeStruct((n, FN_WIDTH), F32)),
        grid=(n // tm,),
        in_specs=[row(d),
                  pl.BlockSpec((8, d), lambda i: (jnp.maximum(i * hb - 1, 0), 0)),
                  pl.BlockSpec((8, d), lambda i: (jnp.minimum((i + 1) * hb, last), 0)),
                  vec(d), vec(d), vec(d), _const_spec((d, n_in)),
                  pl.BlockSpec((3, c0), lambda i: (0, 0)), vec(c0), _const_spec(cs.shape)],
        out_specs=(prow(c0), row(HY_WIDTH), prow(2 * FN_WIDTH), row(FN_WIDTH)),
        compiler_params=_params(("parallel",), VMEM_LIMIT),
        name="odd_in",
    )(x, x, x, g, scale, shift, w_bf, conv_w, conv_b, cs)


def _filter_kernel(emb_ref, w1_ref, b1_ref, w2_ref, b2_ref, w3f_ref, w3b_ref, fr_ref, dl_ref,
                   lo_ref, hi_ref, mass_ref, *, tm, n_rows, run, pitch):
    i = pl.program_id(0)
    fr = fr_ref[...]
    a = jnp.sin(fr * (jnp.dot(emb_ref[...], w1_ref[...], precision=HIGHEST, preferred_element_type=F32)
                      + b1_ref[...]))
    a = jnp.sin(fr * (jnp.dot(a, w2_ref[...], precision=HIGHEST, preferred_element_type=F32) + b2_ref[...]))
    a_f = a_b = a.astype(BF16)
    j = i * tm + lax.broadcasted_iota(jnp.int32, (tm, 1), 0)
    pos_b = jnp.where(j == 0, 0, n_rows - j)
    dl = jnp.abs(dl_ref[...])
    inv = 1.0 / (n_rows - 1)
    decay_f = jnp.exp(-(j.astype(F32) * inv) * dl)
    decay_b = jnp.exp(-(pos_b.astype(F32) * inv) * dl)

    @pl.when(i == 0)
    def _():
        mass_ref[...] = jnp.zeros_like(mass_ref)

    for o in range(HY_ORDER):
        os_ = slice(o * HY_WIDTH, (o + 1) * HY_WIDTH)
        hf = jnp.dot(a_f, w3f_ref[:, os_], preferred_element_type=F32) * decay_f
        hb = jnp.dot(a_b, w3b_ref[:, os_], preferred_element_type=F32) * decay_b
        _store_pitched(lo_ref, os_, hf, run, pitch)
        _store_pitched(hi_ref, os_, jnp.where(j == 0, 0.0, -hb), run, pitch)
        mass_ref[:, os_] += (jnp.sum(jnp.abs(hf), axis=0, keepdims=True)
                             + jnp.sum(jnp.abs(hb), axis=0, keepdims=True))


def _hyena_filters(emb2, w1, b1, w2, b2, w3, freq, deltas):
    n = emb2.shape[0]
    tm = _row_tile(n, 512)
    nh = HY_ORDER * HY_WIDTH
    hh = HY_HIDDEN
    w1b = jnp.zeros((LANES, 2 * hh), F32).at[0:HY_EMB, 0:hh].set(w1).at[hh:hh + HY_EMB, hh:].set(w1)
    w2b = jnp.zeros((2 * hh, 2 * hh), F32).at[0:hh, 0:hh].set(w2).at[hh:, hh:].set(w2)
    w3r = w3.reshape(hh, HY_ORDER, 2, HY_WIDTH)
    zero = jnp.zeros((hh, nh), F32)
    w3f = jnp.concatenate([w3r[:, :, 0, :].reshape(hh, nh), zero], axis=0).astype(BF16)
    w3b = jnp.concatenate([zero, w3r[:, :, 1, :].reshape(hh, nh)], axis=0).astype(BF16)
    twice = lambda a: jnp.concatenate([a, a], axis=-1)
    ops = (emb2, w1b, twice(b1), w2b, twice(b2), w3f, w3b, twice(freq), deltas)
    run, pitch = n // DFT_N1, _time_pitch(n)
    assert tm % run == 0
    prow = pl.BlockSpec((tm // run * pitch, nh), lambda i: (i, 0))
    kern = functools.partial(_filter_kernel, tm=tm, n_rows=n, run=run, pitch=pitch)
    full = lambda a: pl.BlockSpec(a.shape, lambda i: (0,) * a.ndim)
    row = lambda w: pl.BlockSpec((tm, w), lambda i: (i, 0))
    return pl.pallas_call(
        kern,
        out_shape=(jax.ShapeDtypeStruct((DFT_N1 * pitch, nh), F32), jax.ShapeDtypeStruct((DFT_N1 * pitch, nh), F32),
                   jax.ShapeDtypeStruct((1, nh), F32)),
        grid=(n // tm,),
        in_specs=[row(LANES)] + [full(a) for a in ops[1:]],
        out_specs=(prow, prow, pl.BlockSpec((1, nh), lambda i: (0, 0))),
        compiler_params=_params(("arbitrary",), VMEM_LIMIT),
        name="hyena_filters",
    )(*ops)


DFT_PARTS = 1


def _table_parts(a):
    a32 = jnp.asarray(a.astype(np.float32))
    hi = a32.astype(BF16)
    if DFT_PARTS == 1:
        return hi[None]
    lo = (a32 - hi.astype(F32)).astype(BF16)
    return jnp.stack([hi, lo])


def _split_bf16(x):
    hi = x.astype(BF16)
    if DFT_PARTS == 1:
        return (hi,)
    return (hi, (x - hi.astype(F32)).astype(BF16))


def _ref_parts(ref, *idx):
    return tuple(ref[(q,) + idx] for q in range(ref.shape[0]))


def _dot_parts(lhs, rhs):
    acc = jnp.dot(lhs[0], rhs[0], preferred_element_type=F32)
    if len(lhs) > 1:
        acc = acc + jnp.dot(lhs[1], rhs[0], preferred_element_type=F32)
    if len(rhs) > 1:
        acc = acc + jnp.dot(lhs[0], rhs[1], preferred_element_type=F32)
    return acc


def _dft_tables(n_total, n2, half_shift, n1_used):
    n1_full = n_total // n2
    sh = 0.5 if half_shift else 0.0
    k1 = np.arange(DFT_N1, dtype=np.float64)[None, :, None] + sh
    n1 = np.arange(n1_used, dtype=np.float64)[None, None, :]
    nn2 = np.arange(n2, dtype=np.float64)[:, None, None]
    ph = -2.0 * np.pi * (n1 * k1 / n1_full + nn2 * k1 / n_total)
    fwd = np.stack([np.cos(ph), np.sin(ph)], axis=2).reshape(n2, 2 * DFT_N1, n1_used)
    ph_i = ph[:, :, :DFT_N1]
    inv = np.stack([np.cos(ph_i), np.sin(ph_i)], axis=2).reshape(n2, 2 * DFT_N1, DFT_N1)
    inv = (2.0 / n_total) * inv.transpose(0, 2, 1)
    k2 = np.arange(n2, dtype=np.float64)
    ph2 = -2.0 * np.pi * np.outer(k2, k2) / n2
    fr, fi = np.cos(ph2), np.sin(ph2)
    big_fwd = np.block([[fr, -fi], [fi, fr]])
    big_inv = np.block([[fr, fi], [-fi, fr]])
    return _table_parts(fwd), _table_parts(inv), _table_parts(big_fwd), _table_parts(big_inv)


DFT_STEP = 32


def _stage1_fwd(x_refs, m_ref, a_sc, step, *, pitch):
    for u in range(m_ref.shape[1]):
        nn = step * m_ref.shape[1] + u
        xs = [xr[pl.ds(nn, DFT_N1, stride=pitch), :] for xr in x_refs]
        xs = xs[0] if len(xs) == 1 else jnp.concatenate(xs, axis=0)
        a_sc[pl.ds(nn, 2 * DFT_N1, stride=pitch), :] = _dot_parts(_ref_parts(m_ref, u), _split_bf16(xs))


def _load_cplx(a_sc, k1, *, n2, pitch):
    r0 = pl.multiple_of(k1 * 2 * pitch, 8)
    return jnp.concatenate([a_sc[pl.ds(r0, n2), :], a_sc[pl.ds(r0 + pitch, n2), :]], axis=0)


def _phase_steps(n2):
    n2g = min(DFT_STEP, n2)
    return n2g, n2 // n2g, DFT_N1 // DFT_STEP


def _dft_fwd_kernel(*refs, n_x, n2, pitch, ja, scaled):
    x_refs = refs[:n_x]
    rest = refs[n_x:]
    if scaled:
        mass_ref, rest = rest[0], rest[1:]
    m_ref, big_ref, o_ref, a_sc = rest
    j = pl.program_id(1)

    @pl.when(j < ja)
    def _():
        _stage1_fwd(x_refs, m_ref, a_sc, j, pitch=pitch)

    @pl.when(j >= ja)
    def _():
        big = _ref_parts(big_ref)
        for g in range(DFT_STEP):
            blk = _load_cplx(a_sc, (j - ja) * DFT_STEP + g, n2=n2, pitch=pitch)
            out = _dot_parts(big, _split_bf16(blk))
            if scaled:
                out = out * (1.0 / (mass_ref[...] + EPS))
            o_ref[g * 2 * n2:(g + 1) * 2 * n2, :] = out.astype(o_ref.dtype)


def _dft_fwd(xs, width, m_fwd, big, mass=None, out_dtype=F32):
    n2 = xs[0].shape[0] // DFT_N1 - 8
    pitch = n2 + 8
    n = DFT_N1 * pitch
    n2g, ja, jb = _phase_steps(n2)
    wt = LANES
    rows = 2 * DFT_N1 * n2
    kern = functools.partial(_dft_fwd_kernel, n_x=len(xs), n2=n2, pitch=pitch, ja=ja, scaled=mass is not None)
    x_spec = pl.BlockSpec((n, wt), lambda i, j: (0, i), pipeline_mode=pl.Buffered(1))
    ops = list(xs)
    specs = [x_spec] * len(xs)
    if mass is not None:
        ops.append(mass)
        specs.append(pl.BlockSpec((1, wt), lambda i, j: (0, i)))
    m_spec = pl.BlockSpec((m_fwd.shape[0], n2g) + m_fwd.shape[2:],
                          lambda i, j: (0, jnp.minimum(j, ja - 1), 0, 0))
    return pl.pallas_call(
        kern,
        out_shape=jax.ShapeDtypeStruct((rows, width), out_dtype),
        grid=(width // wt, ja + jb),
        in_specs=specs + [m_spec, _const_spec(big.shape)],
        out_specs=pl.BlockSpec((DFT_STEP * 2 * n2, wt), lambda i, j: (jnp.maximum(j - ja, 0), i)),
        scratch_shapes=[pltpu.VMEM((2 * DFT_N1 * pitch, wt), F32)],
        compiler_params=_params(("parallel", "arbitrary"), VMEM_LIMIT),
        name="dft_fwd",
    )(*ops, m_fwd, big)


def _hyena_conv_kernel(x_ref, g_ref, s_ref, k_ref, m_ref, minv_ref, bigf_ref, bigi_ref, y_ref, a_sc,
                       *, n2, pitch, ja, jb):
    j = pl.program_id(1)

    @pl.when(j < ja)
    def _():
        _stage1_fwd([x_ref], m_ref, a_sc, j, pitch=pitch)

    @pl.when((j >= ja) & (j < ja + jb))
    def _():
        bigf = _ref_parts(bigf_ref)
        bigi = _ref_parts(bigi_ref)
        for g in range(DFT_STEP):
            k1 = (j - ja) * DFT_STEP + g
            z = _dot_parts(bigf, _split_bf16(_load_cplx(a_sc, k1, n2=n2, pitch=pitch)))
            zr, zi = z[0:n2], z[n2:2 * n2]
            kr = k_ref[g * 2 * n2:g * 2 * n2 + n2, :].astype(F32)
            ki = k_ref[g * 2 * n2 + n2:(g + 1) * 2 * n2, :].astype(F32)
            prod = jnp.concatenate([zr * kr - zi * ki, zr * ki + zi * kr], axis=0)
            b = _dot_parts(bigi, _split_bf16(prod))
            r0 = pl.multiple_of(k1 * 2 * pitch, 8)
            a_sc[pl.ds(r0, n2), :] = b[0:n2]
            a_sc[pl.ds(r0 + pitch, n2), :] = b[n2:2 * n2]

    @pl.when(j == ja + jb)
    def _():
        for r in range(pitch - n2):
            y_ref[pl.ds(n2 + r, DFT_N1, stride=pitch), :] = jnp.zeros((DFT_N1, y_ref.shape[1]), F32)

    @pl.when(j >= ja + jb)
    def _():
        skip = s_ref[...]
        for u in range(minv_ref.shape[1]):
            nn = (j - ja - jb) * minv_ref.shape[1] + u
            bs = a_sc[pl.ds(nn, 2 * DFT_N1, stride=pitch), :]
            conv = _dot_parts(_ref_parts(minv_ref, u), _split_bf16(bs))
            rows = pl.ds(nn, DFT_N1, stride=pitch)
            y_ref[rows, :] = g_ref[rows, :] * (conv + skip * x_ref[rows, :])


def _hyena_conv(x, x_col, gate, gate_col, skip, kspec, order, m_fwd, m_inv, big_fwd, big_inv):
    n2 = x.shape[0] // DFT_N1 - 8
    pitch = n2 + 8
    n = DFT_N1 * pitch
    n2g, ja, jb = _phase_steps(n2)
    wt = LANES
    c0 = x_col // wt
    cg = gate_col // wt
    ck = order * HY_WIDTH // wt
    kern = functools.partial(_hyena_conv_kernel, n2=n2, pitch=pitch, ja=ja, jb=jb)
    return pl.pallas_call(
        kern,
        out_shape=jax.ShapeDtypeStruct((n, HY_WIDTH), F32),
        grid=(HY_WIDTH // wt, ja + jb + ja),
        in_specs=[pl.BlockSpec((n, wt), lambda i, j: (0, c0 + i), pipeline_mode=pl.Buffered(1)),
                  pl.BlockSpec((n, wt), lambda i, j: (0, cg + i), pipeline_mode=pl.Buffered(1)),
                  pl.BlockSpec((1, wt), lambda i, j: (0, i)),
                  pl.BlockSpec((DFT_STEP * 2 * n2, wt), lambda i, j: (jnp.clip(j - ja, 0, jb - 1), ck + i)),
                  pl.BlockSpec((m_fwd.shape[0], n2g) + m_fwd.shape[2:],
                               lambda i, j: (0, jnp.minimum(j, ja - 1), 0, 0)),
                  pl.BlockSpec((m_inv.shape[0], n2g) + m_inv.shape[2:],
                               lambda i, j: (0, jnp.clip(j - ja - jb, 0, ja - 1), 0, 0)),
                  _const_spec(big_fwd.shape), _const_spec(big_inv.shape)],
        out_specs=pl.BlockSpec((n, wt), lambda i, j: (0, i), pipeline_mode=pl.Buffered(1)),
        scratch_shapes=[pltpu.VMEM((2 * DFT_N1 * pitch, wt), F32)],
        compiler_params=_params(("parallel", "arbitrary"), CONV_VMEM_LIMIT),
        name="hyena_conv",
    )(x, gate, skip, kspec, m_fwd, m_inv, big_fwd, big_inv)


def _fn_out_kernel(p_ref, q_ref, w_ref, o_ref, *, kb, norm):
    w = w_ref[...]
    for jj in range(kb):
        r = (p_ref[:, 0, jj, :] + q_ref[:, 0, jj, :]) * norm
        o_ref[jj] = jnp.dot(r.astype(BF16), w, preferred_element_type=F32)


def _fn_out(spec, fn_w_bf, n):
    n2 = n // DFT_N1
    c = FN_WIDTH
    kb = 8
    s4 = spec.reshape(DFT_N1, 2, n2, 2 * c)
    kern = functools.partial(_fn_out_kernel, kb=kb, norm=1.0 / math.sqrt(n * c))
    out = pl.pallas_call(
        kern,
        out_shape=jax.ShapeDtypeStruct((n2, DFT_N1, c), F32),
        grid=(n2 // kb,),
        in_specs=[pl.BlockSpec((DFT_N1, 1, kb, c), lambda i: (0, 0, i, 0)),
                  pl.BlockSpec((DFT_N1, 1, kb, c), lambda i: (0, 1, i, 1)),
                  _const_spec((c, c))],
        out_specs=pl.BlockSpec((kb, DFT_N1, c), lambda i: (i, 0, 0)),
        compiler_params=_params(("parallel",), VMEM_LIMIT),
        name="fn_out",
    )(s4, s4, fn_w_bf)
    return out.reshape(n, c)


def _odd_out_kernel(yc_ref, hyg_ref, yd_ref, fng_ref, x_ref, gate_ref, wo_ref, fg_ref, o_ref, *, run, pitch):
    yc = _load_pitched(yc_ref, run, pitch)
    y = jnp.dot((yc * hyg_ref[...]).astype(BF16), wo_ref[0:HY_WIDTH, :], preferred_element_type=F32)
    y = y + jnp.dot((yd_ref[...] * fng_ref[...]).astype(BF16), wo_ref[HY_WIDTH:, :],
                    preferred_element_type=F32)
    xo = x_ref[...] + gate_ref[...] * y
    o_ref[...] = xo * lax.rsqrt(jnp.mean(xo * xo, axis=-1, keepdims=True) + EPS) * fg_ref[...]


def _odd_out(yc, hy_gate, yd, fn_gate, x, gate, w_out_bf, final_g):
    n, d = x.shape
    tm = _row_tile(n, 512)
    run, pitch = n // DFT_N1, _time_pitch(n)
    row = lambda w: pl.BlockSpec((tm, w), lambda i: (i, 0))
    vec = lambda w: pl.BlockSpec((1, w), lambda i: (0, 0))
    kern = functools.partial(_odd_out_kernel, run=run, pitch=pitch)
    return pl.pallas_call(
        kern,
        out_shape=jax.ShapeDtypeStruct((n, d), F32),
        grid=(n // tm,),
        in_specs=[pl.BlockSpec((tm // run * pitch, HY_WIDTH), lambda i: (i, 0)),
                  row(HY_WIDTH), row(FN_WIDTH), row(FN_WIDTH), row(d), vec(d), _const_spec((d, d)), vec(d)],
        out_specs=row(d),
        compiler_params=_params(("parallel",), VMEM_LIMIT),
        name="odd_out",
    )(yc, hy_gate, yd, fn_gate, x, gate, w_out_bf, final_g)


def _rope_tables(n):
    rows = n // GRID_W
    inv_freq = ROPE_THETA ** (-np.arange(ROPE_FREQS, dtype=np.float64) / ROPE_FREQS)
    ang_r = np.arange(rows, dtype=np.float64)[:, None] * inv_freq
    ang_c = np.arange(GRID_W, dtype=np.float64)[:, None] * inv_freq

    def table(fn, sign):
        r = np.broadcast_to(fn(ang_r)[:, None, :], (rows, GRID_W, ROPE_FREQS))
        c = np.broadcast_to(fn(ang_c)[None, :, :], (rows, GRID_W, ROPE_FREQS))
        return jnp.asarray(np.concatenate([sign * r, r, sign * c, c], axis=-1).reshape(n, HEAD_DIM)
                           .astype(np.float32))

    return table(np.cos, 1.0), table(np.sin, -1.0)


def _hyena_embedding(n, pos):
    pos = pos.astype(np.float64)
    t = (pos / (n - 1))[:, None]
    w = 2.0 * np.pi * pos[:, None] / n
    f = np.linspace(1e-4, HY_BANDS - 1, HY_BANDS)[None, :]
    emb = np.concatenate([t, np.cos(f * w), -np.sin(f * w)], axis=-1)
    return np.pad(emb, ((0, 0), (0, HY_HIDDEN - HY_EMB)))


def kernel(x, c, ctx, c_ctx, w_mod, b_mod, norm_g, ev_w_in, ev_w_out, pool_w, pool_scale, q_norm_g, k_norm_g,
           od_w_in, od_w_out, hy_conv_w, hy_conv_b, hy_w1, hy_b1, hy_w2, hy_b2, hy_w3, hy_freq, hy_skip, fn_w,
           final_g):
    n, d = x.shape[1], x.shape[2]
    nc = ctx.shape[1]
    x0 = x[0]
    ctx0 = ctx[0]
    vec = lambda a: a.reshape(1, -1)

    cond = jnp.zeros((8, d), F32).at[0].set(c[0]).at[1].set(c_ctx)
    mod = _modulation(cond, w_mod, b_mod)
    shift0, scale0, gate0 = (mod[0, 0:1, k * d:(k + 1) * d] for k in range(3))
    cshift0, cscale0 = mod[0, 1:2, 0:d], mod[0, 1:2, d:2 * d]
    shift1, scale1, gate1 = (mod[1, 0:1, k * d:(k + 1) * d] for k in range(3))

    w_in0 = ev_w_in[0].astype(BF16)
    cos, sin = _rope_tables(n)
    g0 = vec(norm_g[0])
    qg, kg = vec(q_norm_g[0]), vec(k_norm_g[0])
    a_val, a_gate, q, k, v, b_gate = _even_in(x0, g0, scale0, shift0, w_in0, qg, kg, cos, sin)
    ones = jnp.ones((nc, HEAD_DIM), F32)
    _, _, _, ck, cv, _ = _even_in(ctx0, g0, cscale0, cshift0, w_in0, qg, kg, ones, jnp.zeros_like(ones))
    yb = _attention(q, jnp.concatenate([ck, k], axis=0), jnp.concatenate([cv, v], axis=0), b_gate)
    pool_bd = jax.scipy.linalg.block_diag(*[pool_w[0, gi] for gi in range(pool_w.shape[1])]).astype(BF16)
    x1 = _even_out(a_val, a_gate, yb, x0, gate0, pool_bd, vec(pool_scale[0]), ev_w_out[0].astype(BF16))

    ch = np.arange(FN_WIDTH, dtype=np.float64)
    ph = 2.0 * np.pi * np.outer(ch, ch) / FN_WIDTH
    cs = _table_parts(np.concatenate([np.cos(ph), np.sin(ph)], axis=1))
    uc, hy_gate, pq, fn_gate = _odd_in(x1, vec(norm_g[1]), scale1, shift1, od_w_in[0].astype(BF16),
                                       hy_conv_w[0], vec(hy_conv_b[0]), cs)

    max_decay = math.log(HY_DECAY_TARGET) / HY_FAST_DECAY
    min_decay = math.log(HY_DECAY_TARGET) / HY_SLOW_DECAY
    deltas = jnp.linspace(min_decay, max_decay, HY_WIDTH, dtype=F32)[None, :]
    pos = np.arange(n)
    emb2 = jnp.asarray(np.concatenate([_hyena_embedding(n, pos),
                                       _hyena_embedding(n, np.where(pos == 0, 0, n - pos))], axis=-1)
                       .astype(np.float32))
    k_lo, k_hi, mass = _hyena_filters(emb2, hy_w1[0], vec(hy_b1[0]), hy_w2[0], vec(hy_b2[0]), hy_w3[0],
                                      vec(hy_freq[0]), deltas)

    n2 = n // DFT_N1
    m_full, _, big_fwd, big_inv = _dft_tables(2 * n, n2, True, 2 * DFT_N1)
    m_half, m_inv, _, _ = _dft_tables(2 * n, n2, True, DFT_N1)
    k_spec = _dft_fwd([k_lo, k_hi], HY_ORDER * HY_WIDTH, m_full, big_fwd, mass=mass, out_dtype=BF16)
    z1 = _hyena_conv(uc, 0, uc, HY_WIDTH, hy_skip[0, 0:1], k_spec, 0, m_half, m_inv, big_fwd, big_inv)
    yc = _hyena_conv(z1, 0, uc, 2 * HY_WIDTH, hy_skip[0, 1:2], k_spec, 1, m_half, m_inv, big_fwd, big_inv)

    f_half, _, f_big, _ = _dft_tables(n, n2, False, DFT_N1)
    pq_spec = _dft_fwd([pq], 2 * FN_WIDTH, f_half, f_big)
    yd = _fn_out(pq_spec, fn_w[0].astype(BF16), n)

    out = _odd_out(yc, hy_gate, yd, fn_gate, x1, gate1, od_w_out[0].astype(BF16), vec(final_g))
    return out[None]
```

```python
import functools
import math

import numpy as np
import jax
import jax.numpy as jnp
from jax import lax
from jax.experimental import pallas as pl
from jax.experimental.pallas import tpu as pltpu

F32 = jnp.float32
BF16 = jnp.bfloat16
HIGHEST = lax.Precision.HIGHEST

EPS = 1e-6
GRID_W = 64
HEAD_DIM = 128
ROPE_FREQS = 32
ROPE_THETA = 10000.0
N_Q_HEADS = 6
N_KV_HEADS = 2
Q_PER_KV = N_Q_HEADS // N_KV_HEADS
POOL_WIDTH = 256
POOL_GROUP_DIM = 64
POOL_WINDOWS = (2, 4, 8, 16)
POOL_HALO = 8
ATT_WIDTH = N_Q_HEADS * HEAD_DIM
KV_WIDTH = N_KV_HEADS * HEAD_DIM
HY_WIDTH = 768
HY_ORDER = 2
HY_EMB = 33
HY_BANDS = 16
HY_HIDDEN = 64
FN_WIDTH = 256
HY_DECAY_TARGET = 1e-2
HY_FAST_DECAY = 0.3
HY_SLOW_DECAY = 1.5

LANES = 128
ROW_TILE = 1024
DFT_N1 = 128
VMEM_LIMIT = 56 * 1024 * 1024
CONV_VMEM_LIMIT = 62 * 1024 * 1024


def _row_tile(n, pref):
    t = min(pref, n)
    assert n % t == 0
    return t


def _silu(x):
    return x * jax.nn.sigmoid(x)


def _params(sem, vmem=None, flags=None):
    return pltpu.CompilerParams(dimension_semantics=sem, vmem_limit_bytes=vmem, flags=flags)


def _const_spec(shape):
    nd = len(shape)
    return pl.BlockSpec(shape, lambda *_: (0,) * nd, pipeline_mode=pl.Buffered(1))


def _mod_kernel(cond_ref, w_ref, b_ref, o_ref):
    s = _silu(cond_ref[...])
    o_ref[0] = jnp.dot(s, w_ref[0], precision=HIGHEST, preferred_element_type=F32) + b_ref[0]


def _modulation(cond, w_mod, b_mod):
    depth, d, d3 = w_mod.shape
    tn = 1024
    return pl.pallas_call(
        _mod_kernel,
        out_shape=jax.ShapeDtypeStruct((depth, 8, d3), F32),
        grid=(depth, d3 // tn),
        in_specs=[pl.BlockSpec((8, d), lambda i, j: (0, 0)),
                  pl.BlockSpec((1, d, tn), lambda i, j: (i, 0, j)),
                  pl.BlockSpec((1, 1, tn), lambda i, j: (i, 0, j))],
        out_specs=pl.BlockSpec((1, 8, tn), lambda i, j: (i, 0, j)),
        compiler_params=_params(("arbitrary", "arbitrary")),
        name="modulation",
    )(cond, w_mod, b_mod.reshape(depth, 1, d3))


def _time_pitch(n):
    return n // DFT_N1 + 8


def _store_pitched(ref, cols, val, run, pitch, first=0):
    for b in range(val.shape[0] // run):
        r0 = (first + b) * pitch
        ref[r0:r0 + run, cols] = val[b * run:(b + 1) * run]
        ref[r0 + run:r0 + pitch, cols] = jnp.zeros((pitch - run, val.shape[1]), val.dtype)


def _load_pitched(ref, run, pitch):
    return jnp.concatenate([ref[b * pitch:b * pitch + run, :] for b in range(ref.shape[0] // pitch)], axis=0)


def _norm_mod(x, g, scale, shift):
    y = x * lax.rsqrt(jnp.mean(x * x, axis=-1, keepdims=True) + EPS)
    return (y * g) * (1.0 + scale) + shift


def _even_in_kernel(x_ref, g_ref, sc_ref, sh_ref, w_ref, qg_ref, kg_ref, cos_ref, sin_ref,
                    aval_ref, agate_ref, q_ref, k_ref, v_ref, bgate_ref):
    half = x_ref.shape[0] // 2
    for s in range(2):
        rs = slice(s * half, (s + 1) * half)
        h = _norm_mod(x_ref[rs, :], g_ref[...], sc_ref[...], sh_ref[...])
        p = jnp.dot(h.astype(BF16), w_ref[...], preferred_element_type=F32)
        aval_ref[rs, :] = p[:, 0:POOL_WIDTH]
        agate_ref[rs, :] = _silu(p[:, POOL_WIDTH:2 * POOL_WIDTH])
        cos = cos_ref[rs, :]
        sin = sin_ref[rs, :]
        lane = lax.broadcasted_iota(jnp.int32, cos.shape, 1)
        low_half = (lane % (2 * ROPE_FREQS)) < ROPE_FREQS

        def head(xh, g):
            y = xh * lax.rsqrt(jnp.mean(xh * xh, axis=-1, keepdims=True) + EPS) * g
            rot = jnp.where(low_half, pltpu.roll(y, HEAD_DIM - ROPE_FREQS, 1), pltpu.roll(y, ROPE_FREQS, 1))
            return y * cos + rot * sin

        q0 = 2 * POOL_WIDTH
        for hq in range(N_Q_HEADS):
            sl = slice(hq * HEAD_DIM, (hq + 1) * HEAD_DIM)
            q_ref[rs, sl] = head(p[:, q0 + hq * HEAD_DIM:q0 + (hq + 1) * HEAD_DIM], qg_ref[...]).astype(BF16)
        k0 = q0 + ATT_WIDTH
        for hk in range(N_KV_HEADS):
            sl = slice(hk * HEAD_DIM, (hk + 1) * HEAD_DIM)
            k_ref[rs, sl] = head(p[:, k0 + hk * HEAD_DIM:k0 + (hk + 1) * HEAD_DIM], kg_ref[...]).astype(BF16)
        v0 = k0 + KV_WIDTH
        v_ref[rs, :] = p[:, v0:v0 + KV_WIDTH].astype(BF16)
        b0 = v0 + KV_WIDTH
        bgate_ref[rs, :] = _silu(p[:, b0:b0 + ATT_WIDTH]).astype(bgate_ref.dtype)


def _even_in(x, g, scale, shift, w_bf, qg, kg, cos, sin):
    n, d = x.shape
    n_in = w_bf.shape[1]
    tm = _row_tile(n, ROW_TILE // 2)
    row = lambda w: pl.BlockSpec((tm, w), lambda i: (i, 0))
    vec = lambda w: pl.BlockSpec((1, w), lambda i: (0, 0))
    return pl.pallas_call(
        _even_in_kernel,
        out_shape=(jax.ShapeDtypeStruct((n, POOL_WIDTH), F32), jax.ShapeDtypeStruct((n, POOL_WIDTH), F32),
                   jax.ShapeDtypeStruct((n, ATT_WIDTH), BF16), jax.ShapeDtypeStruct((n, KV_WIDTH), BF16),
                   jax.ShapeDtypeStruct((n, KV_WIDTH), BF16), jax.ShapeDtypeStruct((n, ATT_WIDTH), BF16)),
        grid=(n // tm,),
        in_specs=[row(d), vec(d), vec(d), vec(d), _const_spec((d, n_in)), vec(HEAD_DIM), vec(HEAD_DIM),
                  row(HEAD_DIM), row(HEAD_DIM)],
        out_specs=(row(POOL_WIDTH), row(POOL_WIDTH), row(ATT_WIDTH), row(KV_WIDTH), row(KV_WIDTH),
                   row(ATT_WIDTH)),
        compiler_params=_params(("parallel",), VMEM_LIMIT),
        name="even_in",
    )(x, g, scale, shift, w_bf, qg, kg, cos, sin)


ATT_UNROLL = 4
ATT_ROW_CHUNK = 16


def _attn_kernel(q_ref, k_ref, v_ref, bg_ref, o_ref, s_sc, p_sc, m_sc, al_sc, acc_sc, *, tq, tk, n_kv_tiles):
    c = (HEAD_DIM ** -0.5) * math.log2(math.e)
    rows = Q_PER_KV * tq
    for h in range(N_KV_HEADS):
        hs = slice(h * HEAD_DIM, (h + 1) * HEAD_DIM)
        qs = jnp.concatenate(
            [q_ref[:, (Q_PER_KV * h + g) * HEAD_DIM:(Q_PER_KV * h + g + 1) * HEAD_DIM] for g in range(Q_PER_KV)],
            axis=0)

        def scores(buf, t):
            kt = k_ref[pl.ds(t * tk if isinstance(t, int) else pl.multiple_of(t * tk, tk), tk), hs]
            s_sc[buf] = lax.dot_general(qs, kt, (((1,), (1,)), ((), ())), preferred_element_type=F32)

        def softmax(buf):
            for r in range(rows // ATT_ROW_CHUNK):
                rs = slice(r * ATT_ROW_CHUNK, (r + 1) * ATT_ROW_CHUNK)
                cols = [s_sc[buf, rs, j * LANES:(j + 1) * LANES] for j in range(tk // LANES)]
                mx = functools.reduce(jnp.maximum, cols)
                m_old = m_sc[rs, :]
                m_new = jnp.maximum(m_old, jnp.max(mx, axis=-1, keepdims=True) * c)
                for j, sj in enumerate(cols):
                    p_sc[buf, rs, j * LANES:(j + 1) * LANES] = jnp.exp2(sj * c - m_new).astype(BF16)
                al_sc[buf, rs, :] = jnp.exp2(m_old - m_new)
                m_sc[rs, :] = m_new

        def accumulate(buf, t):
            vt = v_ref[pl.ds(t * tk if isinstance(t, int) else pl.multiple_of(t * tk, tk), tk), hs]
            v_ext = jnp.concatenate([vt, jnp.ones((tk, HEAD_DIM), BF16)], axis=1)
            pv = jnp.dot(p_sc[buf], v_ext, preferred_element_type=F32)
            al = al_sc[buf]
            for j in range(2):
                js = slice(j * HEAD_DIM, (j + 1) * HEAD_DIM)
                acc_sc[:, js] = al * acc_sc[:, js] + pv[:, js]

        def slot(t, buf, with_scores=True):
            if with_scores:
                scores(1 - buf, t + 1)
            softmax(buf)
            accumulate(buf, t)

        m_sc[...] = jnp.full(m_sc.shape, -1e30, F32)
        acc_sc[...] = jnp.zeros(acc_sc.shape, F32)
        scores(0, 0)

        def body(i, carry):
            for u in range(ATT_UNROLL):
                slot(ATT_UNROLL * i + u, u % 2)
            return carry

        n_full = (n_kv_tiles - 1) // ATT_UNROLL
        lax.fori_loop(0, n_full, body, 0)
        for t in range(ATT_UNROLL * n_full, n_kv_tiles):
            slot(t, t % 2, with_scores=t + 1 < n_kv_tiles)
        out = acc_sc[:, 0:HEAD_DIM] / acc_sc[:, HEAD_DIM:2 * HEAD_DIM]
        for g in range(Q_PER_KV):
            cs = slice((Q_PER_KV * h + g) * HEAD_DIM, (Q_PER_KV * h + g + 1) * HEAD_DIM)
            o_ref[:, cs] = (out[g * tq:(g + 1) * tq] * bg_ref[:, cs]).astype(o_ref.dtype)


ATT_KEY_TILE = 1280


def _attention(q, k_all, v_all, bgate):
    n = q.shape[0]
    nk = k_all.shape[0]
    tq = _row_tile(n, 256)
    tk = max(t for t in range(LANES, ATT_KEY_TILE + 1, LANES) if nk % t == 0)
    rows = Q_PER_KV * tq
    kern = functools.partial(_attn_kernel, tq=tq, tk=tk, n_kv_tiles=nk // tk)
    return pl.pallas_call(
        kern,
        out_shape=jax.ShapeDtypeStruct((n, ATT_WIDTH), BF16),
        grid=(n // tq,),
        in_specs=[pl.BlockSpec((tq, ATT_WIDTH), lambda i: (i, 0)),
                  _const_spec((nk, KV_WIDTH)), _const_spec((nk, KV_WIDTH)),
                  pl.BlockSpec((tq, ATT_WIDTH), lambda i: (i, 0))],
        out_specs=pl.BlockSpec((tq, ATT_WIDTH), lambda i: (i, 0)),
        scratch_shapes=[pltpu.VMEM((2, rows, tk), F32), pltpu.VMEM((2, rows, tk), BF16),
                        pltpu.VMEM((rows, LANES), F32), pltpu.VMEM((2, rows, LANES), F32),
                        pltpu.VMEM((rows, 2 * HEAD_DIM), F32)],
        compiler_params=_params(("parallel",), VMEM_LIMIT),
        name="attention",
    )(q, k_all, v_all, bgate)


def _even_out_kernel(a_ref, ap_ref, an_ref, ag_ref, yb_ref, x_ref, gate_ref, pw_ref, ps_ref, wo_ref, o_ref,
                     *, tm, n_rows):
    i = pl.program_id(0)
    n_tiles = pl.num_programs(0)
    u = a_ref[...]
    prev = jnp.where(i > 0, ap_ref[...], 0.0)
    nxt = jnp.where(i < n_tiles - 1, an_ref[...], 0.0)
    e = jnp.concatenate([prev, u, nxt], axis=0)
    n = tm + 2 * POOL_HALO
    s2 = e[0:n - 1] + e[1:n]
    s4 = s2[0:n - 3] + s2[2:n - 1]
    s8 = s4[0:n - 7] + s4[4:n - 3]
    s16 = s8[0:n - 15] + s8[8:n - 7]
    lane = lax.broadcasted_iota(jnp.int32, (tm, POOL_WIDTH), 1)
    grp = lane // POOL_GROUP_DIM
    win = jnp.where(grp == 0, s2[7:7 + tm],
                    jnp.where(grp == 1, s4[6:6 + tm], jnp.where(grp == 2, s8[4:4 + tm], s16[0:tm])))
    half = jnp.where(grp == 0, 1, jnp.where(grp == 1, 2, jnp.where(grp == 2, 4, 8)))
    t = i * tm + lax.broadcasted_iota(jnp.int32, (tm, POOL_WIDTH), 0)
    cnt = (jnp.minimum(t + half, n_rows) - jnp.maximum(t - half, 0)).astype(F32)
    d = win / cnt - u
    ya = jnp.dot(d.astype(BF16), pw_ref[...], preferred_element_type=F32) * ps_ref[...]
    ya = ya * ag_ref[...]
    y = jnp.dot(ya.astype(BF16), wo_ref[0:POOL_WIDTH, :], preferred_element_type=F32)
    y = y + jnp.dot(yb_ref[...].astype(BF16), wo_ref[POOL_WIDTH:, :], preferred_element_type=F32)
    o_ref[...] = x_ref[...] + gate_ref[...] * y


def _even_out(a_val, a_gate, yb, x, gate, pool_bd, pool_scale, w_out_bf):
    n, d = x.shape
    tm = _row_tile(n, ROW_TILE)
    hb = tm // POOL_HALO
    last = n // POOL_HALO - 1
    row = lambda w: pl.BlockSpec((tm, w), lambda i: (i, 0))
    vec = lambda w: pl.BlockSpec((1, w), lambda i: (0, 0))
    kern = functools.partial(_even_out_kernel, tm=tm, n_rows=n)
    return pl.pallas_call(
        kern,
        out_shape=jax.ShapeDtypeStruct((n, d), F32),
        grid=(n // tm,),
        in_specs=[row(POOL_WIDTH),
                  pl.BlockSpec((POOL_HALO, POOL_WIDTH), lambda i: (jnp.maximum(i * hb - 1, 0), 0)),
                  pl.BlockSpec((POOL_HALO, POOL_WIDTH), lambda i: (jnp.minimum((i + 1) * hb, last), 0)),
                  row(POOL_WIDTH), row(ATT_WIDTH), row(d), vec(d),
                  _const_spec((POOL_WIDTH, POOL_WIDTH)), vec(POOL_WIDTH), _const_spec((d, d))],
        out_specs=row(d),
        compiler_params=_params(("parallel",), VMEM_LIMIT),
        name="even_out",
    )(a_val, a_val, a_val, a_gate, yb, x, gate, pool_bd, pool_scale, w_out_bf)


def _odd_in_kernel(x_ref, xp_ref, xn_ref, g_ref, sc_ref, sh_ref, w_ref, cw_ref, cb_ref, cs_ref,
                   uc_ref, hyg_ref, pq_ref, fng_ref, *, tm, run, pitch):
    i = pl.program_id(0)
    n_tiles = pl.num_programs(0)
    g, sc, sh = g_ref[...], sc_ref[...], sh_ref[...]
    c0 = (HY_ORDER + 1) * HY_WIDTH
    halo = _norm_mod(jnp.concatenate([xp_ref[...], xn_ref[...]], axis=0), g, sc, sh)
    ph = jnp.dot(halo.astype(BF16), w_ref[:, 0:c0], preferred_element_type=F32)
    half = tm // 2
    ps = [jnp.dot(_norm_mod(x_ref[s * half:(s + 1) * half, :], g, sc, sh).astype(BF16), w_ref[...],
                  preferred_element_type=F32) for s in range(2)]
    edge = [jnp.where(i > 0, ph[7:8, :], 0.0), ps[0][half - 1:half, 0:c0], ps[1][0:1, 0:c0],
            jnp.where(i < n_tiles - 1, ph[8:9, :], 0.0)]
    for s, p in enumerate(ps):
        rs = slice(s * half, (s + 1) * half)
        u = p[:, 0:c0]
        e = jnp.concatenate([edge[s], u, edge[s + 2]], axis=0)
        uc = e[0:half] * cw_ref[0:1, :] + u * cw_ref[1:2, :] + e[2:half + 2] * cw_ref[2:3, :] + cb_ref[...]
        _store_pitched(uc_ref, slice(None), uc, run, pitch, first=s * (half // run))
        hyg_ref[rs, :] = _silu(p[:, c0:c0 + HY_WIDTH]).astype(hyg_ref.dtype)
        fn_in = p[:, c0 + HY_WIDTH:c0 + HY_WIDTH + FN_WIDTH]
        _store_pitched(pq_ref, slice(None), _dot_parts(_split_bf16(fn_in), _ref_parts(cs_ref)), run, pitch,
                       first=s * (half // run))
        fng_ref[rs, :] = _silu(p[:, c0 + HY_WIDTH + FN_WIDTH:])


def _odd_in(x, g, scale, shift, w_bf, conv_w, conv_b, cs):
    n, d = x.shape
    n_in = w_bf.shape[1]
    tm = _row_tile(n, ROW_TILE // 2)
    hb = tm // 8
    last = n // 8 - 1
    row = lambda w: pl.BlockSpec((tm, w), lambda i: (i, 0))
    vec = lambda w: pl.BlockSpec((1, w), lambda i: (0, 0))
    c0 = (HY_ORDER + 1) * HY_WIDTH
    run, pitch = n // DFT_N1, _time_pitch(n)
    assert tm % run == 0
    tmp = tm // run * pitch
    prow = lambda w: pl.BlockSpec((tmp, w), lambda i: (i, 0))
    kern = functools.partial(_odd_in_kernel, tm=tm, run=run, pitch=pitch)
    return pl.pallas_call(
        kern,
        out_shape=(jax.ShapeDtypeStruct((DFT_N1 * pitch, c0), F32), jax.ShapeDtypeStruct((n, HY_WIDTH), BF16),
                   jax.ShapeDtypeStruct((DFT_N1 * pitch, 2 * FN_WIDTH), F32),
                   jax.ShapeDtypeStruct((n, FN_WIDTH), F32)),
        grid=(n // tm,),
        in_specs=[row(d),
                  pl.BlockSpec((8, d), lambda i: (jnp.maximum(i * hb - 1, 0), 0)),
                  pl.BlockSpec((8, d), lambda i: (jnp.minimum((i + 1) * hb, last), 0)),
                  vec(d), vec(d), vec(d), _const_spec((d, n_in)),
                  pl.BlockSpec((3, c0), lambda i: (0, 0)), vec(c0), _const_spec(cs.shape)],
        out_specs=(prow(c0), row(HY_WIDTH), prow(2 * FN_WIDTH), row(FN_WIDTH)),
        compiler_params=_params(("parallel",), VMEM_LIMIT),
        name="odd_in",
    )(x, x, x, g, scale, shift, w_bf, conv_w, conv_b, cs)


def _filter_kernel(emb_ref, w1_ref, b1_ref, w2_ref, b2_ref, w3f_ref, w3b_ref, fr_ref, dl_ref,
                   lo_ref, hi_ref, mass_ref, *, tm, n_rows, run, pitch):
    i = pl.program_id(0)
    fr = fr_ref[...]
    a = jnp.sin(fr * (jnp.dot(emb_ref[...], w1_ref[...], precision=HIGHEST, preferred_element_type=F32)
                      + b1_ref[...]))
    a = jnp.sin(fr * (jnp.dot(a, w2_ref[...], precision=HIGHEST, preferred_element_type=F32) + b2_ref[...]))
    a_f = a_b = a.astype(BF16)
    j = i * tm + lax.broadcasted_iota(jnp.int32, (tm, 1), 0)
    pos_b = jnp.where(j == 0, 0, n_rows - j)
    dl = jnp.abs(dl_ref[...])
    inv = 1.0 / (n_rows - 1)
    decay_f = jnp.exp(-(j.astype(F32) * inv) * dl)
    decay_b = jnp.exp(-(pos_b.astype(F32) * inv) * dl)

    @pl.when(i == 0)
    def _():
        mass_ref[...] = jnp.zeros_like(mass_ref)

    for o in range(HY_ORDER):
        os_ = slice(o * HY_WIDTH, (o + 1) * HY_WIDTH)
        hf = jnp.dot(a_f, w3f_ref[:, os_], preferred_element_type=F32) * decay_f
        hb = jnp.dot(a_b, w3b_ref[:, os_], preferred_element_type=F32) * decay_b
        _store_pitched(lo_ref, os_, hf, run, pitch)
        _store_pitched(hi_ref, os_, jnp.where(j == 0, 0.0, -hb), run, pitch)
        mass_ref[:, os_] += (jnp.sum(jnp.abs(hf), axis=0, keepdims=True)
                             + jnp.sum(jnp.abs(hb), axis=0, keepdims=True))


def _hyena_filters(emb2, w1, b1, w2, b2, w3, freq, deltas):
    n = emb2.shape[0]
    tm = _row_tile(n, ROW_TILE)
    nh = HY_ORDER * HY_WIDTH
    hh = HY_HIDDEN
    w1b = jnp.zeros((LANES, 2 * hh), F32).at[0:HY_EMB, 0:hh].set(w1).at[hh:hh + HY_EMB, hh:].set(w1)
    w2b = jnp.zeros((2 * hh, 2 * hh), F32).at[0:hh, 0:hh].set(w2).at[hh:, hh:].set(w2)
    w3r = w3.reshape(hh, HY_ORDER, 2, HY_WIDTH)
    zero = jnp.zeros((hh, nh), F32)
    w3f = jnp.concatenate([w3r[:, :, 0, :].reshape(hh, nh), zero], axis=0).astype(BF16)
    w3b = jnp.concatenate([zero, w3r[:, :, 1, :].reshape(hh, nh)], axis=0).astype(BF16)
    twice = lambda a: jnp.concatenate([a, a], axis=-1)
    ops = (emb2, w1b, twice(b1), w2b, twice(b2), w3f, w3b, twice(freq), deltas)
    run, pitch = n // DFT_N1, _time_pitch(n)
    assert tm % run == 0
    prow = pl.BlockSpec((tm // run * pitch, nh), lambda i: (i, 0))
    kern = functools.partial(_filter_kernel, tm=tm, n_rows=n, run=run, pitch=pitch)
    full = lambda a: pl.BlockSpec(a.shape, lambda i: (0,) * a.ndim)
    row = lambda w: pl.BlockSpec((tm, w), lambda i: (i, 0))
    return pl.pallas_call(
        kern,
        out_shape=(jax.ShapeDtypeStruct((DFT_N1 * pitch, nh), F32), jax.ShapeDtypeStruct((DFT_N1 * pitch, nh), F32),
                   jax.ShapeDtypeStruct((1, nh), F32)),
        grid=(n // tm,),
        in_specs=[row(LANES)] + [full(a) for a in ops[1:]],
        out_specs=(prow, prow, pl.BlockSpec((1, nh), lambda i: (0, 0))),
        compiler_params=_params(("arbitrary",), VMEM_LIMIT),
        name="hyena_filters",
    )(*ops)


DFT_PARTS = 1


def _table_parts(a):
    a32 = jnp.asarray(a.astype(np.float32))
    hi = a32.astype(BF16)
    if DFT_PARTS == 1:
        return hi[None]
    lo = (a32 - hi.astype(F32)).astype(BF16)
    return jnp.stack([hi, lo])


def _split_bf16(x):
    hi = x.astype(BF16)
    if DFT_PARTS == 1:
        return (hi,)
    return (hi, (x - hi.astype(F32)).astype(BF16))


def _ref_parts(ref, *idx):
    return tuple(ref[(q,) + idx] for q in range(ref.shape[0]))


def _dot_parts(lhs, rhs):
    acc = jnp.dot(lhs[0], rhs[0], preferred_element_type=F32)
    if len(lhs) > 1:
        acc = acc + jnp.dot(lhs[1], rhs[0], preferred_element_type=F32)
    if len(rhs) > 1:
        acc = acc + jnp.dot(lhs[0], rhs[1], preferred_element_type=F32)
    return acc


def _dft_tables(n_total, n2, half_shift, n1_used):
    n1_full = n_total // n2
    sh = 0.5 if half_shift else 0.0
    k1 = np.arange(DFT_N1, dtype=np.float64)[None, :, None] + sh
    n1 = np.arange(n1_used, dtype=np.float64)[None, None, :]
    nn2 = np.arange(n2, dtype=np.float64)[:, None, None]
    ph = -2.0 * np.pi * (n1 * k1 / n1_full + nn2 * k1 / n_total)
    fwd = np.stack([np.cos(ph), np.sin(ph)], axis=2).reshape(n2, 2 * DFT_N1, n1_used)
    ph_i = ph[:, :, :DFT_N1]
    inv = np.stack([np.cos(ph_i), np.sin(ph_i)], axis=2).reshape(n2, 2 * DFT_N1, DFT_N1)
    inv = (2.0 / n_total) * inv.transpose(0, 2, 1)
    k2 = np.arange(n2, dtype=np.float64)
    ph2 = -2.0 * np.pi * np.outer(k2, k2) / n2
    fr, fi = np.cos(ph2), np.sin(ph2)
    big_fwd = np.block([[fr, -fi], [fi, fr]])
    big_inv = np.block([[fr, fi], [-fi, fr]])
    return _table_parts(fwd), _table_parts(inv), _table_parts(big_fwd), _table_parts(big_inv)


DFT_STEP = 32
DFT_FWD_STEP = 64


def _stage1_fwd(x_refs, m_ref, a_sc, step, *, pitch):
    for u in range(m_ref.shape[1]):
        nn = step * m_ref.shape[1] + u
        xs = [xr[pl.ds(nn, DFT_N1, stride=pitch), :] for xr in x_refs]
        xs = xs[0] if len(xs) == 1 else jnp.concatenate(xs, axis=0)
        a_sc[pl.ds(nn, 2 * DFT_N1, stride=pitch), :] = _dot_parts(_ref_parts(m_ref, u), _split_bf16(xs))


def _load_cplx(a_sc, k1, *, n2, pitch):
    r0 = pl.multiple_of(k1 * 2 * pitch, 8)
    return jnp.concatenate([a_sc[pl.ds(r0, n2), :], a_sc[pl.ds(r0 + pitch, n2), :]], axis=0)


def _phase_steps(n2, step=None):
    step = step or DFT_STEP
    n2g = min(step, n2)
    return n2g, n2 // n2g, DFT_N1 // step


def _dft_fwd_kernel(*refs, n_x, n2, pitch, ja, step, scaled):
    x_refs = refs[:n_x]
    rest = refs[n_x:]
    if scaled:
        mass_ref, rest = rest[0], rest[1:]
    m_ref, big_ref, o_ref, a_sc = rest
    j = pl.program_id(1)

    @pl.when(j < ja)
    def _():
        _stage1_fwd(x_refs, m_ref, a_sc, j, pitch=pitch)

    @pl.when(j >= ja)
    def _():
        big = _ref_parts(big_ref)
        for g in range(step):
            blk = _load_cplx(a_sc, (j - ja) * step + g, n2=n2, pitch=pitch)
            out = _dot_parts(big, _split_bf16(blk))
            if scaled:
                out = out * (1.0 / (mass_ref[...] + EPS))
            o_ref[g * 2 * n2:(g + 1) * 2 * n2, :] = out.astype(o_ref.dtype)


def _dft_fwd(xs, width, m_fwd, big, mass=None, out_dtype=F32):
    n2 = xs[0].shape[0] // DFT_N1 - 8
    pitch = n2 + 8
    n = DFT_N1 * pitch
    n2g, ja, jb = _phase_steps(n2, DFT_FWD_STEP)
    wt = LANES
    rows = 2 * DFT_N1 * n2
    kern = functools.partial(_dft_fwd_kernel, n_x=len(xs), n2=n2, pitch=pitch, ja=ja, step=DFT_FWD_STEP,
                             scaled=mass is not None)
    x_spec = pl.BlockSpec((n, wt), lambda i, j: (0, i), pipeline_mode=pl.Buffered(1))
    ops = list(xs)
    specs = [x_spec] * len(xs)
    if mass is not None:
        ops.append(mass)
        specs.append(pl.BlockSpec((1, wt), lambda i, j: (0, i)))
    m_spec = pl.BlockSpec((m_fwd.shape[0], n2g) + m_fwd.shape[2:],
                          lambda i, j: (0, jnp.minimum(j, ja - 1), 0, 0))
    return pl.pallas_call(
        kern,
        out_shape=jax.ShapeDtypeStruct((rows, width), out_dtype),
        grid=(width // wt, ja + jb),
        in_specs=specs + [m_spec, _const_spec(big.shape)],
        out_specs=pl.BlockSpec((DFT_FWD_STEP * 2 * n2, wt), lambda i, j: (jnp.maximum(j - ja, 0), i)),
        scratch_shapes=[pltpu.VMEM((2 * DFT_N1 * pitch, wt), F32)],
        compiler_params=_params(("parallel", "arbitrary"), CONV_VMEM_LIMIT),
        name="dft_fwd",
    )(*ops, m_fwd, big)


def _hyena_conv_kernel(x_ref, g_ref, s_ref, k_ref, m_ref, minv_ref, bigf_ref, bigi_ref, y_ref, a_sc,
                       *, n2, pitch, ja, jb):
    j = pl.program_id(1)

    @pl.when(j < ja)
    def _():
        _stage1_fwd([x_ref], m_ref, a_sc, j, pitch=pitch)

    @pl.when((j >= ja) & (j < ja + jb))
    def _():
        bigf = _ref_parts(bigf_ref)
        bigi = _ref_parts(bigi_ref)
        for g in range(DFT_STEP):
            k1 = (j - ja) * DFT_STEP + g
            z = _dot_parts(bigf, _split_bf16(_load_cplx(a_sc, k1, n2=n2, pitch=pitch)))
            zr, zi = z[0:n2], z[n2:2 * n2]
            kr = k_ref[g * 2 * n2:g * 2 * n2 + n2, :].astype(F32)
            ki = k_ref[g * 2 * n2 + n2:(g + 1) * 2 * n2, :].astype(F32)
            prod = jnp.concatenate([zr * kr - zi * ki, zr * ki + zi * kr], axis=0)
            b = _dot_parts(bigi, _split_bf16(prod))
            r0 = pl.multiple_of(k1 * 2 * pitch, 8)
            a_sc[pl.ds(r0, n2), :] = b[0:n2]
            a_sc[pl.ds(r0 + pitch, n2), :] = b[n2:2 * n2]

    @pl.when(j == ja + jb)
    def _():
        for r in range(pitch - n2):
            y_ref[pl.ds(n2 + r, DFT_N1, stride=pitch), :] = jnp.zeros((DFT_N1, y_ref.shape[1]), F32)

    @pl.when(j >= ja + jb)
    def _():
        skip = s_ref[...]
        for u in range(minv_ref.shape[1]):
            nn = (j - ja - jb) * minv_ref.shape[1] + u
            bs = a_sc[pl.ds(nn, 2 * DFT_N1, stride=pitch), :]
            conv = _dot_parts(_ref_parts(minv_ref, u), _split_bf16(bs))
            rows = pl.ds(nn, DFT_N1, stride=pitch)
            y_ref[rows, :] = g_ref[rows, :] * (conv + skip * x_ref[rows, :])


def _hyena_conv(x, x_col, gate, gate_col, skip, kspec, order, m_fwd, m_inv, big_fwd, big_inv):
    n2 = x.shape[0] // DFT_N1 - 8
    pitch = n2 + 8
    n = DFT_N1 * pitch
    n2g, ja, jb = _phase_steps(n2)
    wt = LANES
    c0 = x_col // wt
    cg = gate_col // wt
    ck = order * HY_WIDTH // wt
    kern = functools.partial(_hyena_conv_kernel, n2=n2, pitch=pitch, ja=ja, jb=jb)
    return pl.pallas_call(
        kern,
        out_shape=jax.ShapeDtypeStruct((n, HY_WIDTH), F32),
        grid=(HY_WIDTH // wt, ja + jb + ja),
        in_specs=[pl.BlockSpec((n, wt), lambda i, j: (0, c0 + i), pipeline_mode=pl.Buffered(1)),
                  pl.BlockSpec((n, wt), lambda i, j: (0, cg + i), pipeline_mode=pl.Buffered(1)),
                  pl.BlockSpec((1, wt), lambda i, j: (0, i)),
                  pl.BlockSpec((DFT_STEP * 2 * n2, wt), lambda i, j: (jnp.clip(j - ja, 0, jb - 1), ck + i)),
                  pl.BlockSpec((m_fwd.shape[0], n2g) + m_fwd.shape[2:],
                               lambda i, j: (0, jnp.minimum(j, ja - 1), 0, 0)),
                  pl.BlockSpec((m_inv.shape[0], n2g) + m_inv.shape[2:],
                               lambda i, j: (0, jnp.clip(j - ja - jb, 0, ja - 1), 0, 0)),
                  _const_spec(big_fwd.shape), _const_spec(big_inv.shape)],
        out_specs=pl.BlockSpec((n, wt), lambda i, j: (0, i), pipeline_mode=pl.Buffered(1)),
        scratch_shapes=[pltpu.VMEM((2 * DFT_N1 * pitch, wt), F32)],
        compiler_params=_params(("parallel", "arbitrary"), CONV_VMEM_LIMIT),
        name="hyena_conv",
    )(x, gate, skip, kspec, m_fwd, m_inv, big_fwd, big_inv)


def _fn_out_kernel(p_ref, q_ref, w_ref, o_ref, *, kb, norm):
    w = w_ref[...]
    for jj in range(kb):
        r = (p_ref[:, 0, jj, :] + q_ref[:, 0, jj, :]) * norm
        o_ref[jj] = jnp.dot(r.astype(BF16), w, preferred_element_type=F32)


def _fn_out(spec, fn_w_bf, n):
    n2 = n // DFT_N1
    c = FN_WIDTH
    kb = 8
    s4 = spec.reshape(DFT_N1, 2, n2, 2 * c)
    kern = functools.partial(_fn_out_kernel, kb=kb, norm=1.0 / math.sqrt(n * c))
    out = pl.pallas_call(
        kern,
        out_shape=jax.ShapeDtypeStruct((n2, DFT_N1, c), F32),
        grid=(n2 // kb,),
        in_specs=[pl.BlockSpec((DFT_N1, 1, kb, c), lambda i: (0, 0, i, 0)),
                  pl.BlockSpec((DFT_N1, 1, kb, c), lambda i: (0, 1, i, 1)),
                  _const_spec((c, c))],
        out_specs=pl.BlockSpec((kb, DFT_N1, c), lambda i: (i, 0, 0)),
        compiler_params=_params(("parallel",), VMEM_LIMIT),
        name="fn_out",
    )(s4, s4, fn_w_bf)
    return out.reshape(n, c)


def _odd_out_kernel(yc_ref, hyg_ref, yd_ref, fng_ref, x_ref, gate_ref, wo_ref, fg_ref, o_ref, *, run, pitch):
    yc = _load_pitched(yc_ref, run, pitch)
    y = jnp.dot((yc * hyg_ref[...]).astype(BF16), wo_ref[0:HY_WIDTH, :], preferred_element_type=F32)
    y = y + jnp.dot((yd_ref[...] * fng_ref[...]).astype(BF16), wo_ref[HY_WIDTH:, :],
                    preferred_element_type=F32)
    xo = x_ref[...] + gate_ref[...] * y
    o_ref[...] = xo * lax.rsqrt(jnp.mean(xo * xo, axis=-1, keepdims=True) + EPS) * fg_ref[...]


def _odd_out(yc, hy_gate, yd, fn_gate, x, gate, w_out_bf, final_g):
    n, d = x.shape
    tm = _row_tile(n, ROW_TILE)
    run, pitch = n // DFT_N1, _time_pitch(n)
    row = lambda w: pl.BlockSpec((tm, w), lambda i: (i, 0))
    vec = lambda w: pl.BlockSpec((1, w), lambda i: (0, 0))
    kern = functools.partial(_odd_out_kernel, run=run, pitch=pitch)
    return pl.pallas_call(
        kern,
        out_shape=jax.ShapeDtypeStruct((n, d), F32),
        grid=(n // tm,),
        in_specs=[pl.BlockSpec((tm // run * pitch, HY_WIDTH), lambda i: (i, 0)),
                  row(HY_WIDTH), row(FN_WIDTH), row(FN_WIDTH), row(d), vec(d), _const_spec((d, d)), vec(d)],
        out_specs=row(d),
        compiler_params=_params(("parallel",), VMEM_LIMIT),
        name="odd_out",
    )(yc, hy_gate, yd, fn_gate, x, gate, w_out_bf, final_g)


def _rope_tables(n):
    rows = n // GRID_W
    inv_freq = ROPE_THETA ** (-np.arange(ROPE_FREQS, dtype=np.float64) / ROPE_FREQS)
    ang_r = np.arange(rows, dtype=np.float64)[:, None] * inv_freq
    ang_c = np.arange(GRID_W, dtype=np.float64)[:, None] * inv_freq

    def table(fn, sign):
        r = np.broadcast_to(fn(ang_r)[:, None, :], (rows, GRID_W, ROPE_FREQS))
        c = np.broadcast_to(fn(ang_c)[None, :, :], (rows, GRID_W, ROPE_FREQS))
        return jnp.asarray(np.concatenate([sign * r, r, sign * c, c], axis=-1).reshape(n, HEAD_DIM)
                           .astype(np.float32))

    return table(np.cos, 1.0), table(np.sin, -1.0)


def _hyena_embedding(n, pos):
    pos = pos.astype(np.float64)
    t = (pos / (n - 1))[:, None]
    w = 2.0 * np.pi * pos[:, None] / n
    f = np.linspace(1e-4, HY_BANDS - 1, HY_BANDS)[None, :]
    emb = np.concatenate([t, np.cos(f * w), -np.sin(f * w)], axis=-1)
    return np.pad(emb, ((0, 0), (0, HY_HIDDEN - HY_EMB)))


def kernel(x, c, ctx, c_ctx, w_mod, b_mod, norm_g, ev_w_in, ev_w_out, pool_w, pool_scale, q_norm_g, k_norm_g,
           od_w_in, od_w_out, hy_conv_w, hy_conv_b, hy_w1, hy_b1, hy_w2, hy_b2, hy_w3, hy_freq, hy_skip, fn_w,
           final_g):
    n, d = x.shape[1], x.shape[2]
    nc = ctx.shape[1]
    x0 = x[0]
    ctx0 = ctx[0]
    vec = lambda a: a.reshape(1, -1)

    cond = jnp.zeros((8, d), F32).at[0].set(c[0]).at[1].set(c_ctx)
    mod = _modulation(cond, w_mod, b_mod)
    shift0, scale0, gate0 = (mod[0, 0:1, k * d:(k + 1) * d] for k in range(3))
    cshift0, cscale0 = mod[0, 1:2, 0:d], mod[0, 1:2, d:2 * d]
    shift1, scale1, gate1 = (mod[1, 0:1, k * d:(k + 1) * d] for k in range(3))

    w_in0 = ev_w_in[0].astype(BF16)
    cos, sin = _rope_tables(n)
    g0 = vec(norm_g[0])
    qg, kg = vec(q_norm_g[0]), vec(k_norm_g[0])
    a_val, a_gate, q, k, v, b_gate = _even_in(x0, g0, scale0, shift0, w_in0, qg, kg, cos, sin)
    ones = jnp.ones((nc, HEAD_DIM), F32)
    _, _, _, ck, cv, _ = _even_in(ctx0, g0, cscale0, cshift0, w_in0, qg, kg, ones, jnp.zeros_like(ones))
    yb = _attention(q, jnp.concatenate([ck, k], axis=0), jnp.concatenate([cv, v], axis=0), b_gate)
    pool_bd = jax.scipy.linalg.block_diag(*[pool_w[0, gi] for gi in range(pool_w.shape[1])]).astype(BF16)
    x1 = _even_out(a_val, a_gate, yb, x0, gate0, pool_bd, vec(pool_scale[0]), ev_w_out[0].astype(BF16))

    ch = np.arange(FN_WIDTH, dtype=np.float64)
    ph = 2.0 * np.pi * np.outer(ch, ch) / FN_WIDTH
    cs = _table_parts(np.concatenate([np.cos(ph), np.sin(ph)], axis=1))
    uc, hy_gate, pq, fn_gate = _odd_in(x1, vec(norm_g[1]), scale1, shift1, od_w_in[0].astype(BF16),
                                       hy_conv_w[0], vec(hy_conv_b[0]), cs)

    max_decay = math.log(HY_DECAY_TARGET) / HY_FAST_DECAY
    min_decay = math.log(HY_DECAY_TARGET) / HY_SLOW_DECAY
    deltas = jnp.linspace(min_decay, max_decay, HY_WIDTH, dtype=F32)[None, :]
    pos = np.arange(n)
    emb2 = jnp.asarray(np.concatenate([_hyena_embedding(n, pos),
                                       _hyena_embedding(n, np.where(pos == 0, 0, n - pos))], axis=-1)
                       .astype(np.float32))
    k_lo, k_hi, mass = _hyena_filters(emb2, hy_w1[0], vec(hy_b1[0]), hy_w2[0], vec(hy_b2[0]), hy_w3[0],
                                      vec(hy_freq[0]), deltas)

    n2 = n // DFT_N1
    m_full, _, big_fwd, big_inv = _dft_tables(2 * n, n2, True, 2 * DFT_N1)
    m_half, m_inv, _, _ = _dft_tables(2 * n, n2, True, DFT_N1)
    k_spec = _dft_fwd([k_lo, k_hi], HY_ORDER * HY_WIDTH, m_full, big_fwd, mass=mass, out_dtype=BF16)
    z1 = _hyena_conv(uc, 0, uc, HY_WIDTH, hy_skip[0, 0:1], k_spec, 0, m_half, m_inv, big_fwd, big_inv)
    yc = _hyena_conv(z1, 0, uc, 2 * HY_WIDTH, hy_skip[0, 1:2], k_spec, 1, m_half, m_inv, big_fwd, big_inv)

    f_half, _, f_big, _ = _dft_tables(n, n2, False, DFT_N1)
    pq_spec = _dft_fwd([pq], 2 * FN_WIDTH, f_half, f_big)
    yd = _fn_out(pq_spec, fn_w[0].astype(BF16), n)

    out = _odd_out(yc, hy_gate, yd, fn_gate, x1, gate1, od_w_out[0].astype(BF16), vec(final_g))
    return out[None]
```

```python
import functools
import math

import numpy as np
import jax
import jax.numpy as jnp
from jax import lax
from jax.experimental import pallas as pl
from jax.experimental.pallas import tpu as pltpu

F32 = jnp.float32
BF16 = jnp.bfloat16
HIGHEST = lax.Precision.HIGHEST

EPS = 1e-6
GRID_W = 64
HEAD_DIM = 128
ROPE_FREQS = 32
ROPE_THETA = 10000.0
N_Q_HEADS = 6
N_KV_HEADS = 2
Q_PER_KV = N_Q_HEADS // N_KV_HEADS
POOL_WIDTH = 256
POOL_GROUP_DIM = 64
POOL_WINDOWS = (2, 4, 8, 16)
POOL_HALO = 8
ATT_WIDTH = N_Q_HEADS * HEAD_DIM
KV_WIDTH = N_KV_HEADS * HEAD_DIM
HY_WIDTH = 768
HY_ORDER = 2
HY_EMB = 33
HY_BANDS = 16
HY_HIDDEN = 64
FN_WIDTH = 256
HY_DECAY_TARGET = 1e-2
HY_FAST_DECAY = 0.3
HY_SLOW_DECAY = 1.5

LANES = 128
ROW_TILE = 1024
DFT_N1 = 128
VMEM_LIMIT = 56 * 1024 * 1024
CONV_VMEM_LIMIT = 62 * 1024 * 1024


def _row_tile(n, pref):
    t = min(pref, n)
    assert n % t == 0
    return t


def _silu(x):
    return x * jax.nn.sigmoid(x)


def _params(sem, vmem=None, flags=None):
    return pltpu.CompilerParams(dimension_semantics=sem, vmem_limit_bytes=vmem, flags=flags)


def _const_spec(shape):
    nd = len(shape)
    return pl.BlockSpec(shape, lambda *_: (0,) * nd, pipeline_mode=pl.Buffered(1))


def _mod_kernel(cond_ref, w_ref, b_ref, o_ref):
    s = _silu(cond_ref[...])
    o_ref[0] = jnp.dot(s, w_ref[0], precision=HIGHEST, preferred_element_type=F32) + b_ref[0]


def _modulation(cond, w_mod, b_mod):
    depth, d, d3 = w_mod.shape
    tn = 1024
    return pl.pallas_call(
        _mod_kernel,
        out_shape=jax.ShapeDtypeStruct((depth, 8, d3), F32),
        grid=(depth, d3 // tn),
        in_specs=[pl.BlockSpec((8, d), lambda i, j: (0, 0)),
                  pl.BlockSpec((1, d, tn), lambda i, j: (i, 0, j)),
                  pl.BlockSpec((1, 1, tn), lambda i, j: (i, 0, j))],
        out_specs=pl.BlockSpec((1, 8, tn), lambda i, j: (i, 0, j)),
        compiler_params=_params(("arbitrary", "arbitrary")),
        name="modulation",
    )(cond, w_mod, b_mod.reshape(depth, 1, d3))


def _time_pitch(n):
    return n // DFT_N1 + 8


def _store_pitched(ref, cols, val, run, pitch, first=0):
    for b in range(val.shape[0] // run):
        r0 = (first + b) * pitch
        ref[r0:r0 + run, cols] = val[b * run:(b + 1) * run]
        ref[r0 + run:r0 + pitch, cols] = jnp.zeros((pitch - run, val.shape[1]), val.dtype)


def _load_pitched(ref, run, pitch):
    return jnp.concatenate([ref[b * pitch:b * pitch + run, :] for b in range(ref.shape[0] // pitch)], axis=0)


def _norm_mod(x, g, scale, shift):
    y = x * lax.rsqrt(jnp.mean(x * x, axis=-1, keepdims=True) + EPS)
    return (y * g) * (1.0 + scale) + shift


def _even_in_kernel(x_ref, g_ref, sc_ref, sh_ref, w_ref, qg_ref, kg_ref, cos_ref, sin_ref,
                    aval_ref, agate_ref, q_ref, k_ref, v_ref, bgate_ref):
    half = x_ref.shape[0] // 2
    for s in range(2):
        rs = slice(s * half, (s + 1) * half)
        h = _norm_mod(x_ref[rs, :], g_ref[...], sc_ref[...], sh_ref[...])
        p = jnp.dot(h.astype(BF16), w_ref[...], preferred_element_type=F32)
        aval_ref[rs, :] = p[:, 0:POOL_WIDTH]
        agate_ref[rs, :] = _silu(p[:, POOL_WIDTH:2 * POOL_WIDTH])
        cos = cos_ref[rs, :]
        sin = sin_ref[rs, :]
        lane = lax.broadcasted_iota(jnp.int32, cos.shape, 1)
        low_half = (lane % (2 * ROPE_FREQS)) < ROPE_FREQS

        def head(xh, g):
            y = xh * lax.rsqrt(jnp.mean(xh * xh, axis=-1, keepdims=True) + EPS) * g
            rot = jnp.where(low_half, pltpu.roll(y, HEAD_DIM - ROPE_FREQS, 1), pltpu.roll(y, ROPE_FREQS, 1))
            return y * cos + rot * sin

        q0 = 2 * POOL_WIDTH
        for hq in range(N_Q_HEADS):
            sl = slice(hq * HEAD_DIM, (hq + 1) * HEAD_DIM)
            q_ref[rs, sl] = head(p[:, q0 + hq * HEAD_DIM:q0 + (hq + 1) * HEAD_DIM], qg_ref[...]).astype(BF16)
        k0 = q0 + ATT_WIDTH
        v0 = k0 + KV_WIDTH
        for hk in range(N_KV_HEADS):
            k_ref[hk, rs, :] = head(p[:, k0 + hk * HEAD_DIM:k0 + (hk + 1) * HEAD_DIM], kg_ref[...]).astype(BF16)
            v_ref[hk, rs, :] = p[:, v0 + hk * HEAD_DIM:v0 + (hk + 1) * HEAD_DIM].astype(BF16)
        b0 = v0 + KV_WIDTH
        bgate_ref[rs, :] = _silu(p[:, b0:b0 + ATT_WIDTH]).astype(bgate_ref.dtype)


def _even_in(x, g, scale, shift, w_bf, qg, kg, cos, sin):
    n, d = x.shape
    n_in = w_bf.shape[1]
    tm = _row_tile(n, ROW_TILE // 2)
    row = lambda w: pl.BlockSpec((tm, w), lambda i: (i, 0))
    vec = lambda w: pl.BlockSpec((1, w), lambda i: (0, 0))
    kv_shape = jax.ShapeDtypeStruct((N_KV_HEADS, n, HEAD_DIM), BF16)
    kv_spec = pl.BlockSpec((N_KV_HEADS, tm, HEAD_DIM), lambda i: (0, i, 0))
    return pl.pallas_call(
        _even_in_kernel,
        out_shape=(jax.ShapeDtypeStruct((n, POOL_WIDTH), F32), jax.ShapeDtypeStruct((n, POOL_WIDTH), F32),
                   jax.ShapeDtypeStruct((n, ATT_WIDTH), BF16), kv_shape, kv_shape,
                   jax.ShapeDtypeStruct((n, ATT_WIDTH), BF16)),
        grid=(n // tm,),
        in_specs=[row(d), vec(d), vec(d), vec(d), _const_spec((d, n_in)), vec(HEAD_DIM), vec(HEAD_DIM),
                  row(HEAD_DIM), row(HEAD_DIM)],
        out_specs=(row(POOL_WIDTH), row(POOL_WIDTH), row(ATT_WIDTH), kv_spec, kv_spec, row(ATT_WIDTH)),
        compiler_params=_params(("parallel",), VMEM_LIMIT),
        name="even_in",
    )(x, g, scale, shift, w_bf, qg, kg, cos, sin)


ATT_UNROLL = 4
ATT_ROW_CHUNK = 16


def _attn_kernel(q_ref, k_ref, v_ref, bg_ref, o_ref, q_sc, s_sc, p_sc, m_sc, al_sc, acc_sc,
                 *, tq, tk, n_kv_tiles):
    c = (HEAD_DIM ** -0.5) * math.log2(math.e)
    rows = Q_PER_KV * tq
    total = N_KV_HEADS * n_kv_tiles
    for h in range(N_KV_HEADS):
        for g in range(Q_PER_KV):
            cs = slice((Q_PER_KV * h + g) * HEAD_DIM, (Q_PER_KV * h + g + 1) * HEAD_DIM)
            q_sc[h, g * tq:(g + 1) * tq, :] = q_ref[:, cs]

    def head_tile(v):
        if isinstance(v, int):
            return v // n_kv_tiles, (v % n_kv_tiles) * tk
        h = v // n_kv_tiles
        return h, pl.multiple_of((v - h * n_kv_tiles) * tk, tk)

    def scores(buf, v):
        h, r0 = head_tile(v)
        s_sc[buf] = lax.dot_general(q_sc[h], k_ref[h, pl.ds(r0, tk), :], (((1,), (1,)), ((), ())),
                                    preferred_element_type=F32)

    def softmax(buf, v):
        h, _ = head_tile(v)
        for r in range(rows // ATT_ROW_CHUNK):
            rs = slice(r * ATT_ROW_CHUNK, (r + 1) * ATT_ROW_CHUNK)
            cols = [s_sc[buf, rs, j * LANES:(j + 1) * LANES] for j in range(tk // LANES)]
            mx = functools.reduce(jnp.maximum, cols)
            m_old = m_sc[h, rs, :]
            m_new = jnp.maximum(m_old, jnp.max(mx, axis=-1, keepdims=True) * c)
            for j, sj in enumerate(cols):
                p_sc[buf, rs, j * LANES:(j + 1) * LANES] = jnp.exp2(sj * c - m_new).astype(BF16)
            al_sc[buf, rs, :] = jnp.exp2(m_old - m_new)
            m_sc[h, rs, :] = m_new

    def accumulate(buf, v):
        h, r0 = head_tile(v)
        v_ext = jnp.concatenate([v_ref[h, pl.ds(r0, tk), :], jnp.ones((tk, HEAD_DIM), BF16)], axis=1)
        pv = jnp.dot(p_sc[buf], v_ext, preferred_element_type=F32)
        al = al_sc[buf]
        for j in range(2):
            js = slice(j * HEAD_DIM, (j + 1) * HEAD_DIM)
            acc_sc[h, :, js] = al * acc_sc[h, :, js] + pv[:, js]

    def slot(v, buf, with_scores=True):
        if with_scores:
            scores(1 - buf, v + 1)
        softmax(buf, v)
        accumulate(buf, v)

    m_sc[...] = jnp.full(m_sc.shape, -1e30, F32)
    acc_sc[...] = jnp.zeros(acc_sc.shape, F32)
    scores(0, 0)

    def body(i, carry):
        for u in range(ATT_UNROLL):
            slot(ATT_UNROLL * i + u, u % 2)
        return carry

    n_full = (total - 1) // ATT_UNROLL
    lax.fori_loop(0, n_full, body, 0)
    for v in range(ATT_UNROLL * n_full, total):
        slot(v, v % 2, with_scores=v + 1 < total)
    for h in range(N_KV_HEADS):
        out = acc_sc[h, :, 0:HEAD_DIM] / acc_sc[h, :, HEAD_DIM:2 * HEAD_DIM]
        for g in range(Q_PER_KV):
            cs = slice((Q_PER_KV * h + g) * HEAD_DIM, (Q_PER_KV * h + g + 1) * HEAD_DIM)
            o_ref[:, cs] = (out[g * tq:(g + 1) * tq] * bg_ref[:, cs]).astype(o_ref.dtype)


ATT_KEY_TILE = 1280


def _attention(q, k_all, v_all, bgate):
    n = q.shape[0]
    nk = k_all.shape[1]
    tq = _row_tile(n, 256)
    tk = max(t for t in range(LANES, ATT_KEY_TILE + 1, LANES) if nk % t == 0)
    rows = Q_PER_KV * tq
    kern = functools.partial(_attn_kernel, tq=tq, tk=tk, n_kv_tiles=nk // tk)
    return pl.pallas_call(
        kern,
        out_shape=jax.ShapeDtypeStruct((n, ATT_WIDTH), BF16),
        grid=(n // tq,),
        in_specs=[pl.BlockSpec((tq, ATT_WIDTH), lambda i: (i, 0)),
                  _const_spec((N_KV_HEADS, nk, HEAD_DIM)), _const_spec((N_KV_HEADS, nk, HEAD_DIM)),
                  pl.BlockSpec((tq, ATT_WIDTH), lambda i: (i, 0))],
        out_specs=pl.BlockSpec((tq, ATT_WIDTH), lambda i: (i, 0)),
        scratch_shapes=[pltpu.VMEM((N_KV_HEADS, rows, HEAD_DIM), BF16),
                        pltpu.VMEM((2, rows, tk), F32), pltpu.VMEM((2, rows, tk), BF16),
                        pltpu.VMEM((N_KV_HEADS, rows, LANES), F32), pltpu.VMEM((2, rows, LANES), F32),
                        pltpu.VMEM((N_KV_HEADS, rows, 2 * HEAD_DIM), F32)],
        compiler_params=_params(("parallel",), VMEM_LIMIT),
        name="attention",
    )(q, k_all, v_all, bgate)


def _even_out_kernel(a_ref, ap_ref, an_ref, ag_ref, yb_ref, x_ref, gate_ref, pw_ref, ps_ref, wo_ref, o_ref,
                     *, tm, n_rows):
    i = pl.program_id(0)
    n_tiles = pl.num_programs(0)
    u = a_ref[...]
    prev = jnp.where(i > 0, ap_ref[...], 0.0)
    nxt = jnp.where(i < n_tiles - 1, an_ref[...], 0.0)
    e = jnp.concatenate([prev, u, nxt], axis=0)
    n = tm + 2 * POOL_HALO
    s2 = e[0:n - 1] + e[1:n]
    s4 = s2[0:n - 3] + s2[2:n - 1]
    s8 = s4[0:n - 7] + s4[4:n - 3]
    s16 = s8[0:n - 15] + s8[8:n - 7]
    lane = lax.broadcasted_iota(jnp.int32, (tm, POOL_WIDTH), 1)
    grp = lane // POOL_GROUP_DIM
    win = jnp.where(grp == 0, s2[7:7 + tm],
                    jnp.where(grp == 1, s4[6:6 + tm], jnp.where(grp == 2, s8[4:4 + tm], s16[0:tm])))
    half = jnp.where(grp == 0, 1, jnp.where(grp == 1, 2, jnp.where(grp == 2, 4, 8)))
    t = i * tm + lax.broadcasted_iota(jnp.int32, (tm, POOL_WIDTH), 0)
    cnt = (jnp.minimum(t + half, n_rows) - jnp.maximum(t - half, 0)).astype(F32)
    d = win / cnt - u
    ya = jnp.dot(d.astype(BF16), pw_ref[...], preferred_element_type=F32) * ps_ref[...]
    ya = ya * ag_ref[...]
    y = jnp.dot(ya.astype(BF16), wo_ref[0:POOL_WIDTH, :], preferred_element_type=F32)
    y = y + jnp.dot(yb_ref[...].astype(BF16), wo_ref[POOL_WIDTH:, :], preferred_element_type=F32)
    o_ref[...] = x_ref[...] + gate_ref[...] * y


def _even_out(a_val, a_gate, yb, x, gate, pool_bd, pool_scale, w_out_bf):
    n, d = x.shape
    tm = _row_tile(n, ROW_TILE)
    hb = tm // POOL_HALO
    last = n // POOL_HALO - 1
    row = lambda w: pl.BlockSpec((tm, w), lambda i: (i, 0))
    vec = lambda w: pl.BlockSpec((1, w), lambda i: (0, 0))
    kern = functools.partial(_even_out_kernel, tm=tm, n_rows=n)
    return pl.pallas_call(
        kern,
        out_shape=jax.ShapeDtypeStruct((n, d), F32),
        grid=(n // tm,),
        in_specs=[row(POOL_WIDTH),
                  pl.BlockSpec((POOL_HALO, POOL_WIDTH), lambda i: (jnp.maximum(i * hb - 1, 0), 0)),
                  pl.BlockSpec((POOL_HALO, POOL_WIDTH), lambda i: (jnp.minimum((i + 1) * hb, last), 0)),
                  row(POOL_WIDTH), row(ATT_WIDTH), row(d), vec(d),
                  _const_spec((POOL_WIDTH, POOL_WIDTH)), vec(POOL_WIDTH), _const_spec((d, d))],
        out_specs=row(d),
        compiler_params=_params(("parallel",), VMEM_LIMIT),
        name="even_out",
    )(a_val, a_val, a_val, a_gate, yb, x, gate, pool_bd, pool_scale, w_out_bf)


def _odd_in_kernel(x_ref, xp_ref, xn_ref, g_ref, sc_ref, sh_ref, w_ref, cw_ref, cb_ref, cs_ref,
                   uc_ref, hyg_ref, pq_ref, fng_ref, *, tm, run, pitch):
    i = pl.program_id(0)
    n_tiles = pl.num_programs(0)
    g, sc, sh = g_ref[...], sc_ref[...], sh_ref[...]
    c0 = (HY_ORDER + 1) * HY_WIDTH
    halo = _norm_mod(jnp.concatenate([xp_ref[...], xn_ref[...]], axis=0), g, sc, sh)
    ph = jnp.dot(halo.astype(BF16), w_ref[:, 0:c0], preferred_element_type=F32)
    half = tm // 2
    ps = [jnp.dot(_norm_mod(x_ref[s * half:(s + 1) * half, :], g, sc, sh).astype(BF16), w_ref[...],
                  preferred_element_type=F32) for s in range(2)]
    edge = [jnp.where(i > 0, ph[7:8, :], 0.0), ps[0][half - 1:half, 0:c0], ps[1][0:1, 0:c0],
            jnp.where(i < n_tiles - 1, ph[8:9, :], 0.0)]
    for s, p in enumerate(ps):
        rs = slice(s * half, (s + 1) * half)
        u = p[:, 0:c0]
        e = jnp.concatenate([edge[s], u, edge[s + 2]], axis=0)
        uc = e[0:half] * cw_ref[0:1, :] + u * cw_ref[1:2, :] + e[2:half + 2] * cw_ref[2:3, :] + cb_ref[...]
        _store_pitched(uc_ref, slice(None), uc, run, pitch, first=s * (half // run))
        hyg_ref[rs, :] = _silu(p[:, c0:c0 + HY_WIDTH]).astype(hyg_ref.dtype)
        fn_in = p[:, c0 + HY_WIDTH:c0 + HY_WIDTH + FN_WIDTH]
        _store_pitched(pq_ref, slice(None), _dot_parts(_split_bf16(fn_in), _ref_parts(cs_ref)), run, pitch,
                       first=s * (half // run))
        fng_ref[rs, :] = _silu(p[:, c0 + HY_WIDTH + FN_WIDTH:])


def _odd_in(x, g, scale, shift, w_bf, conv_w, conv_b, cs):
    n, d = x.shape
    n_in = w_bf.shape[1]
    tm = _row_tile(n, ROW_TILE // 2)
    hb = tm // 8
    last = n // 8 - 1
    row = lambda w: pl.BlockSpec((tm, w), lambda i: (i, 0))
    vec = lambda w: pl.BlockSpec((1, w), lambda i: (0, 0))
    c0 = (HY_ORDER + 1) * HY_WIDTH
    run, pitch = n // DFT_N1, _time_pitch(n)
    assert tm % run == 0
    tmp = tm // run * pitch
    prow = lambda w: pl.BlockSpec((tmp, w), lambda i: (i, 0))
    kern = functools.partial(_odd_in_kernel, tm=tm, run=run, pitch=pitch)
    return pl.pallas_call(
        kern,
        out_shape=(jax.ShapeDtypeStruct((DFT_N1 * pitch, c0), F32), jax.ShapeDtypeStruct((n, HY_WIDTH), BF16),
                   jax.ShapeDtypeStruct((DFT_N1 * pitch, 2 * FN_WIDTH), F32),
                   jax.ShapeDtypeStruct((n, FN_WIDTH), F32)),
        grid=(n // tm,),
        in_specs=[row(d),
                  pl.BlockSpec((8, d), lambda i: (jnp.maximum(i * hb - 1, 0), 0)),
                  pl.BlockSpec((8, d), lambda i: (jnp.minimum((i + 1) * hb, last), 0)),
                  vec(d), vec(d), vec(d), _const_spec((d, n_in)),
                  pl.BlockSpec((3, c0), lambda i: (0, 0)), vec(c0), _const_spec(cs.shape)],
        out_specs=(prow(c0), row(HY_WIDTH), prow(2 * FN_WIDTH), row(FN_WIDTH)),
        compiler_params=_params(("parallel",), VMEM_LIMIT),
        name="odd_in",
    )(x, x, x, g, scale, shift, w_bf, conv_w, conv_b, cs)


def _filter_kernel(emb_ref, w1_ref, b1_ref, w2_ref, b2_ref, w3f_ref, w3b_ref, fr_ref, dl_ref,
                   lo_ref, hi_ref, mass_ref, *, tm, n_rows, run, pitch):
    i = pl.program_id(0)
    fr = fr_ref[...]
    a = jnp.sin(fr * (jnp.dot(emb_ref[...], w1_ref[...], precision=HIGHEST, preferred_element_type=F32)
                      + b1_ref[...]))
    a = jnp.sin(fr * (jnp.dot(a, w2_ref[...], precision=HIGHEST, preferred_element_type=F32) + b2_ref[...]))
    a_f = a_b = a.astype(BF16)
    j = i * tm + lax.broadcasted_iota(jnp.int32, (tm, 1), 0)
    pos_b = jnp.where(j == 0, 0, n_rows - j)
    dl = jnp.abs(dl_ref[...])
    inv = 1.0 / (n_rows - 1)
    decay_f = jnp.exp(-(j.astype(F32) * inv) * dl)
    decay_b = jnp.exp(-(pos_b.astype(F32) * inv) * dl)

    @pl.when(i == 0)
    def _():
        mass_ref[...] = jnp.zeros_like(mass_ref)

    for o in range(HY_ORDER):
        os_ = slice(o * HY_WIDTH, (o + 1) * HY_WIDTH)
        hf = jnp.dot(a_f, w3f_ref[:, os_], preferred_element_type=F32) * decay_f
        hb = jnp.dot(a_b, w3b_ref[:, os_], preferred_element_type=F32) * decay_b
        _store_pitched(lo_ref, os_, hf, run, pitch)
        _store_pitched(hi_ref, os_, jnp.where(j == 0, 0.0, -hb), run, pitch)
        mass_ref[:, os_] += (jnp.sum(jnp.abs(hf), axis=0, keepdims=True)
                             + jnp.sum(jnp.abs(hb), axis=0, keepdims=True))


def _hyena_filters(emb2, w1, b1, w2, b2, w3, freq, deltas):
    n = emb2.shape[0]
    tm = _row_tile(n, ROW_TILE)
    nh = HY_ORDER * HY_WIDTH
    hh = HY_HIDDEN
    w1b = jnp.zeros((LANES, 2 * hh), F32).at[0:HY_EMB, 0:hh].set(w1).at[hh:hh + HY_EMB, hh:].set(w1)
    w2b = jnp.zeros((2 * hh, 2 * hh), F32).at[0:hh, 0:hh].set(w2).at[hh:, hh:].set(w2)
    w3r = w3.reshape(hh, HY_ORDER, 2, HY_WIDTH)
    zero = jnp.zeros((hh, nh), F32)
    w3f = jnp.concatenate([w3r[:, :, 0, :].reshape(hh, nh), zero], axis=0).astype(BF16)
    w3b = jnp.concatenate([zero, w3r[:, :, 1, :].reshape(hh, nh)], axis=0).astype(BF16)
    twice = lambda a: jnp.concatenate([a, a], axis=-1)
    ops = (emb2, w1b, twice(b1), w2b, twice(b2), w3f, w3b, twice(freq), deltas)
    run, pitch = n // DFT_N1, _time_pitch(n)
    assert tm % run == 0
    prow = pl.BlockSpec((tm // run * pitch, nh), lambda i: (i, 0))
    kern = functools.partial(_filter_kernel, tm=tm, n_rows=n, run=run, pitch=pitch)
    full = lambda a: pl.BlockSpec(a.shape, lambda i: (0,) * a.ndim)
    row = lambda w: pl.BlockSpec((tm, w), lambda i: (i, 0))
    return pl.pallas_call(
        kern,
        out_shape=(jax.ShapeDtypeStruct((DFT_N1 * pitch, nh), F32), jax.ShapeDtypeStruct((DFT_N1 * pitch, nh), F32),
                   jax.ShapeDtypeStruct((1, nh), F32)),
        grid=(n // tm,),
        in_specs=[row(LANES)] + [full(a) for a in ops[1:]],
        out_specs=(prow, prow, pl.BlockSpec((1, nh), lambda i: (0, 0))),
        compiler_params=_params(("arbitrary",), VMEM_LIMIT),
        name="hyena_filters",
    )(*ops)


DFT_PARTS = 1


def _table_parts(a):
    a32 = jnp.asarray(a.astype(np.float32))
    hi = a32.astype(BF16)
    if DFT_PARTS == 1:
        return hi[None]
    lo = (a32 - hi.astype(F32)).astype(BF16)
    return jnp.stack([hi, lo])


def _split_bf16(x):
    hi = x.astype(BF16)
    if DFT_PARTS == 1:
        return (hi,)
    return (hi, (x - hi.astype(F32)).astype(BF16))


def _ref_parts(ref, *idx):
    return tuple(ref[(q,) + idx] for q in range(ref.shape[0]))


def _dot_parts(lhs, rhs):
    acc = jnp.dot(lhs[0], rhs[0], preferred_element_type=F32)
    if len(lhs) > 1:
        acc = acc + jnp.dot(lhs[1], rhs[0], preferred_element_type=F32)
    if len(rhs) > 1:
        acc = acc + jnp.dot(lhs[0], rhs[1], preferred_element_type=F32)
    return acc


def _dft_tables(n_total, n2, half_shift, n1_used):
    n1_full = n_total // n2
    sh = 0.5 if half_shift else 0.0
    k1 = np.arange(DFT_N1, dtype=np.float64)[None, :, None] + sh
    n1 = np.arange(n1_used, dtype=np.float64)[None, None, :]
    nn2 = np.arange(n2, dtype=np.float64)[:, None, None]
    ph = -2.0 * np.pi * (n1 * k1 / n1_full + nn2 * k1 / n_total)
    fwd = np.stack([np.cos(ph), np.sin(ph)], axis=2).reshape(n2, 2 * DFT_N1, n1_used)
    ph_i = ph[:, :, :DFT_N1]
    inv = np.stack([np.cos(ph_i), np.sin(ph_i)], axis=2).reshape(n2, 2 * DFT_N1, DFT_N1)
    inv = (2.0 / n_total) * inv.transpose(0, 2, 1)
    k2 = np.arange(n2, dtype=np.float64)
    ph2 = -2.0 * np.pi * np.outer(k2, k2) / n2
    fr, fi = np.cos(ph2), np.sin(ph2)
    big_fwd = np.block([[fr, -fi], [fi, fr]])
    big_inv = np.block([[fr, fi], [-fi, fr]])
    return _table_parts(fwd), _table_parts(inv), _table_parts(big_fwd), _table_parts(big_inv)


DFT_STEP = 32
DFT_FWD_STEP = 64


def _stage1_fwd(x_refs, m_ref, a_sc, step, *, pitch):
    for u in range(m_ref.shape[1]):
        nn = step * m_ref.shape[1] + u
        xs = [xr[pl.ds(nn, DFT_N1, stride=pitch), :] for xr in x_refs]
        xs = xs[0] if len(xs) == 1 else jnp.concatenate(xs, axis=0)
        a_sc[pl.ds(nn, 2 * DFT_N1, stride=pitch), :] = _dot_parts(_ref_parts(m_ref, u), _split_bf16(xs))


def _load_cplx(a_sc, k1, *, n2, pitch):
    r0 = pl.multiple_of(k1 * 2 * pitch, 8)
    return jnp.concatenate([a_sc[pl.ds(r0, n2), :], a_sc[pl.ds(r0 + pitch, n2), :]], axis=0)


def _phase_steps(n2, step=None):
    step = step or DFT_STEP
    n2g = min(step, n2)
    return n2g, n2 // n2g, DFT_N1 // step


def _dft_fwd_kernel(*refs, n_x, n2, pitch, ja, step, scaled):
    x_refs = refs[:n_x]
    rest = refs[n_x:]
    if scaled:
        mass_ref, rest = rest[0], rest[1:]
    m_ref, big_ref, o_ref, a_sc = rest
    j = pl.program_id(1)

    @pl.when(j < ja)
    def _():
        _stage1_fwd(x_refs, m_ref, a_sc, j, pitch=pitch)

    @pl.when(j >= ja)
    def _():
        big = _ref_parts(big_ref)
        for g in range(step):
            blk = _load_cplx(a_sc, (j - ja) * step + g, n2=n2, pitch=pitch)
            out = _dot_parts(big, _split_bf16(blk))
            if scaled:
                out = out * (1.0 / (mass_ref[...] + EPS))
            o_ref[g * 2 * n2:(g + 1) * 2 * n2, :] = out.astype(o_ref.dtype)


def _dft_fwd(xs, width, m_fwd, big, mass=None, out_dtype=F32):
    n2 = xs[0].shape[0] // DFT_N1 - 8
    pitch = n2 + 8
    n = DFT_N1 * pitch
    n2g, ja, jb = _phase_steps(n2, DFT_FWD_STEP)
    wt = LANES
    rows = 2 * DFT_N1 * n2
    kern = functools.partial(_dft_fwd_kernel, n_x=len(xs), n2=n2, pitch=pitch, ja=ja, step=DFT_FWD_STEP,
                             scaled=mass is not None)
    x_spec = pl.BlockSpec((n, wt), lambda i, j: (0, i), pipeline_mode=pl.Buffered(1))
    ops = list(xs)
    specs = [x_spec] * len(xs)
    if mass is not None:
        ops.append(mass)
        specs.append(pl.BlockSpec((1, wt), lambda i, j: (0, i)))
    m_spec = pl.BlockSpec((m_fwd.shape[0], n2g) + m_fwd.shape[2:],
                          lambda i, j: (0, jnp.minimum(j, ja - 1), 0, 0))
    return pl.pallas_call(
        kern,
        out_shape=jax.ShapeDtypeStruct((rows, width), out_dtype),
        grid=(width // wt, ja + jb),
        in_specs=specs + [m_spec, _const_spec(big.shape)],
        out_specs=pl.BlockSpec((DFT_FWD_STEP * 2 * n2, wt), lambda i, j: (jnp.maximum(j - ja, 0), i)),
        scratch_shapes=[pltpu.VMEM((2 * DFT_N1 * pitch, wt), F32)],
        compiler_params=_params(("parallel", "arbitrary"), CONV_VMEM_LIMIT),
        name="dft_fwd",
    )(*ops, m_fwd, big)


def _hyena_conv_kernel(x_ref, g_ref, s_ref, k_ref, m_ref, minv_ref, bigf_ref, bigi_ref, y_ref, a_sc,
                       *, n2, pitch, ja, jb):
    j = pl.program_id(1)

    @pl.when(j < ja)
    def _():
        _stage1_fwd([x_ref], m_ref, a_sc, j, pitch=pitch)

    @pl.when((j >= ja) & (j < ja + jb))
    def _():
        bigf = _ref_parts(bigf_ref)
        bigi = _ref_parts(bigi_ref)
        for g in range(DFT_STEP):
            k1 = (j - ja) * DFT_STEP + g
            z = _dot_parts(bigf, _split_bf16(_load_cplx(a_sc, k1, n2=n2, pitch=pitch)))
            zr, zi = z[0:n2], z[n2:2 * n2]
            kr = k_ref[g * 2 * n2:g * 2 * n2 + n2, :].astype(F32)
            ki = k_ref[g * 2 * n2 + n2:(g + 1) * 2 * n2, :].astype(F32)
            prod = jnp.concatenate([zr * kr - zi * ki, zr * ki + zi * kr], axis=0)
            b = _dot_parts(bigi, _split_bf16(prod))
            r0 = pl.multiple_of(k1 * 2 * pitch, 8)
            a_sc[pl.ds(r0, n2), :] = b[0:n2]
            a_sc[pl.ds(r0 + pitch, n2), :] = b[n2:2 * n2]

    @pl.when(j == ja + jb)
    def _():
        for r in range(pitch - n2):
            y_ref[pl.ds(n2 + r, DFT_N1, stride=pitch), :] = jnp.zeros((DFT_N1, y_ref.shape[1]), F32)

    @pl.when(j >= ja + jb)
    def _():
        skip = s_ref[...]
        for u in range(minv_ref.shape[1]):
            nn = (j - ja - jb) * minv_ref.shape[1] + u
            bs = a_sc[pl.ds(nn, 2 * DFT_N1, stride=pitch), :]
            conv = _dot_parts(_ref_parts(minv_ref, u), _split_bf16(bs))
            rows = pl.ds(nn, DFT_N1, stride=pitch)
            y_ref[rows, :] = g_ref[rows, :] * (conv + skip * x_ref[rows, :])


def _hyena_conv(x, x_col, gate, gate_col, skip, kspec, order, m_fwd, m_inv, big_fwd, big_inv):
    n2 = x.shape[0] // DFT_N1 - 8
    pitch = n2 + 8
    n = DFT_N1 * pitch
    n2g, ja, jb = _phase_steps(n2)
    wt = LANES
    c0 = x_col // wt
    cg = gate_col // wt
    ck = order * HY_WIDTH // wt
    kern = functools.partial(_hyena_conv_kernel, n2=n2, pitch=pitch, ja=ja, jb=jb)
    return pl.pallas_call(
        kern,
        out_shape=jax.ShapeDtypeStruct((n, HY_WIDTH), F32),
        grid=(HY_WIDTH // wt, ja + jb + ja),
        in_specs=[pl.BlockSpec((n, wt), lambda i, j: (0, c0 + i), pipeline_mode=pl.Buffered(1)),
                  pl.BlockSpec((n, wt), lambda i, j: (0, cg + i), pipeline_mode=pl.Buffered(1)),
                  pl.BlockSpec((1, wt), lambda i, j: (0, i)),
                  pl.BlockSpec((DFT_STEP * 2 * n2, wt), lambda i, j: (jnp.clip(j - ja, 0, jb - 1), ck + i)),
                  pl.BlockSpec((m_fwd.shape[0], n2g) + m_fwd.shape[2:],
                               lambda i, j: (0, jnp.minimum(j, ja - 1), 0, 0)),
                  pl.BlockSpec((m_inv.shape[0], n2g) + m_inv.shape[2:],
                               lambda i, j: (0, jnp.clip(j - ja - jb, 0, ja - 1), 0, 0)),
                  _const_spec(big_fwd.shape), _const_spec(big_inv.shape)],
        out_specs=pl.BlockSpec((n, wt), lambda i, j: (0, i), pipeline_mode=pl.Buffered(1)),
        scratch_shapes=[pltpu.VMEM((2 * DFT_N1 * pitch, wt), F32)],
        compiler_params=_params(("parallel", "arbitrary"), CONV_VMEM_LIMIT),
        name="hyena_conv",
    )(x, gate, skip, kspec, m_fwd, m_inv, big_fwd, big_inv)


def _fn_out_kernel(p_ref, q_ref, w_ref, o_ref, *, kb, norm):
    w = w_ref[...]
    for jj in range(kb):
        r = (p_ref[:, 0, jj, :] + q_ref[:, 0, jj, :]) * norm
        o_ref[jj] = jnp.dot(r.astype(BF16), w, preferred_element_type=F32)


def _fn_out(spec, fn_w_bf, n):
    n2 = n // DFT_N1
    c = FN_WIDTH
    kb = 8
    s4 = spec.reshape(DFT_N1, 2, n2, 2 * c)
    kern = functools.partial(_fn_out_kernel, kb=kb, norm=1.0 / math.sqrt(n * c))
    out = pl.pallas_call(
        kern,
        out_shape=jax.ShapeDtypeStruct((n2, DFT_N1, c), F32),
        grid=(n2 // kb,),
        in_specs=[pl.BlockSpec((DFT_N1, 1, kb, c), lambda i: (0, 0, i, 0)),
                  pl.BlockSpec((DFT_N1, 1, kb, c), lambda i: (0, 1, i, 1)),
                  _const_spec((c, c))],
        out_specs=pl.BlockSpec((kb, DFT_N1, c), lambda i: (i, 0, 0)),
        compiler_params=_params(("parallel",), VMEM_LIMIT),
        name="fn_out",
    )(s4, s4, fn_w_bf)
    return out.reshape(n, c)


def _odd_out_kernel(yc_ref, hyg_ref, yd_ref, fng_ref, x_ref, gate_ref, wo_ref, fg_ref, o_ref, *, run, pitch):
    yc = _load_pitched(yc_ref, run, pitch)
    y = jnp.dot((yc * hyg_ref[...]).astype(BF16), wo_ref[0:HY_WIDTH, :], preferred_element_type=F32)
    y = y + jnp.dot((yd_ref[...] * fng_ref[...]).astype(BF16), wo_ref[HY_WIDTH:, :],
                    preferred_element_type=F32)
    xo = x_ref[...] + gate_ref[...] * y
    o_ref[...] = xo * lax.rsqrt(jnp.mean(xo * xo, axis=-1, keepdims=True) + EPS) * fg_ref[...]


def _odd_out(yc, hy_gate, yd, fn_gate, x, gate, w_out_bf, final_g):
    n, d = x.shape
    tm = _row_tile(n, ROW_TILE)
    run, pitch = n // DFT_N1, _time_pitch(n)
    row = lambda w: pl.BlockSpec((tm, w), lambda i: (i, 0))
    vec = lambda w: pl.BlockSpec((1, w), lambda i: (0, 0))
    kern = functools.partial(_odd_out_kernel, run=run, pitch=pitch)
    return pl.pallas_call(
        kern,
        out_shape=jax.ShapeDtypeStruct((n, d), F32),
        grid=(n // tm,),
        in_specs=[pl.BlockSpec((tm // run * pitch, HY_WIDTH), lambda i: (i, 0)),
                  row(HY_WIDTH), row(FN_WIDTH), row(FN_WIDTH), row(d), vec(d), _const_spec((d, d)), vec(d)],
        out_specs=row(d),
        compiler_params=_params(("parallel",), VMEM_LIMIT),
        name="odd_out",
    )(yc, hy_gate, yd, fn_gate, x, gate, w_out_bf, final_g)


def _rope_tables(n):
    rows = n // GRID_W
    inv_freq = ROPE_THETA ** (-np.arange(ROPE_FREQS, dtype=np.float64) / ROPE_FREQS)
    ang_r = np.arange(rows, dtype=np.float64)[:, None] * inv_freq
    ang_c = np.arange(GRID_W, dtype=np.float64)[:, None] * inv_freq

    def table(fn, sign):
        r = np.broadcast_to(fn(ang_r)[:, None, :], (rows, GRID_W, ROPE_FREQS))
        c = np.broadcast_to(fn(ang_c)[None, :, :], (rows, GRID_W, ROPE_FREQS))
        return jnp.asarray(np.concatenate([sign * r, r, sign * c, c], axis=-1).reshape(n, HEAD_DIM)
                           .astype(np.float32))

    return table(np.cos, 1.0), table(np.sin, -1.0)


def _hyena_embedding(n, pos):
    pos = pos.astype(np.float64)
    t = (pos / (n - 1))[:, None]
    w = 2.0 * np.pi * pos[:, None] / n
    f = np.linspace(1e-4, HY_BANDS - 1, HY_BANDS)[None, :]
    emb = np.concatenate([t, np.cos(f * w), -np.sin(f * w)], axis=-1)
    return np.pad(emb, ((0, 0), (0, HY_HIDDEN - HY_EMB)))


def kernel(x, c, ctx, c_ctx, w_mod, b_mod, norm_g, ev_w_in, ev_w_out, pool_w, pool_scale, q_norm_g, k_norm_g,
           od_w_in, od_w_out, hy_conv_w, hy_conv_b, hy_w1, hy_b1, hy_w2, hy_b2, hy_w3, hy_freq, hy_skip, fn_w,
           final_g):
    n, d = x.shape[1], x.shape[2]
    nc = ctx.shape[1]
    x0 = x[0]
    ctx0 = ctx[0]
    vec = lambda a: a.reshape(1, -1)

    cond = jnp.zeros((8, d), F32).at[0].set(c[0]).at[1].set(c_ctx)
    mod = _modulation(cond, w_mod, b_mod)
    shift0, scale0, gate0 = (mod[0, 0:1, k * d:(k + 1) * d] for k in range(3))
    cshift0, cscale0 = mod[0, 1:2, 0:d], mod[0, 1:2, d:2 * d]
    shift1, scale1, gate1 = (mod[1, 0:1, k * d:(k + 1) * d] for k in range(3))

    w_in0 = ev_w_in[0].astype(BF16)
    cos, sin = _rope_tables(n)
    g0 = vec(norm_g[0])
    qg, kg = vec(q_norm_g[0]), vec(k_norm_g[0])
    a_val, a_gate, q, k, v, b_gate = _even_in(x0, g0, scale0, shift0, w_in0, qg, kg, cos, sin)
    ones = jnp.ones((nc, HEAD_DIM), F32)
    _, _, _, ck, cv, _ = _even_in(ctx0, g0, cscale0, cshift0, w_in0, qg, kg, ones, jnp.zeros_like(ones))
    yb = _attention(q, jnp.concatenate([ck, k], axis=1), jnp.concatenate([cv, v], axis=1), b_gate)
    pool_bd = jax.scipy.linalg.block_diag(*[pool_w[0, gi] for gi in range(pool_w.shape[1])]).astype(BF16)
    x1 = _even_out(a_val, a_gate, yb, x0, gate0, pool_bd, vec(pool_scale[0]), ev_w_out[0].astype(BF16))

    ch = np.arange(FN_WIDTH, dtype=np.float64)
    ph = 2.0 * np.pi * np.outer(ch, ch) / FN_WIDTH
    cs = _table_parts(np.concatenate([np.cos(ph), np.sin(ph)], axis=1))
    uc, hy_gate, pq, fn_gate = _odd_in(x1, vec(norm_g[1]), scale1, shift1, od_w_in[0].astype(BF16),
                                       hy_conv_w[0], vec(hy_conv_b[0]), cs)

    max_decay = math.log(HY_DECAY_TARGET) / HY_FAST_DECAY
    min_decay = math.log(HY_DECAY_TARGET) / HY_SLOW_DECAY
    deltas = jnp.linspace(min_decay, max_decay, HY_WIDTH, dtype=F32)[None, :]
    pos = np.arange(n)
    emb2 = jnp.asarray(np.concatenate([_hyena_embedding(n, pos),
                                       _hyena_embedding(n, np.where(pos == 0, 0, n - pos))], axis=-1)
                       .astype(np.float32))
    k_lo, k_hi, mass = _hyena_filters(emb2, hy_w1[0], vec(hy_b1[0]), hy_w2[0], vec(hy_b2[0]), hy_w3[0],
                                      vec(hy_freq[0]), deltas)

    n2 = n // DFT_N1
    m_full, _, big_fwd, big_inv = _dft_tables(2 * n, n2, True, 2 * DFT_N1)
    m_half, m_inv, _, _ = _dft_tables(2 * n, n2, True, DFT_N1)
    k_spec = _dft_fwd([k_lo, k_hi], HY_ORDER * HY_WIDTH, m_full, big_fwd, mass=mass, out_dtype=BF16)
    z1 = _hyena_conv(uc, 0, uc, HY_WIDTH, hy_skip[0, 0:1], k_spec, 0, m_half, m_inv, big_fwd, big_inv)
    yc = _hyena_conv(z1, 0, uc, 2 * HY_WIDTH, hy_skip[0, 1:2], k_spec, 1, m_half, m_inv, big_fwd, big_inv)

    f_half, _, f_big, _ = _dft_tables(n, n2, False, DFT_N1)
    pq_spec = _dft_fwd([pq], 2 * FN_WIDTH, f_half, f_big)
    yd = _fn_out(pq_spec, fn_w[0].astype(BF16), n)

    out = _odd_out(yc, hy_gate, yd, fn_gate, x1, gate1, od_w_out[0].astype(BF16), vec(final_g))
    return out[None]
```

```python
import functools
import math

import numpy as np
import jax
import jax.numpy as jnp
from jax import lax
from jax.experimental import pallas as pl
from jax.experimental.pallas import tpu as pltpu

F32 = jnp.float32
BF16 = jnp.bfloat16
HIGHEST = lax.Precision.HIGHEST

EPS = 1e-6
GRID_W = 64
HEAD_DIM = 128
ROPE_FREQS = 32
ROPE_THETA = 10000.0
N_Q_HEADS = 6
N_KV_HEADS = 2
Q_PER_KV = N_Q_HEADS // N_KV_HEADS
POOL_WIDTH = 256
POOL_GROUP_DIM = 64
POOL_WINDOWS = (2, 4, 8, 16)
POOL_HALO = 8
ATT_WIDTH = N_Q_HEADS * HEAD_DIM
KV_WIDTH = N_KV_HEADS * HEAD_DIM
HY_WIDTH = 768
HY_ORDER = 2
HY_EMB = 33
HY_BANDS = 16
HY_HIDDEN = 64
FN_WIDTH = 256
HY_DECAY_TARGET = 1e-2
HY_FAST_DECAY = 0.3
HY_SLOW_DECAY = 1.5

LANES = 128
ROW_TILE = 1024
DFT_N1 = 128
VMEM_LIMIT = 56 * 1024 * 1024
CONV_VMEM_LIMIT = 62 * 1024 * 1024


def _row_tile(n, pref):
    t = min(pref, n)
    assert n % t == 0
    return t


def _silu(x):
    return x * jax.nn.sigmoid(x)


def _params(sem, vmem=None, flags=None):
    return pltpu.CompilerParams(dimension_semantics=sem, vmem_limit_bytes=vmem, flags=flags)


def _const_spec(shape):
    nd = len(shape)
    return pl.BlockSpec(shape, lambda *_: (0,) * nd, pipeline_mode=pl.Buffered(1))


def _mod_kernel(cond_ref, w_ref, b_ref, o_ref):
    s = _silu(cond_ref[...])
    o_ref[0] = jnp.dot(s, w_ref[0], precision=HIGHEST, preferred_element_type=F32) + b_ref[0]


def _modulation(cond, w_mod, b_mod):
    depth, d, d3 = w_mod.shape
    tn = 1024
    return pl.pallas_call(
        _mod_kernel,
        out_shape=jax.ShapeDtypeStruct((depth, 8, d3), F32),
        grid=(depth, d3 // tn),
        in_specs=[pl.BlockSpec((8, d), lambda i, j: (0, 0)),
                  pl.BlockSpec((1, d, tn), lambda i, j: (i, 0, j)),
                  pl.BlockSpec((1, 1, tn), lambda i, j: (i, 0, j))],
        out_specs=pl.BlockSpec((1, 8, tn), lambda i, j: (i, 0, j)),
        compiler_params=_params(("arbitrary", "arbitrary")),
        name="modulation",
    )(cond, w_mod, b_mod.reshape(depth, 1, d3))


def _time_pitch(n):
    return n // DFT_N1 + 8


def _store_pitched(ref, cols, val, run, pitch, first=0):
    for b in range(val.shape[0] // run):
        r0 = (first + b) * pitch
        ref[r0:r0 + run, cols] = val[b * run:(b + 1) * run]
        ref[r0 + run:r0 + pitch, cols] = jnp.zeros((pitch - run, val.shape[1]), val.dtype)


def _load_pitched(ref, run, pitch):
    return jnp.concatenate([ref[b * pitch:b * pitch + run, :] for b in range(ref.shape[0] // pitch)], axis=0)


def _norm_mod(x, g, scale, shift):
    y = x * lax.rsqrt(jnp.mean(x * x, axis=-1, keepdims=True) + EPS)
    return (y * g) * (1.0 + scale) + shift


def _even_in_kernel(x_ref, g_ref, sc_ref, sh_ref, w_ref, qg_ref, kg_ref, cos_ref, sin_ref,
                    aval_ref, agate_ref, q_ref, k_ref, v_ref, bgate_ref):
    half = x_ref.shape[0] // 2
    for s in range(2):
        rs = slice(s * half, (s + 1) * half)
        h = _norm_mod(x_ref[rs, :], g_ref[...], sc_ref[...], sh_ref[...])
        p = jnp.dot(h.astype(BF16), w_ref[...], preferred_element_type=F32)
        aval_ref[rs, :] = p[:, 0:POOL_WIDTH]
        agate_ref[rs, :] = _silu(p[:, POOL_WIDTH:2 * POOL_WIDTH])
        cos = cos_ref[rs, :]
        sin = sin_ref[rs, :]
        lane = lax.broadcasted_iota(jnp.int32, cos.shape, 1)
        low_half = (lane % (2 * ROPE_FREQS)) < ROPE_FREQS

        def head(xh, g):
            y = xh * lax.rsqrt(jnp.mean(xh * xh, axis=-1, keepdims=True) + EPS) * g
            rot = jnp.where(low_half, pltpu.roll(y, HEAD_DIM - ROPE_FREQS, 1), pltpu.roll(y, ROPE_FREQS, 1))
            return y * cos + rot * sin

        q0 = 2 * POOL_WIDTH
        for hq in range(N_Q_HEADS):
            sl = slice(hq * HEAD_DIM, (hq + 1) * HEAD_DIM)
            q_ref[rs, sl] = head(p[:, q0 + hq * HEAD_DIM:q0 + (hq + 1) * HEAD_DIM], qg_ref[...]).astype(BF16)
        k0 = q0 + ATT_WIDTH
        v0 = k0 + KV_WIDTH
        for hk in range(N_KV_HEADS):
            k_ref[hk, rs, :] = head(p[:, k0 + hk * HEAD_DIM:k0 + (hk + 1) * HEAD_DIM], kg_ref[...]).astype(BF16)
            v_ref[hk, rs, :] = p[:, v0 + hk * HEAD_DIM:v0 + (hk + 1) * HEAD_DIM].astype(BF16)
        b0 = v0 + KV_WIDTH
        bgate_ref[rs, :] = _silu(p[:, b0:b0 + ATT_WIDTH]).astype(bgate_ref.dtype)


def _even_in(x, g, scale, shift, w_bf, qg, kg, cos, sin):
    n, d = x.shape
    n_in = w_bf.shape[1]
    tm = _row_tile(n, ROW_TILE // 2)
    row = lambda w: pl.BlockSpec((tm, w), lambda i: (i, 0))
    vec = lambda w: pl.BlockSpec((1, w), lambda i: (0, 0))
    kv_shape = jax.ShapeDtypeStruct((N_KV_HEADS, n, HEAD_DIM), BF16)
    kv_spec = pl.BlockSpec((N_KV_HEADS, tm, HEAD_DIM), lambda i: (0, i, 0))
    return pl.pallas_call(
        _even_in_kernel,
        out_shape=(jax.ShapeDtypeStruct((n, POOL_WIDTH), F32), jax.ShapeDtypeStruct((n, POOL_WIDTH), F32),
                   jax.ShapeDtypeStruct((n, ATT_WIDTH), BF16), kv_shape, kv_shape,
                   jax.ShapeDtypeStruct((n, ATT_WIDTH), BF16)),
        grid=(n // tm,),
        in_specs=[row(d), vec(d), vec(d), vec(d), _const_spec((d, n_in)), vec(HEAD_DIM), vec(HEAD_DIM),
                  row(HEAD_DIM), row(HEAD_DIM)],
        out_specs=(row(POOL_WIDTH), row(POOL_WIDTH), row(ATT_WIDTH), kv_spec, kv_spec, row(ATT_WIDTH)),
        compiler_params=_params(("parallel",), VMEM_LIMIT),
        name="even_in",
    )(x, g, scale, shift, w_bf, qg, kg, cos, sin)


ATT_UNROLL = 4
ATT_ROW_CHUNK = 16


def _attn_kernel(q_ref, k_ref, v_ref, bg_ref, o_ref, q_sc, s_sc, p_sc, m_sc, al_sc, acc_sc,
                 *, tq, tk, n_kv_tiles):
    c = (HEAD_DIM ** -0.5) * math.log2(math.e)
    rows = Q_PER_KV * tq
    total = N_KV_HEADS * n_kv_tiles
    for h in range(N_KV_HEADS):
        for g in range(Q_PER_KV):
            cs = slice((Q_PER_KV * h + g) * HEAD_DIM, (Q_PER_KV * h + g + 1) * HEAD_DIM)
            q_sc[h, g * tq:(g + 1) * tq, :] = q_ref[:, cs]

    def head_tile(v):
        if isinstance(v, int):
            return v // n_kv_tiles, (v % n_kv_tiles) * tk
        h = v // n_kv_tiles
        return h, pl.multiple_of((v - h * n_kv_tiles) * tk, tk)

    def scores(buf, v):
        h, r0 = head_tile(v)
        s_sc[buf] = lax.dot_general(q_sc[h], k_ref[h, pl.ds(r0, tk), :], (((1,), (1,)), ((), ())),
                                    preferred_element_type=F32)

    def softmax(buf, v):
        h, _ = head_tile(v)
        for r in range(rows // ATT_ROW_CHUNK):
            rs = slice(r * ATT_ROW_CHUNK, (r + 1) * ATT_ROW_CHUNK)
            cols = [s_sc[buf, rs, j * LANES:(j + 1) * LANES] for j in range(tk // LANES)]
            mx = functools.reduce(jnp.maximum, cols)
            m_old = m_sc[h, rs, :]
            m_new = jnp.maximum(m_old, jnp.max(mx, axis=-1, keepdims=True) * c)
            for j, sj in enumerate(cols):
                p_sc[buf, rs, j * LANES:(j + 1) * LANES] = jnp.exp2(sj * c - m_new).astype(BF16)
            al_sc[buf, rs, :] = jnp.exp2(m_old - m_new)
            m_sc[h, rs, :] = m_new

    def accumulate(buf, v):
        h, r0 = head_tile(v)
        v_ext = jnp.concatenate([v_ref[h, pl.ds(r0, tk), :], jnp.ones((tk, HEAD_DIM), BF16)], axis=1)
        pv = jnp.dot(p_sc[buf], v_ext, preferred_element_type=F32)
        al = al_sc[buf]
        for j in range(2):
            js = slice(j * HEAD_DIM, (j + 1) * HEAD_DIM)
            acc_sc[h, :, js] = al * acc_sc[h, :, js] + pv[:, js]

    def slot(v, buf, with_scores=True):
        if with_scores:
            scores(1 - buf, v + 1)
        softmax(buf, v)
        accumulate(buf, v)

    m_sc[...] = jnp.full(m_sc.shape, -1e30, F32)
    acc_sc[...] = jnp.zeros(acc_sc.shape, F32)
    scores(0, 0)

    def body(i, carry):
        for u in range(ATT_UNROLL):
            slot(ATT_UNROLL * i + u, u % 2)
        return carry

    n_full = (total - 1) // ATT_UNROLL
    lax.fori_loop(0, n_full, body, 0)
    for v in range(ATT_UNROLL * n_full, total):
        slot(v, v % 2, with_scores=v + 1 < total)
    for h in range(N_KV_HEADS):
        out = acc_sc[h, :, 0:HEAD_DIM] / acc_sc[h, :, HEAD_DIM:2 * HEAD_DIM]
        for g in range(Q_PER_KV):
            cs = slice((Q_PER_KV * h + g) * HEAD_DIM, (Q_PER_KV * h + g + 1) * HEAD_DIM)
            o_ref[:, cs] = (out[g * tq:(g + 1) * tq] * bg_ref[:, cs]).astype(o_ref.dtype)


ATT_KEY_TILE = 1280


def _attention(q, k_all, v_all, bgate):
    n = q.shape[0]
    nk = k_all.shape[1]
    tq = _row_tile(n, 256)
    tk = max(t for t in range(LANES, ATT_KEY_TILE + 1, LANES) if nk % t == 0)
    rows = Q_PER_KV * tq
    kern = functools.partial(_attn_kernel, tq=tq, tk=tk, n_kv_tiles=nk // tk)
    return pl.pallas_call(
        kern,
        out_shape=jax.ShapeDtypeStruct((n, ATT_WIDTH), BF16),
        grid=(n // tq,),
        in_specs=[pl.BlockSpec((tq, ATT_WIDTH), lambda i: (i, 0)),
                  _const_spec((N_KV_HEADS, nk, HEAD_DIM)), _const_spec((N_KV_HEADS, nk, HEAD_DIM)),
                  pl.BlockSpec((tq, ATT_WIDTH), lambda i: (i, 0))],
        out_specs=pl.BlockSpec((tq, ATT_WIDTH), lambda i: (i, 0)),
        scratch_shapes=[pltpu.VMEM((N_KV_HEADS, rows, HEAD_DIM), BF16),
                        pltpu.VMEM((2, rows, tk), F32), pltpu.VMEM((2, rows, tk), BF16),
                        pltpu.VMEM((N_KV_HEADS, rows, LANES), F32), pltpu.VMEM((2, rows, LANES), F32),
                        pltpu.VMEM((N_KV_HEADS, rows, 2 * HEAD_DIM), F32)],
        compiler_params=_params(("parallel",), VMEM_LIMIT),
        name="attention",
    )(q, k_all, v_all, bgate)


def _even_out_kernel(a_ref, ap_ref, an_ref, ag_ref, yb_ref, x_ref, gate_ref, pw_ref, ps_ref, wo_ref, o_ref,
                     *, tm, n_rows):
    i = pl.program_id(0)
    n_tiles = pl.num_programs(0)
    u = a_ref[...]
    prev = jnp.where(i > 0, ap_ref[...], 0.0)
    nxt = jnp.where(i < n_tiles - 1, an_ref[...], 0.0)
    e = jnp.concatenate([prev, u, nxt], axis=0)
    n = tm + 2 * POOL_HALO
    s2 = e[0:n - 1] + e[1:n]
    s4 = s2[0:n - 3] + s2[2:n - 1]
    s8 = s4[0:n - 7] + s4[4:n - 3]
    s16 = s8[0:n - 15] + s8[8:n - 7]
    lane = lax.broadcasted_iota(jnp.int32, (tm, POOL_WIDTH), 1)
    grp = lane // POOL_GROUP_DIM
    win = jnp.where(grp == 0, s2[7:7 + tm],
                    jnp.where(grp == 1, s4[6:6 + tm], jnp.where(grp == 2, s8[4:4 + tm], s16[0:tm])))
    half = jnp.where(grp == 0, 1, jnp.where(grp == 1, 2, jnp.where(grp == 2, 4, 8)))
    t = i * tm + lax.broadcasted_iota(jnp.int32, (tm, POOL_WIDTH), 0)
    cnt = (jnp.minimum(t + half, n_rows) - jnp.maximum(t - half, 0)).astype(F32)
    d = win / cnt - u
    ya = jnp.dot(d.astype(BF16), pw_ref[...], preferred_element_type=F32) * ps_ref[...]
    ya = ya * ag_ref[...]
    y = jnp.dot(ya.astype(BF16), wo_ref[0:POOL_WIDTH, :], preferred_element_type=F32)
    y = y + jnp.dot(yb_ref[...].astype(BF16), wo_ref[POOL_WIDTH:, :], preferred_element_type=F32)
    o_ref[...] = x_ref[...] + gate_ref[...] * y


def _even_out(a_val, a_gate, yb, x, gate, pool_bd, pool_scale, w_out_bf):
    n, d = x.shape
    tm = _row_tile(n, ROW_TILE)
    hb = tm // POOL_HALO
    last = n // POOL_HALO - 1
    row = lambda w: pl.BlockSpec((tm, w), lambda i: (i, 0))
    vec = lambda w: pl.BlockSpec((1, w), lambda i: (0, 0))
    kern = functools.partial(_even_out_kernel, tm=tm, n_rows=n)
    return pl.pallas_call(
        kern,
        out_shape=jax.ShapeDtypeStruct((n, d), F32),
        grid=(n // tm,),
        in_specs=[row(POOL_WIDTH),
                  pl.BlockSpec((POOL_HALO, POOL_WIDTH), lambda i: (jnp.maximum(i * hb - 1, 0), 0)),
                  pl.BlockSpec((POOL_HALO, POOL_WIDTH), lambda i: (jnp.minimum((i + 1) * hb, last), 0)),
                  row(POOL_WIDTH), row(ATT_WIDTH), row(d), vec(d),
                  _const_spec((POOL_WIDTH, POOL_WIDTH)), vec(POOL_WIDTH), _const_spec((d, d))],
        out_specs=row(d),
        compiler_params=_params(("parallel",), VMEM_LIMIT),
        name="even_out",
    )(a_val, a_val, a_val, a_gate, yb, x, gate, pool_bd, pool_scale, w_out_bf)


def _odd_in_kernel(x_ref, xp_ref, xn_ref, g_ref, sc_ref, sh_ref, w_ref, cw_ref, cb_ref, cs_ref,
                   uc_ref, hyg_ref, pq_ref, fng_ref, *, tm, run, pitch):
    i = pl.program_id(0)
    n_tiles = pl.num_programs(0)
    g, sc, sh = g_ref[...], sc_ref[...], sh_ref[...]
    c0 = (HY_ORDER + 1) * HY_WIDTH
    halo = _norm_mod(jnp.concatenate([xp_ref[...], xn_ref[...]], axis=0), g, sc, sh)
    ph = jnp.dot(halo.astype(BF16), w_ref[:, 0:c0], preferred_element_type=F32)
    half = tm // 2
    ps = [jnp.dot(_norm_mod(x_ref[s * half:(s + 1) * half, :], g, sc, sh).astype(BF16), w_ref[...],
                  preferred_element_type=F32) for s in range(2)]
    edge = [jnp.where(i > 0, ph[7:8, :], 0.0), ps[0][half - 1:half, 0:c0], ps[1][0:1, 0:c0],
            jnp.where(i < n_tiles - 1, ph[8:9, :], 0.0)]
    for s, p in enumerate(ps):
        rs = slice(s * half, (s + 1) * half)
        u = p[:, 0:c0]
        e = jnp.concatenate([edge[s], u, edge[s + 2]], axis=0)
        uc = e[0:half] * cw_ref[0:1, :] + u * cw_ref[1:2, :] + e[2:half + 2] * cw_ref[2:3, :] + cb_ref[...]
        _store_pitched(uc_ref, slice(None), uc, run, pitch, first=s * (half // run))
        hyg_ref[rs, :] = _silu(p[:, c0:c0 + HY_WIDTH]).astype(hyg_ref.dtype)
        fn_in = p[:, c0 + HY_WIDTH:c0 + HY_WIDTH + FN_WIDTH]
        _store_pitched(pq_ref, slice(None), _dot_parts(_split_bf16(fn_in), _ref_parts(cs_ref)), run, pitch,
                       first=s * (half // run))
        fng_ref[rs, :] = _silu(p[:, c0 + HY_WIDTH + FN_WIDTH:])


def _odd_in(x, g, scale, shift, w_bf, conv_w, conv_b, cs):
    n, d = x.shape
    n_in = w_bf.shape[1]
    tm = _row_tile(n, ROW_TILE // 2)
    hb = tm // 8
    last = n // 8 - 1
    row = lambda w: pl.BlockSpec((tm, w), lambda i: (i, 0))
    vec = lambda w: pl.BlockSpec((1, w), lambda i: (0, 0))
    c0 = (HY_ORDER + 1) * HY_WIDTH
    run, pitch = n // DFT_N1, _time_pitch(n)
    assert tm % run == 0
    tmp = tm // run * pitch
    prow = lambda w: pl.BlockSpec((tmp, w), lambda i: (i, 0))
    kern = functools.partial(_odd_in_kernel, tm=tm, run=run, pitch=pitch)
    return pl.pallas_call(
        kern,
        out_shape=(jax.ShapeDtypeStruct((DFT_N1 * pitch, c0), F32), jax.ShapeDtypeStruct((n, HY_WIDTH), BF16),
                   jax.ShapeDtypeStruct((DFT_N1 * pitch, 2 * FN_WIDTH), F32),
                   jax.ShapeDtypeStruct((n, FN_WIDTH), F32)),
        grid=(n // tm,),
        in_specs=[row(d),
                  pl.BlockSpec((8, d), lambda i: (jnp.maximum(i * hb - 1, 0), 0)),
                  pl.BlockSpec((8, d), lambda i: (jnp.minimum((i + 1) * hb, last), 0)),
                  vec(d), vec(d), vec(d), _const_spec((d, n_in)),
                  pl.BlockSpec((3, c0), lambda i: (0, 0)), vec(c0), _const_spec(cs.shape)],
        out_specs=(prow(c0), row(HY_WIDTH), prow(2 * FN_WIDTH), row(FN_WIDTH)),
        compiler_params=_params(("parallel",), VMEM_LIMIT),
        name="odd_in",
    )(x, x, x, g, scale, shift, w_bf, conv_w, conv_b, cs)


def _filter_kernel(emb_ref, w1_ref, b1_ref, w2_ref, b2_ref, w3f_ref, w3b_ref, fr_ref, dl_ref,
                   lo_ref, hi_ref, mass_ref, *, tm, n_rows, run, pitch):
    i = pl.program_id(0)
    fr = fr_ref[...]
    a = jnp.sin(fr * (jnp.dot(emb_ref[...], w1_ref[...], precision=HIGHEST, preferred_element_type=F32)
                      + b1_ref[...]))
    a = jnp.sin(fr * (jnp.dot(a, w2_ref[...], precision=HIGHEST, preferred_element_type=F32) + b2_ref[...]))
    a_f = a_b = a.astype(BF16)
    j = i * tm + lax.broadcasted_iota(jnp.int32, (tm, 1), 0)
    pos_b = jnp.where(j == 0, 0, n_rows - j)
    dl = jnp.abs(dl_ref[...])
    inv = 1.0 / (n_rows - 1)
    decay_f = jnp.exp(-(j.astype(F32) * inv) * dl)
    decay_b = jnp.exp(-(pos_b.astype(F32) * inv) * dl)

    @pl.when(i == 0)
    def _():
        mass_ref[...] = jnp.zeros_like(mass_ref)

    for o in range(HY_ORDER):
        os_ = slice(o * HY_WIDTH, (o + 1) * HY_WIDTH)
        hf = jnp.dot(a_f, w3f_ref[:, os_], preferred_element_type=F32) * decay_f
        hb = jnp.dot(a_b, w3b_ref[:, os_], preferred_element_type=F32) * decay_b
        _store_pitched(lo_ref, os_, hf, run, pitch)
        _store_pitched(hi_ref, os_, jnp.where(j == 0, 0.0, -hb), run, pitch)
        mass_ref[:, os_] += (jnp.sum(jnp.abs(hf), axis=0, keepdims=True)
                             + jnp.sum(jnp.abs(hb), axis=0, keepdims=True))


def _hyena_filters(emb2, w1, b1, w2, b2, w3, freq, deltas):
    n = emb2.shape[0]
    tm = _row_tile(n, ROW_TILE)
    nh = HY_ORDER * HY_WIDTH
    hh = HY_HIDDEN
    w1b = jnp.zeros((LANES, 2 * hh), F32).at[0:HY_EMB, 0:hh].set(w1).at[hh:hh + HY_EMB, hh:].set(w1)
    w2b = jnp.zeros((2 * hh, 2 * hh), F32).at[0:hh, 0:hh].set(w2).at[hh:, hh:].set(w2)
    w3r = w3.reshape(hh, HY_ORDER, 2, HY_WIDTH)
    zero = jnp.zeros((hh, nh), F32)
    w3f = jnp.concatenate([w3r[:, :, 0, :].reshape(hh, nh), zero], axis=0).astype(BF16)
    w3b = jnp.concatenate([zero, w3r[:, :, 1, :].reshape(hh, nh)], axis=0).astype(BF16)
    twice = lambda a: jnp.concatenate([a, a], axis=-1)
    ops = (emb2, w1b, twice(b1), w2b, twice(b2), w3f, w3b, twice(freq), deltas)
    run, pitch = n // DFT_N1, _time_pitch(n)
    assert tm % run == 0
    prow = pl.BlockSpec((tm // run * pitch, nh), lambda i: (i, 0))
    kern = functools.partial(_filter_kernel, tm=tm, n_rows=n, run=run, pitch=pitch)
    full = lambda a: pl.BlockSpec(a.shape, lambda i: (0,) * a.ndim)
    row = lambda w: pl.BlockSpec((tm, w), lambda i: (i, 0))
    return pl.pallas_call(
        kern,
        out_shape=(jax.ShapeDtypeStruct((DFT_N1 * pitch, nh), F32), jax.ShapeDtypeStruct((DFT_N1 * pitch, nh), F32),
                   jax.ShapeDtypeStruct((1, nh), F32)),
        grid=(n // tm,),
        in_specs=[row(LANES)] + [full(a) for a in ops[1:]],
        out_specs=(prow, prow, pl.BlockSpec((1, nh), lambda i: (0, 0))),
        compiler_params=_params(("arbitrary",), VMEM_LIMIT),
        name="hyena_filters",
    )(*ops)


DFT_PARTS = 1


def _table_parts(a):
    a32 = jnp.asarray(a.astype(np.float32))
    hi = a32.astype(BF16)
    if DFT_PARTS == 1:
        return hi[None]
    lo = (a32 - hi.astype(F32)).astype(BF16)
    return jnp.stack([hi, lo])


def _split_bf16(x):
    hi = x.astype(BF16)
    if DFT_PARTS == 1:
        return (hi,)
    return (hi, (x - hi.astype(F32)).astype(BF16))


def _ref_parts(ref, *idx):
    return tuple(ref[(q,) + idx] for q in range(ref.shape[0]))


def _dot_parts(lhs, rhs):
    acc = jnp.dot(lhs[0], rhs[0], preferred_element_type=F32)
    if len(lhs) > 1:
        acc = acc + jnp.dot(lhs[1], rhs[0], preferred_element_type=F32)
    if len(rhs) > 1:
        acc = acc + jnp.dot(lhs[0], rhs[1], preferred_element_type=F32)
    return acc


def _dft_tables(n_total, n2, half_shift, n1_used):
    n1_full = n_total // n2
    sh = 0.5 if half_shift else 0.0
    k1 = np.arange(DFT_N1, dtype=np.float64)[None, :, None] + sh
    n1 = np.arange(n1_used, dtype=np.float64)[None, None, :]
    nn2 = np.arange(n2, dtype=np.float64)[:, None, None]
    ph = -2.0 * np.pi * (n1 * k1 / n1_full + nn2 * k1 / n_total)
    fwd = np.stack([np.cos(ph), np.sin(ph)], axis=2).reshape(n2, 2 * DFT_N1, n1_used)
    ph_i = ph[:, :, :DFT_N1]
    inv = np.stack([np.cos(ph_i), np.sin(ph_i)], axis=2).reshape(n2, 2 * DFT_N1, DFT_N1)
    inv = (2.0 / n_total) * inv.transpose(0, 2, 1)
    k2 = np.arange(n2, dtype=np.float64)
    ph2 = -2.0 * np.pi * np.outer(k2, k2) / n2
    fr, fi = np.cos(ph2), np.sin(ph2)
    big_fwd = np.block([[fr, -fi], [fi, fr]])
    big_inv = np.block([[fr, fi], [-fi, fr]])
    return _table_parts(fwd), _table_parts(inv), _table_parts(big_fwd), _table_parts(big_inv)


DFT_STEP = 32
DFT_FWD_STEP = 64


def _stage1_fwd(x_refs, m_ref, a_sc, step, *, pitch):
    for u in range(m_ref.shape[1]):
        nn = step * m_ref.shape[1] + u
        xs = [xr[pl.ds(nn, DFT_N1, stride=pitch), :] for xr in x_refs]
        xs = xs[0] if len(xs) == 1 else jnp.concatenate(xs, axis=0)
        a_sc[pl.ds(nn, 2 * DFT_N1, stride=pitch), :] = _dot_parts(_ref_parts(m_ref, u), _split_bf16(xs))


def _load_cplx(a_sc, k1, *, n2, pitch):
    r0 = pl.multiple_of(k1 * 2 * pitch, 8)
    return jnp.concatenate([a_sc[pl.ds(r0, n2), :], a_sc[pl.ds(r0 + pitch, n2), :]], axis=0)


def _load_cplx_pair(a_sc, k1, *, n2, pitch):
    return jnp.concatenate([_load_cplx(a_sc, k1, n2=n2, pitch=pitch),
                            _load_cplx(a_sc, k1 + 1, n2=n2, pitch=pitch)], axis=1)


def _phase_steps(n2, step=None):
    step = step or DFT_STEP
    n2g = min(step, n2)
    return n2g, n2 // n2g, DFT_N1 // step


def _dft_fwd_kernel(*refs, n_x, n2, pitch, ja, step, scaled):
    x_refs = refs[:n_x]
    rest = refs[n_x:]
    if scaled:
        mass_ref, rest = rest[0], rest[1:]
    m_ref, big_ref, o_ref, a_sc = rest
    j = pl.program_id(1)

    @pl.when(j < ja)
    def _():
        _stage1_fwd(x_refs, m_ref, a_sc, j, pitch=pitch)

    @pl.when(j >= ja)
    def _():
        big = _ref_parts(big_ref)
        wt = o_ref.shape[1]
        for g in range(0, step, 2):
            blk = _load_cplx_pair(a_sc, (j - ja) * step + g, n2=n2, pitch=pitch)
            out = _dot_parts(big, _split_bf16(blk))
            for q in range(2):
                oq = out[:, q * wt:(q + 1) * wt]
                if scaled:
                    oq = oq * (1.0 / (mass_ref[...] + EPS))
                o_ref[(g + q) * 2 * n2:(g + q + 1) * 2 * n2, :] = oq.astype(o_ref.dtype)


def _dft_fwd(xs, width, m_fwd, big, mass=None, out_dtype=F32):
    n2 = xs[0].shape[0] // DFT_N1 - 8
    pitch = n2 + 8
    n = DFT_N1 * pitch
    n2g, ja, jb = _phase_steps(n2, DFT_FWD_STEP)
    wt = LANES
    rows = 2 * DFT_N1 * n2
    kern = functools.partial(_dft_fwd_kernel, n_x=len(xs), n2=n2, pitch=pitch, ja=ja, step=DFT_FWD_STEP,
                             scaled=mass is not None)
    x_spec = pl.BlockSpec((n, wt), lambda i, j: (0, i), pipeline_mode=pl.Buffered(1))
    ops = list(xs)
    specs = [x_spec] * len(xs)
    if mass is not None:
        ops.append(mass)
        specs.append(pl.BlockSpec((1, wt), lambda i, j: (0, i)))
    m_spec = pl.BlockSpec((m_fwd.shape[0], n2g) + m_fwd.shape[2:],
                          lambda i, j: (0, jnp.minimum(j, ja - 1), 0, 0))
    return pl.pallas_call(
        kern,
        out_shape=jax.ShapeDtypeStruct((rows, width), out_dtype),
        grid=(width // wt, ja + jb),
        in_specs=specs + [m_spec, _const_spec(big.shape)],
        out_specs=pl.BlockSpec((DFT_FWD_STEP * 2 * n2, wt), lambda i, j: (jnp.maximum(j - ja, 0), i)),
        scratch_shapes=[pltpu.VMEM((2 * DFT_N1 * pitch, wt), F32)],
        compiler_params=_params(("parallel", "arbitrary"), CONV_VMEM_LIMIT),
        name="dft_fwd",
    )(*ops, m_fwd, big)


def _hyena_conv_kernel(x_ref, g_ref, s_ref, k_ref, m_ref, minv_ref, bigf_ref, bigi_ref, y_ref, a_sc,
                       *, n2, pitch, ja, jb):
    j = pl.program_id(1)

    @pl.when(j < ja)
    def _():
        _stage1_fwd([x_ref], m_ref, a_sc, j, pitch=pitch)

    @pl.when((j >= ja) & (j < ja + jb))
    def _():
        bigf = _ref_parts(bigf_ref)
        bigi = _ref_parts(bigi_ref)
        wt = k_ref.shape[1]
        for g in range(0, DFT_STEP, 2):
            k1 = (j - ja) * DFT_STEP + g
            z = _dot_parts(bigf, _split_bf16(_load_cplx_pair(a_sc, k1, n2=n2, pitch=pitch)))
            zr, zi = z[0:n2], z[n2:2 * n2]
            kr = jnp.concatenate([k_ref[(g + q) * 2 * n2:(g + q) * 2 * n2 + n2, :] for q in range(2)],
                                 axis=1).astype(F32)
            ki = jnp.concatenate([k_ref[(g + q) * 2 * n2 + n2:(g + q + 1) * 2 * n2, :] for q in range(2)],
                                 axis=1).astype(F32)
            prod = jnp.concatenate([zr * kr - zi * ki, zr * ki + zi * kr], axis=0)
            b = _dot_parts(bigi, _split_bf16(prod))
            for q in range(2):
                r0 = pl.multiple_of((k1 + q) * 2 * pitch, 8)
                a_sc[pl.ds(r0, n2), :] = b[0:n2, q * wt:(q + 1) * wt]
                a_sc[pl.ds(r0 + pitch, n2), :] = b[n2:2 * n2, q * wt:(q + 1) * wt]

    @pl.when(j == ja + jb)
    def _():
        for r in range(pitch - n2):
            y_ref[pl.ds(n2 + r, DFT_N1, stride=pitch), :] = jnp.zeros((DFT_N1, y_ref.shape[1]), F32)

    @pl.when(j >= ja + jb)
    def _():
        skip = s_ref[...]
        for u in range(minv_ref.shape[1]):
            nn = (j - ja - jb) * minv_ref.shape[1] + u
            bs = a_sc[pl.ds(nn, 2 * DFT_N1, stride=pitch), :]
            conv = _dot_parts(_ref_parts(minv_ref, u), _split_bf16(bs))
            rows = pl.ds(nn, DFT_N1, stride=pitch)
            y_ref[rows, :] = g_ref[rows, :] * (conv + skip * x_ref[rows, :])


def _hyena_conv(x, x_col, gate, gate_col, skip, kspec, order, m_fwd, m_inv, big_fwd, big_inv):
    n2 = x.shape[0] // DFT_N1 - 8
    pitch = n2 + 8
    n = DFT_N1 * pitch
    n2g, ja, jb = _phase_steps(n2)
    wt = LANES
    c0 = x_col // wt
    cg = gate_col // wt
    ck = order * HY_WIDTH // wt
    kern = functools.partial(_hyena_conv_kernel, n2=n2, pitch=pitch, ja=ja, jb=jb)
    return pl.pallas_call(
        kern,
        out_shape=jax.ShapeDtypeStruct((n, HY_WIDTH), F32),
        grid=(HY_WIDTH // wt, ja + jb + ja),
        in_specs=[pl.BlockSpec((n, wt), lambda i, j: (0, c0 + i), pipeline_mode=pl.Buffered(1)),
                  pl.BlockSpec((n, wt), lambda i, j: (0, cg + i), pipeline_mode=pl.Buffered(1)),
                  pl.BlockSpec((1, wt), lambda i, j: (0, i)),
                  pl.BlockSpec((DFT_STEP * 2 * n2, wt), lambda i, j: (jnp.clip(j - ja, 0, jb - 1), ck + i)),
                  pl.BlockSpec((m_fwd.shape[0], n2g) + m_fwd.shape[2:],
                               lambda i, j: (0, jnp.minimum(j, ja - 1), 0, 0)),
                  pl.BlockSpec((m_inv.shape[0], n2g) + m_inv.shape[2:],
                               lambda i, j: (0, jnp.clip(j - ja - jb, 0, ja - 1), 0, 0)),
                  _const_spec(big_fwd.shape), _const_spec(big_inv.shape)],
        out_specs=pl.BlockSpec((n, wt), lambda i, j: (0, i), pipeline_mode=pl.Buffered(1)),
        scratch_shapes=[pltpu.VMEM((2 * DFT_N1 * pitch, wt), F32)],
        compiler_params=_params(("parallel", "arbitrary"), CONV_VMEM_LIMIT),
        name="hyena_conv",
    )(x, gate, skip, kspec, m_fwd, m_inv, big_fwd, big_inv)


def _fn_out_kernel(p_ref, q_ref, w_ref, o_ref, *, kb, norm):
    w = w_ref[...]
    for jj in range(kb):
        r = (p_ref[:, 0, jj, :] + q_ref[:, 0, jj, :]) * norm
        o_ref[jj] = jnp.dot(r.astype(BF16), w, preferred_element_type=F32)


def _fn_out(spec, fn_w_bf, n):
    n2 = n // DFT_N1
    c = FN_WIDTH
    kb = 8
    s4 = spec.reshape(DFT_N1, 2, n2, 2 * c)
    kern = functools.partial(_fn_out_kernel, kb=kb, norm=1.0 / math.sqrt(n * c))
    out = pl.pallas_call(
        kern,
        out_shape=jax.ShapeDtypeStruct((n2, DFT_N1, c), F32),
        grid=(n2 // kb,),
        in_specs=[pl.BlockSpec((DFT_N1, 1, kb, c), lambda i: (0, 0, i, 0)),
                  pl.BlockSpec((DFT_N1, 1, kb, c), lambda i: (0, 1, i, 1)),
                  _const_spec((c, c))],
        out_specs=pl.BlockSpec((kb, DFT_N1, c), lambda i: (i, 0, 0)),
        compiler_params=_params(("parallel",), VMEM_LIMIT),
        name="fn_out",
    )(s4, s4, fn_w_bf)
    return out.reshape(n, c)


def _odd_out_kernel(yc_ref, hyg_ref, yd_ref, fng_ref, x_ref, gate_ref, wo_ref, fg_ref, o_ref, *, run, pitch):
    yc = _load_pitched(yc_ref, run, pitch)
    y = jnp.dot((yc * hyg_ref[...]).astype(BF16), wo_ref[0:HY_WIDTH, :], preferred_element_type=F32)
    y = y + jnp.dot((yd_ref[...] * fng_ref[...]).astype(BF16), wo_ref[HY_WIDTH:, :],
                    preferred_element_type=F32)
    xo = x_ref[...] + gate_ref[...] * y
    o_ref[...] = xo * lax.rsqrt(jnp.mean(xo * xo, axis=-1, keepdims=True) + EPS) * fg_ref[...]


def _odd_out(yc, hy_gate, yd, fn_gate, x, gate, w_out_bf, final_g):
    n, d = x.shape
    tm = _row_tile(n, ROW_TILE)
    run, pitch = n // DFT_N1, _time_pitch(n)
    row = lambda w: pl.BlockSpec((tm, w), lambda i: (i, 0))
    vec = lambda w: pl.BlockSpec((1, w), lambda i: (0, 0))
    kern = functools.partial(_odd_out_kernel, run=run, pitch=pitch)
    return pl.pallas_call(
        kern,
        out_shape=jax.ShapeDtypeStruct((n, d), F32),
        grid=(n // tm,),
        in_specs=[pl.BlockSpec((tm // run * pitch, HY_WIDTH), lambda i: (i, 0)),
                  row(HY_WIDTH), row(FN_WIDTH), row(FN_WIDTH), row(d), vec(d), _const_spec((d, d)), vec(d)],
        out_specs=row(d),
        compiler_params=_params(("parallel",), VMEM_LIMIT),
        name="odd_out",
    )(yc, hy_gate, yd, fn_gate, x, gate, w_out_bf, final_g)


def _rope_tables(n):
    rows = n // GRID_W
    inv_freq = ROPE_THETA ** (-np.arange(ROPE_FREQS, dtype=np.float64) / ROPE_FREQS)
    ang_r = np.arange(rows, dtype=np.float64)[:, None] * inv_freq
    ang_c = np.arange(GRID_W, dtype=np.float64)[:, None] * inv_freq

    def table(fn, sign):
        r = np.broadcast_to(fn(ang_r)[:, None, :], (rows, GRID_W, ROPE_FREQS))
        c = np.broadcast_to(fn(ang_c)[None, :, :], (rows, GRID_W, ROPE_FREQS))
        return jnp.asarray(np.concatenate([sign * r, r, sign * c, c], axis=-1).reshape(n, HEAD_DIM)
                           .astype(np.float32))

    return table(np.cos, 1.0), table(np.sin, -1.0)


def _hyena_embedding(n, pos):
    pos = pos.astype(np.float64)
    t = (pos / (n - 1))[:, None]
    w = 2.0 * np.pi * pos[:, None] / n
    f = np.linspace(1e-4, HY_BANDS - 1, HY_BANDS)[None, :]
    emb = np.concatenate([t, np.cos(f * w), -np.sin(f * w)], axis=-1)
    return np.pad(emb, ((0, 0), (0, HY_HIDDEN - HY_EMB)))


def kernel(x, c, ctx, c_ctx, w_mod, b_mod, norm_g, ev_w_in, ev_w_out, pool_w, pool_scale, q_norm_g, k_norm_g,
           od_w_in, od_w_out, hy_conv_w, hy_conv_b, hy_w1, hy_b1, hy_w2, hy_b2, hy_w3, hy_freq, hy_skip, fn_w,
           final_g):
    n, d = x.shape[1], x.shape[2]
    nc = ctx.shape[1]
    x0 = x[0]
    ctx0 = ctx[0]
    vec = lambda a: a.reshape(1, -1)

    cond = jnp.zeros((8, d), F32).at[0].set(c[0]).at[1].set(c_ctx)
    mod = _modulation(cond, w_mod, b_mod)
    shift0, scale0, gate0 = (mod[0, 0:1, k * d:(k + 1) * d] for k in range(3))
    cshift0, cscale0 = mod[0, 1:2, 0:d], mod[0, 1:2, d:2 * d]
    shift1, scale1, gate1 = (mod[1, 0:1, k * d:(k + 1) * d] for k in range(3))

    w_in0 = ev_w_in[0].astype(BF16)
    cos, sin = _rope_tables(n)
    g0 = vec(norm_g[0])
    qg, kg = vec(q_norm_g[0]), vec(k_norm_g[0])
    a_val, a_gate, q, k, v, b_gate = _even_in(x0, g0, scale0, shift0, w_in0, qg, kg, cos, sin)
    ones = jnp.ones((nc, HEAD_DIM), F32)
    _, _, _, ck, cv, _ = _even_in(ctx0, g0, cscale0, cshift0, w_in0, qg, kg, ones, jnp.zeros_like(ones))
    yb = _attention(q, jnp.concatenate([ck, k], axis=1), jnp.concatenate([cv, v], axis=1), b_gate)
    pool_bd = jax.scipy.linalg.block_diag(*[pool_w[0, gi] for gi in range(pool_w.shape[1])]).astype(BF16)
    x1 = _even_out(a_val, a_gate, yb, x0, gate0, pool_bd, vec(pool_scale[0]), ev_w_out[0].astype(BF16))

    ch = np.arange(FN_WIDTH, dtype=np.float64)
    ph = 2.0 * np.pi * np.outer(ch, ch) / FN_WIDTH
    cs = _table_parts(np.concatenate([np.cos(ph), np.sin(ph)], axis=1))
    uc, hy_gate, pq, fn_gate = _odd_in(x1, vec(norm_g[1]), scale1, shift1, od_w_in[0].astype(BF16),
                                       hy_conv_w[0], vec(hy_conv_b[0]), cs)

    max_decay = math.log(HY_DECAY_TARGET) / HY_FAST_DECAY
    min_decay = math.log(HY_DECAY_TARGET) / HY_SLOW_DECAY
    deltas = jnp.linspace(min_decay, max_decay, HY_WIDTH, dtype=F32)[None, :]
    pos = np.arange(n)
    emb2 = jnp.asarray(np.concatenate([_hyena_embedding(n, pos),
                                       _hyena_embedding(n, np.where(pos == 0, 0, n - pos))], axis=-1)
                       .astype(np.float32))
    k_lo, k_hi, mass = _hyena_filters(emb2, hy_w1[0], vec(hy_b1[0]), hy_w2[0], vec(hy_b2[0]), hy_w3[0],
                                      vec(hy_freq[0]), deltas)

    n2 = n // DFT_N1
    m_full, _, big_fwd, big_inv = _dft_tables(2 * n, n2, True, 2 * DFT_N1)
    m_half, m_inv, _, _ = _dft_tables(2 * n, n2, True, DFT_N1)
    k_spec = _dft_fwd([k_lo, k_hi], HY_ORDER * HY_WIDTH, m_full, big_fwd, mass=mass, out_dtype=BF16)
    z1 = _hyena_conv(uc, 0, uc, HY_WIDTH, hy_skip[0, 0:1], k_spec, 0, m_half, m_inv, big_fwd, big_inv)
    yc = _hyena_conv(z1, 0, uc, 2 * HY_WIDTH, hy_skip[0, 1:2], k_spec, 1, m_half, m_inv, big_fwd, big_inv)

    f_half, _, f_big, _ = _dft_tables(n, n2, False, DFT_N1)
    pq_spec = _dft_fwd([pq], 2 * FN_WIDTH, f_half, f_big)
    yd = _fn_out(pq_spec, fn_w[0].astype(BF16), n)

    out = _odd_out(yc, hy_gate, yd, fn_gate, x1, gate1, od_w_out[0].astype(BF16), vec(final_g))
    return out[None]
```

```python
import functools
import math

import numpy as np
import jax
import jax.numpy as jnp
from jax import lax
from jax.experimental import pallas as pl
from jax.experimental.pallas import tpu as pltpu

F32 = jnp.float32
BF16 = jnp.bfloat16
HIGHEST = lax.Precision.HIGHEST

EPS = 1e-6
GRID_W = 64
HEAD_DIM = 128
ROPE_FREQS = 32
ROPE_THETA = 10000.0
N_Q_HEADS = 6
N_KV_HEADS = 2
Q_PER_KV = N_Q_HEADS // N_KV_HEADS
POOL_WIDTH = 256
POOL_GROUP_DIM = 64
POOL_WINDOWS = (2, 4, 8, 16)
POOL_HALO = 8
ATT_WIDTH = N_Q_HEADS * HEAD_DIM
KV_WIDTH = N_KV_HEADS * HEAD_DIM
HY_WIDTH = 768
HY_ORDER = 2
HY_EMB = 33
HY_BANDS = 16
HY_HIDDEN = 64
FN_WIDTH = 256
HY_DECAY_TARGET = 1e-2
HY_FAST_DECAY = 0.3
HY_SLOW_DECAY = 1.5

LANES = 128
ROW_TILE = 1024
DFT_N1 = 128
VMEM_LIMIT = 56 * 1024 * 1024
CONV_VMEM_LIMIT = 62 * 1024 * 1024


def _row_tile(n, pref):
    t = min(pref, n)
    assert n % t == 0
    return t


def _silu(x):
    return x * jax.nn.sigmoid(x)


def _params(sem, vmem=None, flags=None):
    return pltpu.CompilerParams(dimension_semantics=sem, vmem_limit_bytes=vmem, flags=flags)


def _const_spec(shape):
    nd = len(shape)
    return pl.BlockSpec(shape, lambda *_: (0,) * nd, pipeline_mode=pl.Buffered(1))


def _mod_kernel(cond_ref, w_ref, b_ref, o_ref):
    s = _silu(cond_ref[...])
    o_ref[0] = jnp.dot(s, w_ref[0], precision=HIGHEST, preferred_element_type=F32) + b_ref[0]


def _modulation(cond, w_mod, b_mod):
    depth, d, d3 = w_mod.shape
    tn = 1024
    return pl.pallas_call(
        _mod_kernel,
        out_shape=jax.ShapeDtypeStruct((depth, 8, d3), F32),
        grid=(depth, d3 // tn),
        in_specs=[pl.BlockSpec((8, d), lambda i, j: (0, 0)),
                  pl.BlockSpec((1, d, tn), lambda i, j: (i, 0, j)),
                  pl.BlockSpec((1, 1, tn), lambda i, j: (i, 0, j))],
        out_specs=pl.BlockSpec((1, 8, tn), lambda i, j: (i, 0, j)),
        compiler_params=_params(("arbitrary", "arbitrary")),
        name="modulation",
    )(cond, w_mod, b_mod.reshape(depth, 1, d3))


def _time_pitch(n):
    return n // DFT_N1 + 8


def _store_pitched(ref, cols, val, run, pitch, first=0):
    for b in range(val.shape[0] // run):
        r0 = (first + b) * pitch
        ref[r0:r0 + run, cols] = val[b * run:(b + 1) * run]
        ref[r0 + run:r0 + pitch, cols] = jnp.zeros((pitch - run, val.shape[1]), val.dtype)


def _load_pitched(ref, run, pitch):
    return jnp.concatenate([ref[b * pitch:b * pitch + run, :] for b in range(ref.shape[0] // pitch)], axis=0)


def _norm_mod(x, g, scale, shift):
    y = x * lax.rsqrt(jnp.mean(x * x, axis=-1, keepdims=True) + EPS)
    return (y * g) * (1.0 + scale) + shift


def _even_in_kernel(x_ref, g_ref, sc_ref, sh_ref, w_ref, qg_ref, kg_ref, cos_ref, sin_ref,
                    aval_ref, agate_ref, q_ref, k_ref, v_ref, bgate_ref, w_sc):
    @pl.when(pl.program_id(0) == 0)
    def _():
        w_sc[...] = w_ref[...].astype(BF16)

    half = x_ref.shape[0] // 2
    for s in range(2):
        rs = slice(s * half, (s + 1) * half)
        h = _norm_mod(x_ref[rs, :], g_ref[...], sc_ref[...], sh_ref[...])
        p = jnp.dot(h.astype(BF16), w_sc[...], preferred_element_type=F32)
        aval_ref[rs, :] = p[:, 0:POOL_WIDTH]
        agate_ref[rs, :] = _silu(p[:, POOL_WIDTH:2 * POOL_WIDTH])
        cos = cos_ref[rs, :]
        sin = sin_ref[rs, :]
        lane = lax.broadcasted_iota(jnp.int32, cos.shape, 1)
        low_half = (lane % (2 * ROPE_FREQS)) < ROPE_FREQS

        def head(xh, g):
            y = xh * lax.rsqrt(jnp.mean(xh * xh, axis=-1, keepdims=True) + EPS) * g
            rot = jnp.where(low_half, pltpu.roll(y, HEAD_DIM - ROPE_FREQS, 1), pltpu.roll(y, ROPE_FREQS, 1))
            return y * cos + rot * sin

        q0 = 2 * POOL_WIDTH
        for hq in range(N_Q_HEADS):
            sl = slice(hq * HEAD_DIM, (hq + 1) * HEAD_DIM)
            q_ref[rs, sl] = head(p[:, q0 + hq * HEAD_DIM:q0 + (hq + 1) * HEAD_DIM], qg_ref[...]).astype(BF16)
        k0 = q0 + ATT_WIDTH
        v0 = k0 + KV_WIDTH
        for hk in range(N_KV_HEADS):
            k_ref[hk, rs, :] = head(p[:, k0 + hk * HEAD_DIM:k0 + (hk + 1) * HEAD_DIM], kg_ref[...]).astype(BF16)
            v_ref[hk, rs, :] = p[:, v0 + hk * HEAD_DIM:v0 + (hk + 1) * HEAD_DIM].astype(BF16)
        b0 = v0 + KV_WIDTH
        bgate_ref[rs, :] = _silu(p[:, b0:b0 + ATT_WIDTH]).astype(bgate_ref.dtype)


def _even_in(x, g, scale, shift, w_bf, qg, kg, cos, sin):
    n, d = x.shape
    n_in = w_bf.shape[1]
    tm = _row_tile(n, ROW_TILE // 2)
    row = lambda w: pl.BlockSpec((tm, w), lambda i: (i, 0))
    vec = lambda w: pl.BlockSpec((1, w), lambda i: (0, 0))
    kv_shape = jax.ShapeDtypeStruct((N_KV_HEADS, n, HEAD_DIM), BF16)
    kv_spec = pl.BlockSpec((N_KV_HEADS, tm, HEAD_DIM), lambda i: (0, i, 0))
    return pl.pallas_call(
        _even_in_kernel,
        out_shape=(jax.ShapeDtypeStruct((n, POOL_WIDTH), F32), jax.ShapeDtypeStruct((n, POOL_WIDTH), F32),
                   jax.ShapeDtypeStruct((n, ATT_WIDTH), BF16), kv_shape, kv_shape,
                   jax.ShapeDtypeStruct((n, ATT_WIDTH), BF16)),
        grid=(n // tm,),
        in_specs=[row(d), vec(d), vec(d), vec(d), _const_spec((d, n_in)), vec(HEAD_DIM), vec(HEAD_DIM),
                  row(HEAD_DIM), row(HEAD_DIM)],
        out_specs=(row(POOL_WIDTH), row(POOL_WIDTH), row(ATT_WIDTH), kv_spec, kv_spec, row(ATT_WIDTH)),
        scratch_shapes=[pltpu.VMEM((d, n_in), BF16)],
        compiler_params=_params(("arbitrary",), VMEM_LIMIT),
        name="even_in",
    )(x, g, scale, shift, w_bf, qg, kg, cos, sin)


ATT_UNROLL = 4
ATT_ROW_CHUNK = 16


def _attn_kernel(q_ref, k_ref, v_ref, bg_ref, o_ref, q_sc, s_sc, p_sc, m_sc, al_sc, acc_sc,
                 *, tq, tk, n_kv_tiles):
    c = (HEAD_DIM ** -0.5) * math.log2(math.e)
    rows = Q_PER_KV * tq
    total = N_KV_HEADS * n_kv_tiles
    for h in range(N_KV_HEADS):
        for g in range(Q_PER_KV):
            cs = slice((Q_PER_KV * h + g) * HEAD_DIM, (Q_PER_KV * h + g + 1) * HEAD_DIM)
            q_sc[h, g * tq:(g + 1) * tq, :] = q_ref[:, cs]

    def head_tile(v):
        if isinstance(v, int):
            return v // n_kv_tiles, (v % n_kv_tiles) * tk
        h = v // n_kv_tiles
        return h, pl.multiple_of((v - h * n_kv_tiles) * tk, tk)

    def scores(buf, v):
        h, r0 = head_tile(v)
        s_sc[buf] = lax.dot_general(q_sc[h], k_ref[h, pl.ds(r0, tk), :], (((1,), (1,)), ((), ())),
                                    preferred_element_type=F32)

    def softmax(buf, v):
        h, _ = head_tile(v)
        for r in range(rows // ATT_ROW_CHUNK):
            rs = slice(r * ATT_ROW_CHUNK, (r + 1) * ATT_ROW_CHUNK)
            cols = [s_sc[buf, rs, j * LANES:(j + 1) * LANES] for j in range(tk // LANES)]
            mx = functools.reduce(jnp.maximum, cols)
            m_old = m_sc[h, rs, :]
            m_new = jnp.maximum(m_old, jnp.max(mx, axis=-1, keepdims=True) * c)
            for j, sj in enumerate(cols):
                p_sc[buf, rs, j * LANES:(j + 1) * LANES] = jnp.exp2(sj * c - m_new).astype(BF16)
            al_sc[buf, rs, :] = jnp.exp2(m_old - m_new)
            m_sc[h, rs, :] = m_new

    def accumulate(buf, v):
        h, r0 = head_tile(v)
        v_ext = jnp.concatenate([v_ref[h, pl.ds(r0, tk), :], jnp.ones((tk, HEAD_DIM), BF16)], axis=1)
        pv = jnp.dot(p_sc[buf], v_ext, preferred_element_type=F32)
        al = al_sc[buf]
        for j in range(2):
            js = slice(j * HEAD_DIM, (j + 1) * HEAD_DIM)
            acc_sc[h, :, js] = al * acc_sc[h, :, js] + pv[:, js]

    def slot(v, buf, with_scores=True):
        if with_scores:
            scores(1 - buf, v + 1)
        softmax(buf, v)
        accumulate(buf, v)

    m_sc[...] = jnp.full(m_sc.shape, -1e30, F32)
    acc_sc[...] = jnp.zeros(acc_sc.shape, F32)
    scores(0, 0)

    def body(i, carry):
        for u in range(ATT_UNROLL):
            slot(ATT_UNROLL * i + u, u % 2)
        return carry

    n_full = (total - 1) // ATT_UNROLL
    lax.fori_loop(0, n_full, body, 0)
    for v in range(ATT_UNROLL * n_full, total):
        slot(v, v % 2, with_scores=v + 1 < total)
    for h in range(N_KV_HEADS):
        out = acc_sc[h, :, 0:HEAD_DIM] / acc_sc[h, :, HEAD_DIM:2 * HEAD_DIM]
        for g in range(Q_PER_KV):
            cs = slice((Q_PER_KV * h + g) * HEAD_DIM, (Q_PER_KV * h + g + 1) * HEAD_DIM)
            o_ref[:, cs] = (out[g * tq:(g + 1) * tq] * bg_ref[:, cs]).astype(o_ref.dtype)


ATT_KEY_TILE = 1280


def _attention(q, k_all, v_all, bgate):
    n = q.shape[0]
    nk = k_all.shape[1]
    tq = _row_tile(n, 256)
    tk = max(t for t in range(LANES, ATT_KEY_TILE + 1, LANES) if nk % t == 0)
    rows = Q_PER_KV * tq
    kern = functools.partial(_attn_kernel, tq=tq, tk=tk, n_kv_tiles=nk // tk)
    return pl.pallas_call(
        kern,
        out_shape=jax.ShapeDtypeStruct((n, ATT_WIDTH), BF16),
        grid=(n // tq,),
        in_specs=[pl.BlockSpec((tq, ATT_WIDTH), lambda i: (i, 0)),
                  _const_spec((N_KV_HEADS, nk, HEAD_DIM)), _const_spec((N_KV_HEADS, nk, HEAD_DIM)),
                  pl.BlockSpec((tq, ATT_WIDTH), lambda i: (i, 0))],
        out_specs=pl.BlockSpec((tq, ATT_WIDTH), lambda i: (i, 0)),
        scratch_shapes=[pltpu.VMEM((N_KV_HEADS, rows, HEAD_DIM), BF16),
                        pltpu.VMEM((2, rows, tk), F32), pltpu.VMEM((2, rows, tk), BF16),
                        pltpu.VMEM((N_KV_HEADS, rows, LANES), F32), pltpu.VMEM((2, rows, LANES), F32),
                        pltpu.VMEM((N_KV_HEADS, rows, 2 * HEAD_DIM), F32)],
        compiler_params=_params(("parallel",), VMEM_LIMIT),
        name="attention",
    )(q, k_all, v_all, bgate)


def _even_out_kernel(a_ref, ap_ref, an_ref, ag_ref, yb_ref, x_ref, gate_ref, pw_ref, ps_ref, wo_ref, o_ref,
                     *, tm, n_rows):
    i = pl.program_id(0)
    n_tiles = pl.num_programs(0)
    u = a_ref[...]
    prev = jnp.where(i > 0, ap_ref[...], 0.0)
    nxt = jnp.where(i < n_tiles - 1, an_ref[...], 0.0)
    e = jnp.concatenate([prev, u, nxt], axis=0)
    n = tm + 2 * POOL_HALO
    s2 = e[0:n - 1] + e[1:n]
    s4 = s2[0:n - 3] + s2[2:n - 1]
    s8 = s4[0:n - 7] + s4[4:n - 3]
    s16 = s8[0:n - 15] + s8[8:n - 7]
    lane = lax.broadcasted_iota(jnp.int32, (tm, POOL_WIDTH), 1)
    grp = lane // POOL_GROUP_DIM
    win = jnp.where(grp == 0, s2[7:7 + tm],
                    jnp.where(grp == 1, s4[6:6 + tm], jnp.where(grp == 2, s8[4:4 + tm], s16[0:tm])))
    half = jnp.where(grp == 0, 1, jnp.where(grp == 1, 2, jnp.where(grp == 2, 4, 8)))
    t = i * tm + lax.broadcasted_iota(jnp.int32, (tm, POOL_WIDTH), 0)
    cnt = (jnp.minimum(t + half, n_rows) - jnp.maximum(t - half, 0)).astype(F32)
    d = win / cnt - u
    ya = jnp.dot(d.astype(BF16), pw_ref[...], preferred_element_type=F32) * ps_ref[...]
    ya = ya * ag_ref[...]
    y = jnp.dot(ya.astype(BF16), wo_ref[0:POOL_WIDTH, :], preferred_element_type=F32)
    y = y + jnp.dot(yb_ref[...].astype(BF16), wo_ref[POOL_WIDTH:, :], preferred_element_type=F32)
    o_ref[...] = x_ref[...] + gate_ref[...] * y


def _even_out(a_val, a_gate, yb, x, gate, pool_bd, pool_scale, w_out_bf):
    n, d = x.shape
    tm = _row_tile(n, ROW_TILE)
    hb = tm // POOL_HALO
    last = n // POOL_HALO - 1
    row = lambda w: pl.BlockSpec((tm, w), lambda i: (i, 0))
    vec = lambda w: pl.BlockSpec((1, w), lambda i: (0, 0))
    kern = functools.partial(_even_out_kernel, tm=tm, n_rows=n)
    return pl.pallas_call(
        kern,
        out_shape=jax.ShapeDtypeStruct((n, d), F32),
        grid=(n // tm,),
        in_specs=[row(POOL_WIDTH),
                  pl.BlockSpec((POOL_HALO, POOL_WIDTH), lambda i: (jnp.maximum(i * hb - 1, 0), 0)),
                  pl.BlockSpec((POOL_HALO, POOL_WIDTH), lambda i: (jnp.minimum((i + 1) * hb, last), 0)),
                  row(POOL_WIDTH), row(ATT_WIDTH), row(d), vec(d),
                  _const_spec((POOL_WIDTH, POOL_WIDTH)), vec(POOL_WIDTH), _const_spec((d, d))],
        out_specs=row(d),
        compiler_params=_params(("parallel",), VMEM_LIMIT),
        name="even_out",
    )(a_val, a_val, a_val, a_gate, yb, x, gate, pool_bd, pool_scale, w_out_bf)


def _odd_in_kernel(x_ref, xp_ref, xn_ref, g_ref, sc_ref, sh_ref, w_ref, cw_ref, cb_ref, cs_ref,
                   uc_ref, hyg_ref, pq_ref, fng_ref, w_sc, *, tm, run, pitch):
    i = pl.program_id(0)

    @pl.when(i == 0)
    def _():
        w_sc[...] = w_ref[...].astype(BF16)

    n_tiles = pl.num_programs(0)
    g, sc, sh = g_ref[...], sc_ref[...], sh_ref[...]
    c0 = (HY_ORDER + 1) * HY_WIDTH
    halo = _norm_mod(jnp.concatenate([xp_ref[...], xn_ref[...]], axis=0), g, sc, sh)
    ph = jnp.dot(halo.astype(BF16), w_sc[:, 0:c0], preferred_element_type=F32)
    half = tm // 2
    ps = [jnp.dot(_norm_mod(x_ref[s * half:(s + 1) * half, :], g, sc, sh).astype(BF16), w_sc[...],
                  preferred_element_type=F32) for s in range(2)]
    edge = [jnp.where(i > 0, ph[7:8, :], 0.0), ps[0][half - 1:half, 0:c0], ps[1][0:1, 0:c0],
            jnp.where(i < n_tiles - 1, ph[8:9, :], 0.0)]
    for s, p in enumerate(ps):
        rs = slice(s * half, (s + 1) * half)
        u = p[:, 0:c0]
        e = jnp.concatenate([edge[s], u, edge[s + 2]], axis=0)
        uc = e[0:half] * cw_ref[0:1, :] + u * cw_ref[1:2, :] + e[2:half + 2] * cw_ref[2:3, :] + cb_ref[...]
        _store_pitched(uc_ref, slice(None), uc, run, pitch, first=s * (half // run))
        hyg_ref[rs, :] = _silu(p[:, c0:c0 + HY_WIDTH]).astype(hyg_ref.dtype)
        fn_in = p[:, c0 + HY_WIDTH:c0 + HY_WIDTH + FN_WIDTH]
        _store_pitched(pq_ref, slice(None), _dot_parts(_split_bf16(fn_in), _ref_parts(cs_ref)), run, pitch,
                       first=s * (half // run))
        fng_ref[rs, :] = _silu(p[:, c0 + HY_WIDTH + FN_WIDTH:])


def _odd_in(x, g, scale, shift, w_bf, conv_w, conv_b, cs):
    n, d = x.shape
    n_in = w_bf.shape[1]
    tm = _row_tile(n, ROW_TILE // 2)
    hb = tm // 8
    last = n // 8 - 1
    row = lambda w: pl.BlockSpec((tm, w), lambda i: (i, 0))
    vec = lambda w: pl.BlockSpec((1, w), lambda i: (0, 0))
    c0 = (HY_ORDER + 1) * HY_WIDTH
    run, pitch = n // DFT_N1, _time_pitch(n)
    assert tm % run == 0
    tmp = tm // run * pitch
    prow = lambda w: pl.BlockSpec((tmp, w), lambda i: (i, 0))
    kern = functools.partial(_odd_in_kernel, tm=tm, run=run, pitch=pitch)
    return pl.pallas_call(
        kern,
        out_shape=(jax.ShapeDtypeStruct((DFT_N1 * pitch, c0), F32), jax.ShapeDtypeStruct((n, HY_WIDTH), BF16),
                   jax.ShapeDtypeStruct((DFT_N1 * pitch, 2 * FN_WIDTH), F32),
                   jax.ShapeDtypeStruct((n, FN_WIDTH), F32)),
        grid=(n // tm,),
        in_specs=[row(d),
                  pl.BlockSpec((8, d), lambda i: (jnp.maximum(i * hb - 1, 0), 0)),
                  pl.BlockSpec((8, d), lambda i: (jnp.minimum((i + 1) * hb, last), 0)),
                  vec(d), vec(d), vec(d), _const_spec((d, n_in)),
                  pl.BlockSpec((3, c0), lambda i: (0, 0)), vec(c0), _const_spec(cs.shape)],
        out_specs=(prow(c0), row(HY_WIDTH), prow(2 * FN_WIDTH), row(FN_WIDTH)),
        scratch_shapes=[pltpu.VMEM((d, n_in), BF16)],
        compiler_params=_params(("arbitrary",), VMEM_LIMIT),
        name="odd_in",
    )(x, x, x, g, scale, shift, w_bf, conv_w, conv_b, cs)


def _filter_kernel(emb_ref, w1_ref, b1_ref, w2_ref, b2_ref, w3f_ref, w3b_ref, fr_ref, dl_ref,
                   lo_ref, hi_ref, mass_ref, *, tm, n_rows, run, pitch):
    i = pl.program_id(0)
    fr = fr_ref[...]
    a = jnp.sin(fr * (jnp.dot(emb_ref[...], w1_ref[...], precision=HIGHEST, preferred_element_type=F32)
                      + b1_ref[...]))
    a = jnp.sin(fr * (jnp.dot(a, w2_ref[...], precision=HIGHEST, preferred_element_type=F32) + b2_ref[...]))
    a_f = a_b = a.astype(BF16)
    j = i * tm + lax.broadcasted_iota(jnp.int32, (tm, 1), 0)
    pos_b = jnp.where(j == 0, 0, n_rows - j)
    dl = jnp.abs(dl_ref[...])
    inv = 1.0 / (n_rows - 1)
    decay_f = jnp.exp(-(j.astype(F32) * inv) * dl)
    decay_b = jnp.exp(-(pos_b.astype(F32) * inv) * dl)

    @pl.when(i == 0)
    def _():
        mass_ref[...] = jnp.zeros_like(mass_ref)

    for o in range(HY_ORDER):
        os_ = slice(o * HY_WIDTH, (o + 1) * HY_WIDTH)
        hf = jnp.dot(a_f, w3f_ref[:, os_], preferred_element_type=F32) * decay_f
        hb = jnp.dot(a_b, w3b_ref[:, os_], preferred_element_type=F32) * decay_b
        _store_pitched(lo_ref, os_, hf, run, pitch)
        _store_pitched(hi_ref, os_, jnp.where(j == 0, 0.0, -hb), run, pitch)
        mass_ref[:, os_] += (jnp.sum(jnp.abs(hf), axis=0, keepdims=True)
                             + jnp.sum(jnp.abs(hb), axis=0, keepdims=True))


def _hyena_filters(emb2, w1, b1, w2, b2, w3, freq, deltas):
    n = emb2.shape[0]
    tm = _row_tile(n, ROW_TILE)
    nh = HY_ORDER * HY_WIDTH
    hh = HY_HIDDEN
    w1b = jnp.zeros((LANES, 2 * hh), F32).at[0:HY_EMB, 0:hh].set(w1).at[hh:hh + HY_EMB, hh:].set(w1)
    w2b = jnp.zeros((2 * hh, 2 * hh), F32).at[0:hh, 0:hh].set(w2).at[hh:, hh:].set(w2)
    w3r = w3.reshape(hh, HY_ORDER, 2, HY_WIDTH)
    zero = jnp.zeros((hh, nh), F32)
    w3f = jnp.concatenate([w3r[:, :, 0, :].reshape(hh, nh), zero], axis=0).astype(BF16)
    w3b = jnp.concatenate([zero, w3r[:, :, 1, :].reshape(hh, nh)], axis=0).astype(BF16)
    twice = lambda a: jnp.concatenate([a, a], axis=-1)
    ops = (emb2, w1b, twice(b1), w2b, twice(b2), w3f, w3b, twice(freq), deltas)
    run, pitch = n // DFT_N1, _time_pitch(n)
    assert tm % run == 0
    prow = pl.BlockSpec((tm // run * pitch, nh), lambda i: (i, 0))
    kern = functools.partial(_filter_kernel, tm=tm, n_rows=n, run=run, pitch=pitch)
    full = lambda a: pl.BlockSpec(a.shape, lambda i: (0,) * a.ndim)
    row = lambda w: pl.BlockSpec((tm, w), lambda i: (i, 0))
    return pl.pallas_call(
        kern,
        out_shape=(jax.ShapeDtypeStruct((DFT_N1 * pitch, nh), F32), jax.ShapeDtypeStruct((DFT_N1 * pitch, nh), F32),
                   jax.ShapeDtypeStruct((1, nh), F32)),
        grid=(n // tm,),
        in_specs=[row(LANES)] + [full(a) for a in ops[1:]],
        out_specs=(prow, prow, pl.BlockSpec((1, nh), lambda i: (0, 0))),
        compiler_params=_params(("arbitrary",), VMEM_LIMIT),
        name="hyena_filters",
    )(*ops)


DFT_PARTS = 1


def _table_parts(a):
    a32 = jnp.asarray(a.astype(np.float32))
    hi = a32.astype(BF16)
    if DFT_PARTS == 1:
        return hi[None]
    lo = (a32 - hi.astype(F32)).astype(BF16)
    return jnp.stack([hi, lo])


def _split_bf16(x):
    hi = x.astype(BF16)
    if DFT_PARTS == 1:
        return (hi,)
    return (hi, (x - hi.astype(F32)).astype(BF16))


def _ref_parts(ref, *idx):
    return tuple(ref[(q,) + idx] for q in range(ref.shape[0]))


def _dot_parts(lhs, rhs):
    acc = jnp.dot(lhs[0], rhs[0], preferred_element_type=F32)
    if len(lhs) > 1:
        acc = acc + jnp.dot(lhs[1], rhs[0], preferred_element_type=F32)
    if len(rhs) > 1:
        acc = acc + jnp.dot(lhs[0], rhs[1], preferred_element_type=F32)
    return acc


def _dft_tables(n_total, n2, half_shift, n1_used):
    n1_full = n_total // n2
    sh = 0.5 if half_shift else 0.0
    k1 = np.arange(DFT_N1, dtype=np.float64)[None, :, None] + sh
    n1 = np.arange(n1_used, dtype=np.float64)[None, None, :]
    nn2 = np.arange(n2, dtype=np.float64)[:, None, None]
    ph = -2.0 * np.pi * (n1 * k1 / n1_full + nn2 * k1 / n_total)
    fwd = np.stack([np.cos(ph), np.sin(ph)], axis=2).reshape(n2, 2 * DFT_N1, n1_used)
    ph_i = ph[:, :, :DFT_N1]
    inv = np.stack([np.cos(ph_i), np.sin(ph_i)], axis=2).reshape(n2, 2 * DFT_N1, DFT_N1)
    inv = (2.0 / n_total) * inv.transpose(0, 2, 1)
    k2 = np.arange(n2, dtype=np.float64)
    ph2 = -2.0 * np.pi * np.outer(k2, k2) / n2
    fr, fi = np.cos(ph2), np.sin(ph2)
    big_fwd = np.block([[fr, -fi], [fi, fr]])
    big_inv = np.block([[fr, fi], [-fi, fr]])
    return _table_parts(fwd), _table_parts(inv), _table_parts(big_fwd), _table_parts(big_inv)


DFT_STEP = 32
DFT_FWD_STEP = 64


def _stage1_fwd(x_refs, m_ref, a_sc, step, *, pitch):
    for u in range(m_ref.shape[1]):
        nn = step * m_ref.shape[1] + u
        xs = [xr[pl.ds(nn, DFT_N1, stride=pitch), :] for xr in x_refs]
        xs = xs[0] if len(xs) == 1 else jnp.concatenate(xs, axis=0)
        a_sc[pl.ds(nn, 2 * DFT_N1, stride=pitch), :] = _dot_parts(_ref_parts(m_ref, u), _split_bf16(xs))


def _load_cplx(a_sc, k1, *, n2, pitch):
    r0 = pl.multiple_of(k1 * 2 * pitch, 8)
    return jnp.concatenate([a_sc[pl.ds(r0, n2), :], a_sc[pl.ds(r0 + pitch, n2), :]], axis=0)


def _load_cplx_pair(a_sc, k1, *, n2, pitch):
    return jnp.concatenate([_load_cplx(a_sc, k1, n2=n2, pitch=pitch),
                            _load_cplx(a_sc, k1 + 1, n2=n2, pitch=pitch)], axis=1)


def _phase_steps(n2, step=None):
    step = step or DFT_STEP
    n2g = min(step, n2)
    return n2g, n2 // n2g, DFT_N1 // step


def _dft_fwd_kernel(*refs, n_x, n2, pitch, ja, step, scaled):
    x_refs = refs[:n_x]
    rest = refs[n_x:]
    if scaled:
        mass_ref, rest = rest[0], rest[1:]
    m_ref, big_ref, o_ref, a_sc = rest
    j = pl.program_id(1)

    @pl.when(j < ja)
    def _():
        _stage1_fwd(x_refs, m_ref, a_sc, j, pitch=pitch)

    @pl.when(j >= ja)
    def _():
        big = _ref_parts(big_ref)
        wt = o_ref.shape[1]
        for g in range(0, step, 2):
            blk = _load_cplx_pair(a_sc, (j - ja) * step + g, n2=n2, pitch=pitch)
            out = _dot_parts(big, _split_bf16(blk))
            for q in range(2):
                oq = out[:, q * wt:(q + 1) * wt]
                if scaled:
                    oq = oq * (1.0 / (mass_ref[...] + EPS))
                o_ref[(g + q) * 2 * n2:(g + q + 1) * 2 * n2, :] = oq.astype(o_ref.dtype)


def _dft_fwd(xs, width, m_fwd, big, mass=None, out_dtype=F32):
    n2 = xs[0].shape[0] // DFT_N1 - 8
    pitch = n2 + 8
    n = DFT_N1 * pitch
    n2g, ja, jb = _phase_steps(n2, DFT_FWD_STEP)
    wt = LANES
    rows = 2 * DFT_N1 * n2
    kern = functools.partial(_dft_fwd_kernel, n_x=len(xs), n2=n2, pitch=pitch, ja=ja, step=DFT_FWD_STEP,
                             scaled=mass is not None)
    x_spec = pl.BlockSpec((n, wt), lambda i, j: (0, i), pipeline_mode=pl.Buffered(1))
    ops = list(xs)
    specs = [x_spec] * len(xs)
    if mass is not None:
        ops.append(mass)
        specs.append(pl.BlockSpec((1, wt), lambda i, j: (0, i)))
    m_spec = pl.BlockSpec((m_fwd.shape[0], n2g) + m_fwd.shape[2:],
                          lambda i, j: (0, jnp.minimum(j, ja - 1), 0, 0))
    return pl.pallas_call(
        kern,
        out_shape=jax.ShapeDtypeStruct((rows, width), out_dtype),
        grid=(width // wt, ja + jb),
        in_specs=specs + [m_spec, _const_spec(big.shape)],
        out_specs=pl.BlockSpec((DFT_FWD_STEP * 2 * n2, wt), lambda i, j: (jnp.maximum(j - ja, 0), i)),
        scratch_shapes=[pltpu.VMEM((2 * DFT_N1 * pitch, wt), F32)],
        compiler_params=_params(("parallel", "arbitrary"), CONV_VMEM_LIMIT),
        name="dft_fwd",
    )(*ops, m_fwd, big)


def _hyena_conv_kernel(x_ref, g_ref, s_ref, k_ref, m_ref, minv_ref, bigf_ref, bigi_ref, y_ref, a_sc,
                       *, n2, pitch, ja, jb):
    j = pl.program_id(1)

    @pl.when(j < ja)
    def _():
        _stage1_fwd([x_ref], m_ref, a_sc, j, pitch=pitch)

    @pl.when((j >= ja) & (j < ja + jb))
    def _():
        bigf = _ref_parts(bigf_ref)
        bigi = _ref_parts(bigi_ref)
        wt = k_ref.shape[1]
        for g in range(0, DFT_STEP, 2):
            k1 = (j - ja) * DFT_STEP + g
            z = _dot_parts(bigf, _split_bf16(_load_cplx_pair(a_sc, k1, n2=n2, pitch=pitch)))
            zr, zi = z[0:n2], z[n2:2 * n2]
            kr = jnp.concatenate([k_ref[(g + q) * 2 * n2:(g + q) * 2 * n2 + n2, :] for q in range(2)],
                                 axis=1).astype(F32)
            ki = jnp.concatenate([k_ref[(g + q) * 2 * n2 + n2:(g + q + 1) * 2 * n2, :] for q in range(2)],
                                 axis=1).astype(F32)
            prod = jnp.concatenate([zr * kr - zi * ki, zr * ki + zi * kr], axis=0)
            b = _dot_parts(bigi, _split_bf16(prod))
            for q in range(2):
                r0 = pl.multiple_of((k1 + q) * 2 * pitch, 8)
                a_sc[pl.ds(r0, n2), :] = b[0:n2, q * wt:(q + 1) * wt]
                a_sc[pl.ds(r0 + pitch, n2), :] = b[n2:2 * n2, q * wt:(q + 1) * wt]

    @pl.when(j == ja + jb)
    def _():
        for r in range(pitch - n2):
            y_ref[pl.ds(n2 + r, DFT_N1, stride=pitch), :] = jnp.zeros((DFT_N1, y_ref.shape[1]), F32)

    @pl.when(j >= ja + jb)
    def _():
        skip = s_ref[...]
        for u in range(minv_ref.shape[1]):
            nn = (j - ja - jb) * minv_ref.shape[1] + u
            bs = a_sc[pl.ds(nn, 2 * DFT_N1, stride=pitch), :]
            conv = _dot_parts(_ref_parts(minv_ref, u), _split_bf16(bs))
            rows = pl.ds(nn, DFT_N1, stride=pitch)
            y_ref[rows, :] = g_ref[rows, :] * (conv + skip * x_ref[rows, :])


def _hyena_conv(x, x_col, gate, gate_col, skip, kspec, order, m_fwd, m_inv, big_fwd, big_inv):
    n2 = x.shape[0] // DFT_N1 - 8
    pitch = n2 + 8
    n = DFT_N1 * pitch
    n2g, ja, jb = _phase_steps(n2)
    wt = LANES
    c0 = x_col // wt
    cg = gate_col // wt
    ck = order * HY_WIDTH // wt
    kern = functools.partial(_hyena_conv_kernel, n2=n2, pitch=pitch, ja=ja, jb=jb)
    return pl.pallas_call(
        kern,
        out_shape=jax.ShapeDtypeStruct((n, HY_WIDTH), F32),
        grid=(HY_WIDTH // wt, ja + jb + ja),
        in_specs=[pl.BlockSpec((n, wt), lambda i, j: (0, c0 + i), pipeline_mode=pl.Buffered(1)),
                  pl.BlockSpec((n, wt), lambda i, j: (0, cg + i), pipeline_mode=pl.Buffered(1)),
                  pl.BlockSpec((1, wt), lambda i, j: (0, i)),
                  pl.BlockSpec((DFT_STEP * 2 * n2, wt), lambda i, j: (jnp.clip(j - ja, 0, jb - 1), ck + i)),
                  pl.BlockSpec((m_fwd.shape[0], n2g) + m_fwd.shape[2:],
                               lambda i, j: (0, jnp.minimum(j, ja - 1), 0, 0)),
                  pl.BlockSpec((m_inv.shape[0], n2g) + m_inv.shape[2:],
                               lambda i, j: (0, jnp.clip(j - ja - jb, 0, ja - 1), 0, 0)),
                  _const_spec(big_fwd.shape), _const_spec(big_inv.shape)],
        out_specs=pl.BlockSpec((n, wt), lambda i, j: (0, i), pipeline_mode=pl.Buffered(1)),
        scratch_shapes=[pltpu.VMEM((2 * DFT_N1 * pitch, wt), F32)],
        compiler_params=_params(("parallel", "arbitrary"), CONV_VMEM_LIMIT),
        name="hyena_conv",
    )(x, gate, skip, kspec, m_fwd, m_inv, big_fwd, big_inv)


def _fn_out_kernel(p_ref, q_ref, w_ref, o_ref, *, kb, norm):
    w = w_ref[...]
    for jj in range(kb):
        r = (p_ref[:, 0, jj, :] + q_ref[:, 0, jj, :]) * norm
        o_ref[jj] = jnp.dot(r.astype(BF16), w, preferred_element_type=F32)


def _fn_out(spec, fn_w_bf, n):
    n2 = n // DFT_N1
    c = FN_WIDTH
    kb = 8
    s4 = spec.reshape(DFT_N1, 2, n2, 2 * c)
    kern = functools.partial(_fn_out_kernel, kb=kb, norm=1.0 / math.sqrt(n * c))
    out = pl.pallas_call(
        kern,
        out_shape=jax.ShapeDtypeStruct((n2, DFT_N1, c), F32),
        grid=(n2 // kb,),
        in_specs=[pl.BlockSpec((DFT_N1, 1, kb, c), lambda i: (0, 0, i, 0)),
                  pl.BlockSpec((DFT_N1, 1, kb, c), lambda i: (0, 1, i, 1)),
                  _const_spec((c, c))],
        out_specs=pl.BlockSpec((kb, DFT_N1, c), lambda i: (i, 0, 0)),
        compiler_params=_params(("parallel",), VMEM_LIMIT),
        name="fn_out",
    )(s4, s4, fn_w_bf)
    return out.reshape(n, c)


def _odd_out_kernel(yc_ref, hyg_ref, yd_ref, fng_ref, x_ref, gate_ref, wo_ref, fg_ref, o_ref, *, run, pitch):
    yc = _load_pitched(yc_ref, run, pitch)
    y = jnp.dot((yc * hyg_ref[...]).astype(BF16), wo_ref[0:HY_WIDTH, :], preferred_element_type=F32)
    y = y + jnp.dot((yd_ref[...] * fng_ref[...]).astype(BF16), wo_ref[HY_WIDTH:, :],
                    preferred_element_type=F32)
    xo = x_ref[...] + gate_ref[...] * y
    o_ref[...] = xo * lax.rsqrt(jnp.mean(xo * xo, axis=-1, keepdims=True) + EPS) * fg_ref[...]


def _odd_out(yc, hy_gate, yd, fn_gate, x, gate, w_out_bf, final_g):
    n, d = x.shape
    tm = _row_tile(n, ROW_TILE)
    run, pitch = n // DFT_N1, _time_pitch(n)
    row = lambda w: pl.BlockSpec((tm, w), lambda i: (i, 0))
    vec = lambda w: pl.BlockSpec((1, w), lambda i: (0, 0))
    kern = functools.partial(_odd_out_kernel, run=run, pitch=pitch)
    return pl.pallas_call(
        kern,
        out_shape=jax.ShapeDtypeStruct((n, d), F32),
        grid=(n // tm,),
        in_specs=[pl.BlockSpec((tm // run * pitch, HY_WIDTH), lambda i: (i, 0)),
                  row(HY_WIDTH), row(FN_WIDTH), row(FN_WIDTH), row(d), vec(d), _const_spec((d, d)), vec(d)],
        out_specs=row(d),
        compiler_params=_params(("parallel",), VMEM_LIMIT),
        name="odd_out",
    )(yc, hy_gate, yd, fn_gate, x, gate, w_out_bf, final_g)


def _rope_tables(n):
    rows = n // GRID_W
    inv_freq = ROPE_THETA ** (-np.arange(ROPE_FREQS, dtype=np.float64) / ROPE_FREQS)
    ang_r = np.arange(rows, dtype=np.float64)[:, None] * inv_freq
    ang_c = np.arange(GRID_W, dtype=np.float64)[:, None] * inv_freq

    def table(fn, sign):
        r = np.broadcast_to(fn(ang_r)[:, None, :], (rows, GRID_W, ROPE_FREQS))
        c = np.broadcast_to(fn(ang_c)[None, :, :], (rows, GRID_W, ROPE_FREQS))
        return jnp.asarray(np.concatenate([sign * r, r, sign * c, c], axis=-1).reshape(n, HEAD_DIM)
                           .astype(np.float32))

    return table(np.cos, 1.0), table(np.sin, -1.0)


def _hyena_embedding(n, pos):
    pos = pos.astype(np.float64)
    t = (pos / (n - 1))[:, None]
    w = 2.0 * np.pi * pos[:, None] / n
    f = np.linspace(1e-4, HY_BANDS - 1, HY_BANDS)[None, :]
    emb = np.concatenate([t, np.cos(f * w), -np.sin(f * w)], axis=-1)
    return np.pad(emb, ((0, 0), (0, HY_HIDDEN - HY_EMB)))


def kernel(x, c, ctx, c_ctx, w_mod, b_mod, norm_g, ev_w_in, ev_w_out, pool_w, pool_scale, q_norm_g, k_norm_g,
           od_w_in, od_w_out, hy_conv_w, hy_conv_b, hy_w1, hy_b1, hy_w2, hy_b2, hy_w3, hy_freq, hy_skip, fn_w,
           final_g):
    n, d = x.shape[1], x.shape[2]
    nc = ctx.shape[1]
    x0 = x[0]
    ctx0 = ctx[0]
    vec = lambda a: a.reshape(1, -1)

    cond = jnp.zeros((8, d), F32).at[0].set(c[0]).at[1].set(c_ctx)
    mod = _modulation(cond, w_mod, b_mod)
    shift0, scale0, gate0 = (mod[0, 0:1, k * d:(k + 1) * d] for k in range(3))
    cshift0, cscale0 = mod[0, 1:2, 0:d], mod[0, 1:2, d:2 * d]
    shift1, scale1, gate1 = (mod[1, 0:1, k * d:(k + 1) * d] for k in range(3))

    w_in0 = ev_w_in[0]
    cos, sin = _rope_tables(n)
    g0 = vec(norm_g[0])
    qg, kg = vec(q_norm_g[0]), vec(k_norm_g[0])
    a_val, a_gate, q, k, v, b_gate = _even_in(x0, g0, scale0, shift0, w_in0, qg, kg, cos, sin)
    ones = jnp.ones((nc, HEAD_DIM), F32)
    _, _, _, ck, cv, _ = _even_in(ctx0, g0, cscale0, cshift0, w_in0, qg, kg, ones, jnp.zeros_like(ones))
    yb = _attention(q, jnp.concatenate([ck, k], axis=1), jnp.concatenate([cv, v], axis=1), b_gate)
    pool_bd = jax.scipy.linalg.block_diag(*[pool_w[0, gi] for gi in range(pool_w.shape[1])]).astype(BF16)
    x1 = _even_out(a_val, a_gate, yb, x0, gate0, pool_bd, vec(pool_scale[0]), ev_w_out[0].astype(BF16))

    ch = np.arange(FN_WIDTH, dtype=np.float64)
    ph = 2.0 * np.pi * np.outer(ch, ch) / FN_WIDTH
    cs = _table_parts(np.concatenate([np.cos(ph), np.sin(ph)], axis=1))
    uc, hy_gate, pq, fn_gate = _odd_in(x1, vec(norm_g[1]), scale1, shift1, od_w_in[0],
                                       hy_conv_w[0], vec(hy_conv_b[0]), cs)

    max_decay = math.log(HY_DECAY_TARGET) / HY_FAST_DECAY
    min_decay = math.log(HY_DECAY_TARGET) / HY_SLOW_DECAY
    deltas = jnp.linspace(min_decay, max_decay, HY_WIDTH, dtype=F32)[None, :]
    pos = np.arange(n)
    emb2 = jnp.asarray(np.concatenate([_hyena_embedding(n, pos),
                                       _hyena_embedding(n, np.where(pos == 0, 0, n - pos))], axis=-1)
                       .astype(np.float32))
    k_lo, k_hi, mass = _hyena_filters(emb2, hy_w1[0], vec(hy_b1[0]), hy_w2[0], vec(hy_b2[0]), hy_w3[0],
                                      vec(hy_freq[0]), deltas)

    n2 = n // DFT_N1
    m_full, _, big_fwd, big_inv = _dft_tables(2 * n, n2, True, 2 * DFT_N1)
    m_half, m_inv, _, _ = _dft_tables(2 * n, n2, True, DFT_N1)
    k_spec = _dft_fwd([k_lo, k_hi], HY_ORDER * HY_WIDTH, m_full, big_fwd, mass=mass, out_dtype=BF16)
    z1 = _hyena_conv(uc, 0, uc, HY_WIDTH, hy_skip[0, 0:1], k_spec, 0, m_half, m_inv, big_fwd, big_inv)
    yc = _hyena_conv(z1, 0, uc, 2 * HY_WIDTH, hy_skip[0, 1:2], k_spec, 1, m_half, m_inv, big_fwd, big_inv)

    f_half, _, f_big, _ = _dft_tables(n, n2, False, DFT_N1)
    pq_spec = _dft_fwd([pq], 2 * FN_WIDTH, f_half, f_big)
    yd = _fn_out(pq_spec, fn_w[0].astype(BF16), n)

    out = _odd_out(yc, hy_gate, yd, fn_gate, x1, gate1, od_w_out[0].astype(BF16), vec(final_g))
    return out[None]
```
